```python
import math
import jax
import jax.numpy as jnp
from jax import lax
import numpy as np

D_MODEL = 1024
BATCH = 16
SEQ = 256
DEPTH = 4
DEC_BATCH = 8
DEC_SEQ = 1024
PAST_LEN = 512

GRID_W = 64
CHUNK = 64
EPS = 1e-6
GLA_H = 4
GLA_DK = 64
GLA_DV = 128
GLA_QK = GLA_H * GLA_DK
GLA_W = GLA_H * GLA_DV
GLA_LR = 16
GLA_TAU = 16.0
S5_GH = 16
S5_W = 512
S5_G = S5_W // S5_GH
S5_P = 64
GDN_H = 4
GDN_DK = 128
GDN_DV = 128
GDN_W = GDN_H * GDN_DV
CONV_K = 3
MIX_W = GLA_W + S5_W + GDN_W
SPLITS = (GLA_QK, GLA_QK, GLA_W, 2 * GLA_LR, GLA_W, S5_W, S5_W, 3 * GDN_W, 2 * GDN_H, 2 * GDN_H, GDN_W)
IN_DIM = 2 * GLA_QK + 2 * GLA_W + 2 * GLA_LR + 2 * S5_W + 4 * GDN_W + 4 * GDN_H

kernel_name = 'hybrid_gla_s5_deltanet_diffusion_step'


def rmsnorm(x, g):
    xf = x.astype(jnp.float32)
    return xf * lax.rsqrt(jnp.mean(xf * xf, axis=-1, keepdims=True) + EPS) * g.astype(jnp.float32)


def l2norm(x):
    return x * lax.rsqrt(jnp.sum(x * x, axis=-1, keepdims=True) + EPS)


def to_heads(t, n):
    b, l, _ = t.shape
    return t.reshape(b, l, n, -1).transpose(0, 2, 1, 3)


def from_heads(t):
    b, n, l, d = t.shape
    return t.transpose(0, 2, 1, 3).reshape(b, l, n * d)


def flip_seq(t):
    return jnp.flip(t, axis=2)


def to_chunks(t):
    b, h, l = t.shape[:3]
    return jnp.moveaxis(t.reshape(b, h, l // CHUNK, CHUNK, *t.shape[3:]), 2, 0)


def from_chunks(t):
    n, b, h, c, d = t.shape
    return jnp.moveaxis(t, 0, 2).reshape(b, h, n * c, d)


def grid_to_cols(t):
    b, l, ch = t.shape
    return t.reshape(b, l // GRID_W, GRID_W, ch).transpose(0, 2, 1, 3).reshape(b, l, ch)


def cols_to_grid(t):
    b, l, ch = t.shape
    return t.reshape(b, GRID_W, l // GRID_W, ch).transpose(0, 2, 1, 3).reshape(b, l, ch)


def centred_conv(x, w):
    pad = CONV_K // 2
    return lax.conv_general_dilated(x, w[:, None, :].astype(x.dtype), (1,), [(pad, pad)],
                                    dimension_numbers=('NWC', 'WIO', 'NWC'),
                                    feature_group_count=x.shape[-1])


def gla_chunked(q, k, v, g, s0):
    q, k, v, g = to_chunks(q), to_chunks(k), to_chunks(v), to_chunks(g)
    b = jnp.cumsum(g, axis=-2)
    qe = q * jnp.exp(b)
    ke = k * jnp.exp(-b)
    lower = jnp.tril(jnp.ones((CHUNK, CHUNK), dtype=bool))
    scores = jnp.where(lower, jnp.einsum('nbhid,nbhjd->nbhij', qe, ke), 0.0)
    o_intra = jnp.einsum('nbhij,nbhjv->nbhiv', scores, v)
    b_last = b[..., -1:, :]
    k_dec = k * jnp.exp(b_last - b)
    decay_last = jnp.exp(b_last[..., 0, :])

    def step(s, inp):
        qe_n, kd_n, v_n, dl_n, oi_n = inp
        o = oi_n + jnp.einsum('bhid,bhdv->bhiv', qe_n, s)
        s = dl_n[..., None] * s + jnp.einsum('bhid,bhiv->bhdv', kd_n, v_n)
        return s, o

    s_fin, o = lax.scan(step, s0, (qe, k_dec, v, decay_last, o_intra))
    return from_chunks(o), s_fin


def gdn_chunked(q, k, v, g, beta, s0):
    q, k, v = to_chunks(q), to_chunks(k), to_chunks(v)
    g, beta = to_chunks(g), to_chunks(beta)
    gc = jnp.cumsum(g, axis=-1)
    lower = jnp.tril(jnp.ones((CHUNK, CHUNK), dtype=bool))
    strict = jnp.tril(jnp.ones((CHUNK, CHUNK), dtype=bool), -1)
    diff = gc[..., :, None] - gc[..., None, :]
    decay = jnp.where(lower, jnp.exp(jnp.where(lower, diff, 0.0)), 0.0)
    kb = k * beta[..., None]
    m = jnp.where(strict, jnp.einsum('nbhid,nbhjd->nbhij', kb, k) * decay, 0.0)
    eye = jnp.eye(CHUNK, dtype=m.dtype)
    t = lax.linalg.triangular_solve(m + eye, jnp.broadcast_to(eye, m.shape), left_side=True, lower=True)
    u = jnp.einsum('nbhij,nbhjv->nbhiv', t, v * beta[..., None])
    w = jnp.einsum('nbhij,nbhjd->nbhid', t, kb * jnp.exp(gc)[..., None])
    a_qk = jnp.einsum('nbhid,nbhjd->nbhij', q, k) * decay
    qg = q * jnp.exp(gc)[..., None]
    k_dec = k * jnp.exp(gc[..., -1:] - gc)[..., None]
    decay_last = jnp.exp(gc[..., -1])

    def step(s, inp):
        u_n, w_n, qg_n, a_n, kd_n, dl_n = inp
        v_new = u_n - jnp.einsum('bhid,bhdv->bhiv', w_n, s)
        o = jnp.einsum('bhid,bhdv->bhiv', qg_n, s) + jnp.einsum('bhij,bhjv->bhiv', a_n, v_new)
        s = dl_n[..., None, None] * s + jnp.einsum('bhid,bhiv->bhdv', kd_n, v_new)
        return s, o

    s_fin, o = lax.scan(step, s0, (u, w, qg, a_qk, k_dec, decay_last))
    return from_chunks(o), s_fin


def _lin_combine(e1, e2):
    a1, b1 = e1
    a2, b2 = e2
    return a1 * a2, a2 * b1 + b2


def s5_scan(u, lam_re, lam_im, log_dt, b_cplx, c_cplx, h0):
    lam = lax.complex(lam_re.astype(jnp.float32), lam_im.astype(jnp.float32))
    lam_bar = jnp.exp(lam * jnp.exp(log_dt.astype(jnp.float32))[:, None])
    b_bar = ((lam_bar - 1.0) / lam)[..., None] * b_cplx
    bu = jnp.einsum('blgh,gph->blgp', u.astype(jnp.complex64), b_bar)
    bu = bu.at[:, 0].add(lam_bar * h0)
    a = jnp.broadcast_to(lam_bar, bu.shape)
    _, xs = lax.associative_scan(_lin_combine, (a, bu), axis=1)
    y = jnp.einsum('blgp,ghp->blgh', xs, c_cplx).real
    return y, xs[:, -1]


def mixer(h, p, l, st_gla, st_s5, st_gdn, latent):
    f32 = jnp.float32
    bsz, n, _ = h.shape
    idx = [int(i) for i in np.cumsum(SPLITS)[:-1]]
    proj = jnp.einsum('bld,de->ble', h, p['w_in'][l].astype(f32))
    gq, gk, gv, glr, ggate, su, sgate, dqkv, da, db, dgate = jnp.split(proj, idx, axis=-1)

    q = to_heads(gq, GLA_H) * GLA_DK ** -0.5
    k = to_heads(gk, GLA_H)
    v = to_heads(gv, GLA_H)
    glog = jax.nn.log_sigmoid(
        jnp.einsum('blsr,srk->blsk', glr.reshape(bsz, n, 2, GLA_LR), p['gla_gate_w'][l].astype(f32))
        + p['gla_gate_b'][l].astype(f32)) / GLA_TAU
    o_f, sg_f = gla_chunked(q, k, v, to_heads(glog[:, :, 0], GLA_H), st_gla[:, 0])
    o_b, sg_b = gla_chunked(flip_seq(q), flip_seq(k), flip_seq(v),
                            flip_seq(to_heads(glog[:, :, 1], GLA_H)), st_gla[:, 1])
    o_gla = from_heads(rmsnorm(o_f + flip_seq(o_b), p['gla_norm'][l])) * jax.nn.silu(ggate)

    u = grid_to_cols(su) if latent else su
    ug = u.reshape(bsz, n, S5_G, S5_GH)
    b_cplx = lax.complex(p['s5_b_re'][l].astype(f32), p['s5_b_im'][l].astype(f32))
    c_cplx = lax.complex(p['s5_c_re'][l].astype(f32), p['s5_c_im'][l].astype(f32))
    y_f, ss_f = s5_scan(ug, p['s5_lam_re'][l, 0], p['s5_lam_im'][l, 0], p['s5_log_dt'][l, 0],
                        b_cplx, c_cplx[0], st_s5[:, 0])
    y_b, ss_b = s5_scan(jnp.flip(ug, axis=1), p['s5_lam_re'][l, 1], p['s5_lam_im'][l, 1],
                        p['s5_log_dt'][l, 1], b_cplx, c_cplx[1], st_s5[:, 1])
    y = (y_f + jnp.flip(y_b, axis=1)).reshape(bsz, n, S5_W) + p['s5_d'][l].astype(f32) * u
    if latent:
        y = cols_to_grid(y)
    y = jax.nn.gelu(y)
    y = y * jax.nn.sigmoid(y @ p['s5_glu_w'][l].astype(f32) + p['s5_glu_b'][l].astype(f32))
    o_s5 = y * jax.nn.silu(sgate)

    w_conv = p['gdn_conv'][l].astype(f32)
    if latent:
        qkv = centred_conv(dqkv.reshape(bsz * (n // GRID_W), GRID_W, 3 * GDN_W), w_conv)
        qkv = qkv.reshape(bsz, n, 3 * GDN_W)
    else:
        qkv = centred_conv(dqkv, w_conv)
    cq, ck, cv = jnp.split(jax.nn.silu(qkv), 3, axis=-1)
    q = l2norm(to_heads(cq, GDN_H)) * GDN_DK ** -0.5
    k = l2norm(to_heads(ck, GDN_H))
    v = to_heads(cv, GDN_H)
    g = -jnp.exp(p['gdn_a_log'][l].astype(f32)) * jax.nn.softplus(
        da.reshape(bsz, n, 2, GDN_H) + p['gdn_dt_bias'][l].astype(f32))
    beta = jax.nn.sigmoid(db.reshape(bsz, n, 2, GDN_H))
    g = g.transpose(2, 0, 3, 1)
    beta = beta.transpose(2, 0, 3, 1)
    o_f, sd_f = gdn_chunked(q, k, v, g[0], beta[0], st_gdn[:, 0])
    o_b, sd_b = gdn_chunked(flip_seq(q), flip_seq(k), flip_seq(v), flip_seq(g[1]), flip_seq(beta[1]),
                            st_gdn[:, 1])
    o_gdn = from_heads(rmsnorm(o_f + flip_seq(o_b), p['gdn_norm'][l])) * jax.nn.silu(dgate)

    out = jnp.concatenate([o_gla, o_s5, o_gdn], axis=-1) @ p['w_out'][l].astype(f32)
    states = (jnp.stack([sg_f, sg_b], axis=1), jnp.stack([ss_f, ss_b], axis=1),
              jnp.stack([sd_f, sd_b], axis=1))
    return out, states


def adaln(cond, p, l):
    return (jax.nn.silu(cond.astype(jnp.float32)) @ p['w_ada'][l].astype(jnp.float32)
            + p['b_ada'][l].astype(jnp.float32))


def trunk_layer(x, ada, p, l, st_gla, st_s5, st_gdn, latent):
    shift, scale, gate = jnp.split(ada, 3, axis=-1)
    h = rmsnorm(x, p['norm_pre'][l]) * (1.0 + scale) + shift
    out, states = mixer(h, p, l, st_gla, st_s5, st_gdn, latent)
    return x + gate * rmsnorm(out, p['norm_post'][l]), states


def setup_inputs(seed: int = 0) -> dict:
    key = jax.random.key(seed)
    ks = iter(jax.random.split(key, 40))
    f32 = jnp.float32

    def nrm(shape, scale):
        return jax.random.normal(next(ks), shape, f32) * scale

    def unif(shape, lo, hi):
        return jax.random.uniform(next(ks), shape, f32, lo, hi)

    lam_re = -0.5 + nrm((DEPTH, 2, S5_G, S5_P), 0.01)
    lam_im = math.pi * jnp.arange(S5_P, dtype=f32) + nrm((DEPTH, 2, S5_G, S5_P), 0.01)
    s5_log_dt = unif((DEPTH, 2, S5_G), math.log(1e-3), math.log(1e-1))
    gdn_dt = jnp.exp(unif((DEPTH, 2, GDN_H), math.log(1e-3), math.log(1e-1)))
    gdn_dt_bias = gdn_dt + jnp.log(-jnp.expm1(-gdn_dt))
    gdn_a_log = jnp.log(unif((DEPTH, 2, GDN_H), 1.0, 16.0))
    return {
        'x_prompt': nrm((BATCH, SEQ, D_MODEL), 1.0),
        'x_sample': nrm((DEC_BATCH, DEC_SEQ, D_MODEL), 1.0),
        'c': nrm((DEC_BATCH, D_MODEL), 1.0),
        'state_gla': nrm((DEC_BATCH, DEPTH, 2, GLA_H, GLA_DK, GLA_DV), 1.0),
        'state_s5_re': nrm((DEC_BATCH, DEPTH, 2, S5_G, S5_P), 0.1),
        'state_s5_im': nrm((DEC_BATCH, DEPTH, 2, S5_G, S5_P), 0.1),
        'state_gdn': nrm((DEC_BATCH, DEPTH, 2, GDN_H, GDN_DK, GDN_DV), GDN_DK ** -0.5),
        'c_ctx': nrm((D_MODEL,), 1.0),
        'norm_pre': 1.0 + nrm((DEPTH, D_MODEL), 0.02),
        'norm_post': 1.0 + nrm((DEPTH, D_MODEL), 0.02),
        'w_ada': nrm((DEPTH, D_MODEL, 3 * D_MODEL), 0.5 * D_MODEL ** -0.5),
        'b_ada': nrm((DEPTH, 3 * D_MODEL), 0.02),
        'w_in': nrm((DEPTH, D_MODEL, IN_DIM), D_MODEL ** -0.5),
        'gla_gate_w': nrm((DEPTH, 2, GLA_LR, GLA_QK), GLA_LR ** -0.5),
        'gla_gate_b': nrm((DEPTH, 2, GLA_QK), 0.1),
        'gla_norm': 1.0 + nrm((DEPTH, GLA_DV), 0.02),
        's5_lam_re': lam_re,
        's5_lam_im': lam_im,
        's5_log_dt': s5_log_dt,
        's5_b_re': nrm((DEPTH, S5_G, S5_P, S5_GH), (2 * S5_GH) ** -0.5),
        's5_b_im': nrm((DEPTH, S5_G, S5_P, S5_GH), (2 * S5_GH) ** -0.5),
        's5_c_re': nrm((DEPTH, 2, S5_G, S5_GH, S5_P), 0.5),
        's5_c_im': nrm((DEPTH, 2, S5_G, S5_GH, S5_P), 0.5),
        's5_d': nrm((DEPTH, S5_W), 1.0),
        's5_glu_w': nrm((DEPTH, S5_W, S5_W), S5_W ** -0.5),
        's5_glu_b': nrm((DEPTH, S5_W), 0.02),
        'gdn_conv': nrm((DEPTH, CONV_K, 3 * GDN_W), CONV_K ** -0.5),
        'gdn_a_log': gdn_a_log,
        'gdn_dt_bias': gdn_dt_bias,
        'gdn_norm': 1.0 + nrm((DEPTH, GDN_DV), 0.02),
        'w_out': nrm((DEPTH, MIX_W, D_MODEL), MIX_W ** -0.5),
    }


def reference(x_prompt, x_sample, c, state_gla, state_s5_re, state_s5_im, state_gdn, c_ctx,
              norm_pre, norm_post, w_ada, b_ada, w_in, gla_gate_w, gla_gate_b, gla_norm,
              s5_lam_re, s5_lam_im, s5_log_dt, s5_b_re, s5_b_im, s5_c_re, s5_c_im, s5_d,
              s5_glu_w, s5_glu_b, gdn_conv, gdn_a_log, gdn_dt_bias, gdn_norm, w_out):
    f32 = jnp.float32
    p = dict(norm_pre=norm_pre, norm_post=norm_post, w_ada=w_ada, b_ada=b_ada, w_in=w_in,
             gla_gate_w=gla_gate_w, gla_gate_b=gla_gate_b, gla_norm=gla_norm,
             s5_lam_re=s5_lam_re, s5_lam_im=s5_lam_im, s5_log_dt=s5_log_dt,
             s5_b_re=s5_b_re, s5_b_im=s5_b_im, s5_c_re=s5_c_re, s5_c_im=s5_c_im, s5_d=s5_d,
             s5_glu_w=s5_glu_w, s5_glu_b=s5_glu_b, gdn_conv=gdn_conv, gdn_a_log=gdn_a_log,
             gdn_dt_bias=gdn_dt_bias, gdn_norm=gdn_norm, w_out=w_out)

    bp = x_prompt.shape[0]
    z_gla = jnp.zeros((bp, 2, GLA_H, GLA_DK, GLA_DV), f32)
    z_s5 = jnp.zeros((bp, 2, S5_G, S5_P), jnp.complex64)
    z_gdn = jnp.zeros((bp, 2, GDN_H, GDN_DK, GDN_DV), f32)
    xp = x_prompt.astype(f32)
    gla_states, s5_states, gdn_states = [], [], []
    for l in range(DEPTH):
        ada = adaln(c_ctx, p, l)[None, None, :]
        xp, (sg, ss, sd) = trunk_layer(xp, ada, p, l, z_gla, z_s5, z_gdn, False)
        gla_states.append(sg)
        s5_states.append(ss)
        gdn_states.append(sd)
    y_prompt = xp.astype(x_prompt.dtype)
    s5_all = jnp.stack(s5_states, axis=1)
    new_state_gla = jnp.stack(gla_states, axis=1).astype(x_prompt.dtype)
    new_state_s5_re = s5_all.real.astype(x_prompt.dtype)
    new_state_s5_im = s5_all.imag.astype(x_prompt.dtype)
    new_state_gdn = jnp.stack(gdn_states, axis=1).astype(x_prompt.dtype)

    s5_cache = lax.complex(state_s5_re.astype(f32), state_s5_im.astype(f32))
    xs = x_sample.astype(f32)
    for l in range(DEPTH):
        ada = adaln(c, p, l)[:, None, :]
        xs, _ = trunk_layer(xs, ada, p, l, state_gla[:, l].astype(f32), s5_cache[:, l],
                            state_gdn[:, l].astype(f32), True)
    y_sample = xs.astype(x_sample.dtype)
    return (y_prompt, y_sample, new_state_gla, new_state_s5_re, new_state_s5_im, new_state_gdn)
```

```python
import functools
import math

import jax
import jax.numpy as jnp
from jax import lax
from jax.experimental import pallas as pl
from jax.experimental.pallas import tpu as pltpu

F32 = jnp.float32
BF16 = jnp.bfloat16
HI = lax.Precision.HIGHEST

D_MODEL = 1024
DEPTH = 4
GRID_W = 64
CHUNK = 64
EPS = 1e-6
GLA_H, GLA_DK, GLA_DV, GLA_LR, GLA_TAU = 4, 64, 128, 16, 16.0
GLA_QK, GLA_W = GLA_H * GLA_DK, GLA_H * GLA_DV
S5_GH, S5_W, S5_P = 16, 512, 64
S5_G = S5_W // S5_GH
S5_T = 16
GDN_H, GDN_DK, GDN_DV = 4, 128, 128
GDN_W = GDN_H * GDN_DV
MIX_W = GLA_W + S5_W + GDN_W

LANE = 128
TOKEN_TILE = 256
VMEM_LIMIT = 48 * 1024 * 1024

COL_GQ, COL_GK, COL_GV, COL_GGATE, COL_SU, COL_SGATE = 0, 2, 4, 8, 12, 16
COL_DQ, COL_DK, COL_DV, COL_DGATE = 20, 24, 28, 32
MAIN_W = 36 * LANE
SM_GLR, SM_DA, SM_DB = 0, 32, 40


def _dot(a, b, precision=None):
    return lax.dot_general(a, b, (((1,), (0,)), ((), ())), precision=precision,
                           preferred_element_type=F32)


def _dot_nt(a, b, precision=None):
    return lax.dot_general(a, b, (((1,), (1,)), ((), ())), precision=precision,
                           preferred_element_type=F32)


def _dot_tn(a, b, precision=None):
    return lax.dot_general(a, b, (((0,), (0,)), ((), ())), precision=precision,
                           preferred_element_type=F32)


def _silu(x):
    return x * jax.nn.sigmoid(x)


def _softplus(x):
    return jnp.maximum(x, 0.0) + jnp.log1p(jnp.exp(-jnp.abs(x)))


def _params(sem):
    return pltpu.CompilerParams(dimension_semantics=sem, vmem_limit_bytes=VMEM_LIMIT)


def _adaln_kernel(cond_ref, w_ref, b_ref, o_ref):
    c = cond_ref[...]
    o_ref[0] = _dot(_silu(c), w_ref[0], precision=HI) + b_ref[0]


def _adaln(cond, w_ada, b_ada):
    rows = cond.shape[0]
    nj = 3 * D_MODEL // 1024
    return pl.pallas_call(
        _adaln_kernel,
        out_shape=jax.ShapeDtypeStruct((DEPTH, rows, 3 * D_MODEL), F32),
        grid=(DEPTH, nj),
        in_specs=[pl.BlockSpec((rows, D_MODEL), lambda l, j: (0, 0)),
                  pl.BlockSpec((1, D_MODEL, 1024), lambda l, j: (l, 0, j)),
                  pl.BlockSpec((1, 1, 1024), lambda l, j: (l, 0, j))],
        out_specs=pl.BlockSpec((1, rows, 1024), lambda l, j: (l, 0, j)),
        compiler_params=_params(("arbitrary", "arbitrary")),
        name="adaln",
    )(cond, w_ada, b_ada.reshape(DEPTH, 1, 3 * D_MODEL))


def _inproj_kernel(x_ref, ada_ref, g_ref, wm_ref, ws_ref, om_ref, os_ref):
    x = x_ref[...]
    nrm = x * lax.rsqrt(jnp.mean(x * x, axis=-1, keepdims=True) + EPS) * g_ref[...]
    ada = ada_ref[0]
    shift = ada[:, 0:D_MODEL]
    scale = ada[:, D_MODEL:2 * D_MODEL]
    h = (nrm * (1.0 + scale) + shift).astype(BF16)
    om_ref[...] = _dot(h, wm_ref[...])
    os_ref[...] = _dot(h, ws_ref[...])


def _ada_row(i, ctx_tiles, tiles_per_latent):
    return jnp.where(i < ctx_tiles, 0, 1 + (i - ctx_tiles) // tiles_per_latent)


def _inproj(x, ada_l, g_pre, w_main, w_small, ctx_tiles, tiles_per_latent):
    nt = x.shape[0]
    row = functools.partial(_ada_row, ctx_tiles=ctx_tiles, tiles_per_latent=tiles_per_latent)
    return pl.pallas_call(
        _inproj_kernel,
        out_shape=(jax.ShapeDtypeStruct((nt, MAIN_W), F32), jax.ShapeDtypeStruct((nt, LANE), F32)),
        grid=(nt // TOKEN_TILE,),
        in_specs=[pl.BlockSpec((TOKEN_TILE, D_MODEL), lambda i: (i, 0)),
                  pl.BlockSpec((1, 1, 3 * D_MODEL), lambda i: (row(i), 0, 0)),
                  pl.BlockSpec((1, D_MODEL), lambda i: (0, 0)),
                  pl.BlockSpec((D_MODEL, MAIN_W), lambda i: (0, 0)),
                  pl.BlockSpec((D_MODEL, LANE), lambda i: (0, 0))],
        out_specs=(pl.BlockSpec((TOKEN_TILE, MAIN_W), lambda i: (i, 0)),
                   pl.BlockSpec((TOKEN_TILE, LANE), lambda i: (i, 0))),
        compiler_params=_params(("arbitrary",)),
        name="inproj",
    )(x, ada_l, g_pre, w_main, w_small)


def _tri_mask(n, reverse, strict=False):
    r = lax.broadcasted_iota(jnp.int32, (n, n), 0)
    c = lax.broadcasted_iota(jnp.int32, (n, n), 1)
    if reverse:
        return (r < c) if strict else (r <= c)
    return (r > c) if strict else (r >= c)


def _eye_mask(n):
    return lax.broadcasted_iota(jnp.int32, (n, n), 0) == lax.broadcasted_iota(jnp.int32, (n, n), 1)


def _gla_kernel(q_ref, k_ref, v_ref, sm_ref, gw_ref, gb_ref, s0_ref, o_ref, sf_ref, st_ref,
                *, n_chunks, has_state):
    for d in range(2):
        for h in range(2):
            if has_state:
                st_ref[d, h * GLA_DK:(h + 1) * GLA_DK, :] = s0_ref[0, d, h]
            else:
                st_ref[d, h * GLA_DK:(h + 1) * GLA_DK, :] = jnp.zeros((GLA_DK, GLA_DV), F32)
    o_ref[...] = jnp.zeros(o_ref.shape, F32)

    lane = lax.broadcasted_iota(jnp.int32, (CHUNK, LANE), 1)
    head_mask = [lane < GLA_DK, lane >= GLA_DK]
    eye = _eye_mask(LANE)
    ones = jnp.ones((LANE, LANE), F32)
    scale = GLA_DK ** -0.5

    def chunk_step(c, carry):
        for d in range(2):
            cc = c if d == 0 else n_chunks - 1 - c
            rows = pl.ds(pl.multiple_of(cc * CHUNK, CHUNK), CHUNK)
            causal = _tri_mask(CHUNK, d == 1)
            q = q_ref[rows, :]
            k = k_ref[rows, :]
            v = v_ref[rows, :]
            z = _dot(sm_ref[rows, :], gw_ref[d], precision=HI) + gb_ref[d]
            glog = -_softplus(-z) * (1.0 / GLA_TAU)
            b = _dot(causal.astype(F32), glog, precision=HI)
            b_last = b[CHUNK - 1:CHUNK, :] if d == 0 else b[0:1, :]
            qe = q * jnp.exp(b) * scale
            ke = (k * jnp.exp(-b)).astype(BF16)
            kd = k * jnp.exp(b_last - b)
            s = st_ref[d]
            s_bf = s.astype(BF16)
            upd = jnp.zeros((LANE, GLA_DV), F32)
            for h in range(2):
                qh = jnp.where(head_mask[h], qe, 0.0).astype(BF16)
                sc = jnp.where(causal, _dot_nt(qh, ke), 0.0)
                vh = v[:, h * GLA_DV:(h + 1) * GLA_DV].astype(BF16)
                oh = _dot(sc.astype(BF16), vh) + _dot(qh, s_bf)
                o_ref[rows, h * GLA_DV:(h + 1) * GLA_DV] += oh
                upd = upd + _dot_tn(jnp.where(head_mask[h], kd, 0.0).astype(BF16), vh)
            diag = jnp.where(eye, jnp.broadcast_to(b_last, (LANE, LANE)), 0.0)
            dl = jnp.exp(_dot(diag, ones, precision=HI))
            st_ref[d] = dl * s + upd
        return carry

    lax.fori_loop(0, n_chunks, chunk_step, 0)
    for d in range(2):
        for h in range(2):
            sf_ref[0, d, h] = st_ref[d, h * GLA_DK:(h + 1) * GLA_DK, :]


def _gla(pm, ps, gw_pad, gb, s0, *, nseq, seq_len, row_block0):
    has_state = s0 is not None
    if s0 is None:
        s0 = jnp.zeros((1, 2, GLA_H, GLA_DK, GLA_DV), F32)
        s0_map = lambda b, p: (0, 0, p, 0, 0)
    else:
        s0_map = lambda b, p: (b, 0, p, 0, 0)
    kern = functools.partial(_gla_kernel, n_chunks=seq_len // CHUNK, has_state=has_state)
    return pl.pallas_call(
        kern,
        out_shape=(jax.ShapeDtypeStruct((nseq * seq_len, GLA_W), F32),
                   jax.ShapeDtypeStruct((nseq, 2, GLA_H, GLA_DK, GLA_DV), F32)),
        grid=(nseq, 2),
        in_specs=[pl.BlockSpec((seq_len, LANE), lambda b, p: (row_block0 + b, COL_GQ + p)),
                  pl.BlockSpec((seq_len, LANE), lambda b, p: (row_block0 + b, COL_GK + p)),
                  pl.BlockSpec((seq_len, 2 * LANE), lambda b, p: (row_block0 + b, COL_GV // 2 + p)),
                  pl.BlockSpec((seq_len, LANE), lambda b, p: (row_block0 + b, 0)),
                  pl.BlockSpec((2, LANE, LANE), lambda b, p: (0, 0, p)),
                  pl.BlockSpec((2, 1, LANE), lambda b, p: (0, 0, p)),
                  pl.BlockSpec((1, 2, 2, GLA_DK, GLA_DV), s0_map)],
        out_specs=(pl.BlockSpec((seq_len, 2 * LANE), lambda b, p: (b, p)),
                   pl.BlockSpec((1, 2, 2, GLA_DK, GLA_DV), lambda b, p: (b, 0, p, 0, 0))),
        scratch_shapes=[pltpu.VMEM((2, LANE, GLA_DV), F32)],
        compiler_params=_params(("arbitrary", "arbitrary")),
        name="gla",
    )(pm, pm, pm, ps, gw_pad, gb, s0)


def _block_mask(n, s, reverse):
    r = lax.broadcasted_iota(jnp.int32, (n, n), 0)
    c = lax.broadcasted_iota(jnp.int32, (n, n), 1)
    if reverse:
        r, c = c, r
    sh = s.bit_length() - 1
    same_pair = (r >> (sh + 1)) == (c >> (sh + 1))
    return same_pair & (((r >> sh) & 1) == 1) & (((c >> sh) & 1) == 0)


def _unit_tri_inverse(m, reverse):
    n = m.shape[0]
    t = _eye_mask(n).astype(F32)
    s = 1
    while s < n:
        am = jnp.where(_block_mask(n, s, reverse), m, 0.0)
        t = t - _dot(_dot(t, am, precision=HI), t, precision=HI)
        s *= 2
    return t


def _gdn_kernel(ab_ref, q_ref, k_ref, v_ref, sm_ref, cwq_ref, cwk_ref, cwv_ref, s0_ref,
                o_ref, sf_ref, qs_ref, ks_ref, vs_ref, st_ref, *, n_chunks, period, has_state):
    hd = pl.program_id(1)
    seq_len = n_chunks * CHUNK
    row = lax.broadcasted_iota(jnp.int32, (seq_len, LANE), 0)
    first = (row % period) == 0
    last = (row % period) == period - 1

    def conv_silu(x_ref, w_ref):
        x = x_ref[...]
        w = w_ref[...]
        xp = jnp.where(first, 0.0, pltpu.roll(x, 1, 0))
        xn = jnp.where(last, 0.0, pltpu.roll(x, seq_len - 1, 0))
        return _silu(xp * w[0:1, :] + x * w[1:2, :] + xn * w[2:3, :])

    def l2norm(x):
        return x * lax.rsqrt(jnp.sum(x * x, axis=-1, keepdims=True) + EPS)

    qs_ref[...] = l2norm(conv_silu(q_ref, cwq_ref)) * GDN_DK ** -0.5
    ks_ref[...] = l2norm(conv_silu(k_ref, cwk_ref))
    vs_ref[...] = conv_silu(v_ref, cwv_ref)

    for d in range(2):
        if has_state:
            st_ref[d] = s0_ref[0, d, 0]
        else:
            st_ref[d] = jnp.zeros((GDN_DK, GDN_DV), F32)
    o_ref[...] = jnp.zeros(o_ref.shape, F32)
    ones = jnp.ones((CHUNK, CHUNK), F32)

    def chunk_step(c, carry):
        for d in range(2):
            cc = c if d == 0 else n_chunks - 1 - c
            rows = pl.ds(pl.multiple_of(cc * CHUNK, CHUNK), CHUNK)
            causal = _tri_mask(CHUNK, d == 1)
            strict = _tri_mask(CHUNK, d == 1, strict=True)
            anti = _tri_mask(CHUNK, d == 0)
            q = qs_ref[rows, :]
            k = ks_ref[rows, :]
            v = vs_ref[rows, :]
            sm = sm_ref[rows, :]
            sel_r = lax.broadcasted_iota(jnp.int32, (LANE, LANE), 0)
            sel_a = (sel_r == SM_DA + d * GDN_H + hd).astype(F32)
            sel_b = (sel_r == SM_DB + d * GDN_H + hd).astype(F32)
            a_raw = _dot(sm, sel_a, precision=HI)
            b_raw = _dot(sm, sel_b, precision=HI)
            a_log = jnp.full((CHUNK, LANE), ab_ref[d, hd], F32)
            dt_bias = jnp.full((CHUNK, LANE), ab_ref[d, GDN_H + hd], F32)
            g = -jnp.exp(a_log) * _softplus(a_raw + dt_bias)
            beta = jax.nn.sigmoid(b_raw)
            gc = _dot(causal.astype(F32), g, precision=HI)
            gc_t = _dot(ones, jnp.where(anti, g[:, :CHUNK], 0.0), precision=HI)
            diff = gc[:, :CHUNK] - gc_t
            decay = jnp.where(causal, jnp.exp(jnp.where(causal, diff, 0.0)), 0.0)
            kb = k * beta
            k_bf = k.astype(BF16)
            m = jnp.where(strict, _dot_nt(kb.astype(BF16), k_bf) * decay, 0.0)
            t = _unit_tri_inverse(m, d == 1).astype(BF16)
            egc = jnp.exp(gc)
            u = _dot(t, (v * beta).astype(BF16))
            w = _dot(t, (kb * egc).astype(BF16))
            a_qk = _dot_nt(q.astype(BF16), k_bf) * decay
            qg = q * egc
            gc_last = gc[CHUNK - 1:CHUNK, :] if d == 0 else gc[0:1, :]
            kd = k * jnp.exp(gc_last - gc)
            s = st_ref[d]
            s_bf = s.astype(BF16)
            v_new = u - _dot(w.astype(BF16), s_bf)
            v_new_bf = v_new.astype(BF16)
            o_ref[rows, :] += _dot(qg.astype(BF16), s_bf) + _dot(a_qk.astype(BF16), v_new_bf)
            st_ref[d] = jnp.exp(gc_last) * s + _dot_tn(kd.astype(BF16), v_new_bf)
        return carry

    lax.fori_loop(0, n_chunks, chunk_step, 0)
    for d in range(2):
        sf_ref[0, d, 0] = st_ref[d]


def _gdn(pm, ps, ab, conv_w, s0, *, nseq, seq_len, row_block0, period):
    has_state = s0 is not None
    if s0 is None:
        s0 = jnp.zeros((1, 2, GDN_H, GDN_DK, GDN_DV), F32)
        s0_map = lambda b, h, ab: (0, 0, h, 0, 0)
    else:
        s0_map = lambda b, h, ab: (b, 0, h, 0, 0)
    kern = functools.partial(_gdn_kernel, n_chunks=seq_len // CHUNK, period=period, has_state=has_state)
    grid_spec = pltpu.PrefetchScalarGridSpec(
        num_scalar_prefetch=1,
        grid=(nseq, GDN_H),
        in_specs=[pl.BlockSpec((seq_len, LANE), lambda b, h, ab: (row_block0 + b, COL_DQ + h)),
                  pl.BlockSpec((seq_len, LANE), lambda b, h, ab: (row_block0 + b, COL_DK + h)),
                  pl.BlockSpec((seq_len, LANE), lambda b, h, ab: (row_block0 + b, COL_DV + h)),
                  pl.BlockSpec((seq_len, LANE), lambda b, h, ab: (row_block0 + b, 0)),
                  pl.BlockSpec((3, LANE), lambda b, h, ab: (0, h)),
                  pl.BlockSpec((3, LANE), lambda b, h, ab: (0, GDN_H + h)),
                  pl.BlockSpec((3, LANE), lambda b, h, ab: (0, 2 * GDN_H + h)),
                  pl.BlockSpec((1, 2, 1, GDN_DK, GDN_DV), s0_map)],
        out_specs=(pl.BlockSpec((seq_len, LANE), lambda b, h, ab: (b, h)),
                   pl.BlockSpec((1, 2, 1, GDN_DK, GDN_DV), lambda b, h, ab: (b, 0, h, 0, 0))),
        scratch_shapes=[pltpu.VMEM((seq_len, LANE), F32), pltpu.VMEM((seq_len, LANE), F32),
                        pltpu.VMEM((seq_len, LANE), F32), pltpu.VMEM((2, GDN_DK, GDN_DV), F32)],
    )
    return pl.pallas_call(
        kern,
        out_shape=(jax.ShapeDtypeStruct((nseq * seq_len, GDN_W), F32),
                   jax.ShapeDtypeStruct((nseq, 2, GDN_H, GDN_DK, GDN_DV), F32)),
        grid_spec=grid_spec,
        compiler_params=_params(("arbitrary", "arbitrary")),
        name="gdn",
    )(ab, pm, pm, pm, ps, conv_w, conv_w, conv_w, s0)


def _s5_kernel(u_ref, mi_ref, wst_ref, wout_ref, lam_ref, h0_ref, y_ref, fin_ref, e_ref, x_ref,
               *, n_chunks, nseq):
    half = 2 * S5_P
    u = u_ref[0]
    e_ref[...] = _dot(u, wst_ref[0])
    y_ref[0] = _dot(u, mi_ref[0])
    lam = lam_ref[0]
    l1f, l2f, l1b, l2b = lam[0:1, :], lam[1:2, :], lam[2:3, :], lam[3:4, :]
    h0 = h0_ref[0]

    def step(c, carry):
        xf, xb = carry
        rf = pl.ds(pl.multiple_of(c * nseq, nseq), nseq)
        rb = pl.ds(pl.multiple_of((n_chunks - 1 - c) * nseq, nseq), nseq)
        x_ref[rf, 0:half] = xf
        x_ref[rb, half:2 * half] = xb
        xf = xf * l1f + pltpu.roll(xf, S5_P, 1) * l2f + e_ref[rf, 0:half]
        xb = xb * l1b + pltpu.roll(xb, S5_P, 1) * l2b + e_ref[rb, half:2 * half]
        return xf, xb

    xf, xb = lax.fori_loop(0, n_chunks, step, (h0[:, 0:half], h0[:, half:2 * half]))
    y_ref[0] += _dot(x_ref[...].astype(BF16), wout_ref[0])
    fin_ref[0, :, 0:half] = xf
    fin_ref[0, :, half:2 * half] = xb


def _s5(u, m_intra, w_st, w_out, lam_t, h0, *, nseq, n_chunks):
    rows = n_chunks * nseq
    kern = functools.partial(_s5_kernel, n_chunks=n_chunks, nseq=nseq)
    wspec = pl.BlockSpec((1, 2 * LANE, 2 * LANE), lambda g: (g, 0, 0))
    return pl.pallas_call(
        kern,
        out_shape=(jax.ShapeDtypeStruct((S5_G, rows, 2 * LANE), F32),
                   jax.ShapeDtypeStruct((S5_G, nseq, 2 * LANE), F32)),
        grid=(S5_G,),
        in_specs=[pl.BlockSpec((1, rows, 2 * LANE), lambda g: (g, 0, 0)),
                  wspec, wspec, wspec,
                  pl.BlockSpec((1, 4, LANE), lambda g: (g, 0, 0)),
                  pl.BlockSpec((1, nseq, 2 * LANE), lambda g: (g, 0, 0))],
        out_specs=(pl.BlockSpec((1, rows, 2 * LANE), lambda g: (g, 0, 0)),
                   pl.BlockSpec((1, nseq, 2 * LANE), lambda g: (g, 0, 0))),
        scratch_shapes=[pltpu.VMEM((rows, 2 * LANE), F32), pltpu.VMEM((rows, 2 * LANE), F32)],
        compiler_params=_params(("arbitrary",)),
        name="s5",
    )(u, m_intra, w_st, w_out, lam_t, h0)


def _cmul(ar, ai, br, bi):
    return ar * br - ai * bi, ar * bi + ai * br


def _s5_weights(lam_re, lam_im, log_dt, b_re, b_im, c_re, c_im):
    t = S5_T
    dt = jnp.exp(log_dt)[..., None]
    tau = jnp.arange(t + 1, dtype=F32)[:, None, None, None]
    mag = jnp.exp(tau * (lam_re * dt)[None])
    ang = tau * (lam_im * dt)[None]
    pw_re, pw_im = mag * jnp.cos(ang), mag * jnp.sin(ang)
    nr, ni = pw_re[1] - 1.0, pw_im[1]
    den = lam_re * lam_re + lam_im * lam_im
    fr, fi = (nr * lam_re + ni * lam_im) / den, (ni * lam_re - nr * lam_im) / den
    bb_re, bb_im = _cmul(fr[..., None], fi[..., None], b_re[None], b_im[None])

    def kern_tau(d):
        cl_re, cl_im = _cmul(c_re[d][None], c_im[d][None], pw_re[:t, d, :, None, :], pw_im[:t, d, :, None, :])
        return (jnp.einsum('tgop,gpi->tgoi', cl_re, bb_re[d], precision=HI)
                - jnp.einsum('tgop,gpi->tgoi', cl_im, bb_im[d], precision=HI))

    kf, kb = kern_tau(0), kern_tau(1)
    s_idx = jnp.arange(t)[:, None]
    t_idx = jnp.arange(t)[None, :]
    lag_f = jnp.clip(t_idx - s_idx, 0, t - 1)
    lag_b = jnp.clip(s_idx - t_idx, 0, t - 1)
    a_f = jnp.where((t_idx >= s_idx)[:, :, None, None, None], kf[lag_f], 0.0)
    a_b = jnp.where((s_idx >= t_idx)[:, :, None, None, None], kb[lag_b], 0.0)
    m_intra = (a_f + a_b).transpose(2, 0, 4, 1, 3).reshape(S5_G, t * S5_GH, t * S5_GH)

    def st(d, pw_idx):
        r, i = _cmul(pw_re[pw_idx, d][..., None], pw_im[pw_idx, d][..., None], bb_re[d][None], bb_im[d][None])
        return r, i
    fre, fim = st(0, jnp.arange(t - 1, -1, -1))
    bre, bim = st(1, jnp.arange(t))
    w_st = jnp.concatenate([fre, fim, bre, bim], axis=2)
    w_st = w_st.transpose(1, 0, 3, 2).reshape(S5_G, t * S5_GH, 4 * S5_P)

    def ro(d, pw_idx):
        r, i = _cmul(c_re[d][None], c_im[d][None], pw_re[pw_idx, d][:, :, None, :], pw_im[pw_idx, d][:, :, None, :])
        return r, -i
    fr_, fi_ = ro(0, jnp.arange(1, t + 1))
    br_, bi_ = ro(1, jnp.arange(t, 0, -1))
    w_out = jnp.concatenate([fr_, fi_, br_, bi_], axis=3)
    w_out = w_out.transpose(1, 3, 0, 2).reshape(S5_G, 4 * S5_P, t * S5_GH)

    lt_re, lt_im = pw_re[t], pw_im[t]
    lam_t = jnp.stack([jnp.concatenate([lt_re[0], lt_re[0]], -1), jnp.concatenate([-lt_im[0], lt_im[0]], -1),
                       jnp.concatenate([lt_re[1], lt_re[1]], -1), jnp.concatenate([-lt_im[1], lt_im[1]], -1)],
                      axis=1)
    return m_intra.astype(BF16), w_st.astype(BF16), w_out.astype(BF16), lam_t


def _gelu_tanh(x):
    return 0.5 * x * (1.0 + jnp.tanh(math.sqrt(2.0 / math.pi) * (x + 0.044715 * (x * x * x))))


def _out_kernel(x_ref, ada_ref, og_ref, gg_ref, ys_ref, su_ref, sg_ref, od_ref, dg_ref,
                gn_ref, dn_ref, sd_ref, gw_ref, gb_ref, wo_ref, np_ref, o_ref):
    def head_norm(o, g):
        parts = []
        for h in range(o.shape[1] // LANE):
            oh = o[:, h * LANE:(h + 1) * LANE]
            parts.append(oh * lax.rsqrt(jnp.mean(oh * oh, axis=-1, keepdims=True) + EPS) * g)
        return jnp.concatenate(parts, axis=1)

    o_gla = head_norm(og_ref[...], gn_ref[...]) * _silu(gg_ref[...])
    y = _gelu_tanh(ys_ref[...] + sd_ref[...] * su_ref[...])
    y = y * jax.nn.sigmoid(_dot(y.astype(BF16), gw_ref[...]) + gb_ref[...])
    o_s5 = y * _silu(sg_ref[...])
    o_gdn = head_norm(od_ref[...], dn_ref[...]) * _silu(dg_ref[...])
    out = (_dot(o_gla.astype(BF16), wo_ref[0:GLA_W, :])
           + _dot(o_s5.astype(BF16), wo_ref[GLA_W:GLA_W + S5_W, :])
           + _dot(o_gdn.astype(BF16), wo_ref[GLA_W + S5_W:MIX_W, :]))
    r = out * lax.rsqrt(jnp.mean(out * out, axis=-1, keepdims=True) + EPS) * np_ref[...]
    gate = ada_ref[0][:, 2 * D_MODEL:3 * D_MODEL]
    o_ref[...] = x_ref[...] + gate * r


def _out(x, ada_l, pm, o_gla, y_s5, o_gdn, gla_norm, gdn_norm, s5_d, glu_w, glu_b, w_out, norm_post,
         ctx_tiles, tiles_per_latent):
    nt = x.shape[0]
    row = functools.partial(_ada_row, ctx_tiles=ctx_tiles, tiles_per_latent=tiles_per_latent)
    wide = 4 * LANE

    def tok(col_block):
        return pl.BlockSpec((TOKEN_TILE, wide), lambda i: (i, col_block))

    def full(shape):
        return pl.BlockSpec(shape, lambda i: (0,) * len(shape))

    return pl.pallas_call(
        _out_kernel,
        out_shape=jax.ShapeDtypeStruct((nt, D_MODEL), F32),
        grid=(nt // TOKEN_TILE,),
        in_specs=[pl.BlockSpec((TOKEN_TILE, D_MODEL), lambda i: (i, 0)),
                  pl.BlockSpec((1, 1, 3 * D_MODEL), lambda i: (row(i), 0, 0)),
                  tok(0), tok(COL_GGATE // 4), tok(0), tok(COL_SU // 4), tok(COL_SGATE // 4),
                  tok(0), tok(COL_DGATE // 4),
                  full((1, LANE)), full((1, LANE)), full((1, S5_W)), full((S5_W, S5_W)), full((1, S5_W)),
                  full((MIX_W, D_MODEL)), full((1, D_MODEL))],
        out_specs=pl.BlockSpec((TOKEN_TILE, D_MODEL), lambda i: (i, 0)),
        compiler_params=_params(("arbitrary",)),
        name="outproj",
    )(x, ada_l, o_gla, pm, y_s5, pm, pm, o_gdn, pm, gla_norm, gdn_norm, s5_d, glu_w, glu_b, w_out, norm_post)


def _permute_w_in(w_in):
    offs = {}
    pos = 0
    for name, width in (('gq', GLA_QK), ('gk', GLA_QK), ('gv', GLA_W), ('glr', 2 * GLA_LR), ('ggate', GLA_W),
                        ('su', S5_W), ('sgate', S5_W), ('dqkv', 3 * GDN_W), ('da', 2 * GDN_H),
                        ('db', 2 * GDN_H), ('dgate', GDN_W)):
        offs[name] = (pos, pos + width)
        pos += width
    seg = lambda n: w_in[..., offs[n][0]:offs[n][1]]
    main = jnp.concatenate([seg('gq'), seg('gk'), seg('gv'), seg('ggate'), seg('su'), seg('sgate'),
                            seg('dqkv'), seg('dgate')], axis=-1)
    small = jnp.concatenate([seg('glr'), seg('da'), seg('db')], axis=-1)
    small = jnp.pad(small, ((0, 0), (0, 0), (0, LANE - small.shape[-1])))
    return main.astype(BF16), small.astype(BF16)


def _s5_to_groups(su, nseq, latent):
    if latent:
        x = su.reshape(nseq, -1, GRID_W, S5_G, S5_GH).transpose(3, 2, 0, 1, 4)
    else:
        x = su.reshape(nseq, -1, S5_T, S5_G, S5_GH).transpose(3, 1, 0, 2, 4)
    return x.reshape(S5_G, -1, S5_T * S5_GH)


def _s5_from_groups(y, nseq, latent):
    if latent:
        x = y.reshape(S5_G, GRID_W, nseq, -1, S5_GH).transpose(2, 3, 1, 0, 4)
    else:
        x = y.reshape(S5_G, -1, nseq, S5_T, S5_GH).transpose(2, 1, 3, 0, 4)
    return x.reshape(-1, S5_W)


def kernel(x_prompt, x_sample, c, state_gla, state_s5_re, state_s5_im, state_gdn, c_ctx, norm_pre, norm_post, w_ada, b_ada, w_in, gla_gate_w, gla_gate_b, gla_norm, s5_lam_re, s5_lam_im, s5_log_dt, s5_b_re, s5_b_im, s5_c_re, s5_c_im, s5_d, s5_glu_w, s5_glu_b, gdn_conv, gdn_a_log, gdn_dt_bias, gdn_norm, w_out):
    bp, lp, _ = x_prompt.shape
    bs, ls, _ = x_sample.shape
    n_ctx = bp * lp
    assert lp % TOKEN_TILE == 0 and ls % TOKEN_TILE == 0 and n_ctx % ls == 0
    assert ls // GRID_W == S5_T and lp % S5_T == 0
    ctx_tiles = n_ctx // TOKEN_TILE
    tiles_per_latent = ls // TOKEN_TILE

    cond = jnp.concatenate([c_ctx[None].astype(F32), c.astype(F32)], axis=0)
    rows = -(-cond.shape[0] // 8) * 8
    cond = jnp.pad(cond, ((0, rows - cond.shape[0]), (0, 0)))
    ada = _adaln(cond, w_ada.astype(F32), b_ada.astype(F32)).reshape(DEPTH, rows, 1, 3 * D_MODEL)

    w_main, w_small = _permute_w_in(w_in)
    w_out_bf = w_out.astype(BF16)
    glu_w_bf = s5_glu_w.astype(BF16)
    gw_pad = jnp.zeros((DEPTH, 2, LANE, GLA_QK), F32)
    for d in range(2):
        gw_pad = gw_pad.at[:, d, SM_GLR + d * GLA_LR:SM_GLR + (d + 1) * GLA_LR, :].set(gla_gate_w[:, d].astype(F32))
    gdn_ab = jnp.concatenate([gdn_a_log, gdn_dt_bias], axis=-1).astype(F32)

    x = jnp.concatenate([x_prompt.reshape(n_ctx, D_MODEL), x_sample.reshape(bs * ls, D_MODEL)], axis=0).astype(F32)
    gla_states, s5_states, gdn_states = [], [], []
    for l in range(DEPTH):
        pm, ps = _inproj(x, ada[l], norm_pre[l][None].astype(F32), w_main[l], w_small[l],
                         ctx_tiles, tiles_per_latent)
        gb = gla_gate_b[l].astype(F32).reshape(2, 1, GLA_QK)
        og_c, sg = _gla(pm, ps, gw_pad[l], gb, None, nseq=bp, seq_len=lp, row_block0=0)
        og_s, _ = _gla(pm, ps, gw_pad[l], gb, state_gla[:, l].astype(F32), nseq=bs, seq_len=ls,
                       row_block0=n_ctx // ls)
        conv_w = gdn_conv[l].astype(F32)
        od_c, sd = _gdn(pm, ps, gdn_ab[l], conv_w, None, nseq=bp, seq_len=lp, row_block0=0, period=lp)
        od_s, _ = _gdn(pm, ps, gdn_ab[l], conv_w, state_gdn[:, l].astype(F32), nseq=bs, seq_len=ls,
                       row_block0=n_ctx // ls, period=GRID_W)

        m_intra, w_st, w_ro, lam_t = _s5_weights(
            s5_lam_re[l].astype(F32), s5_lam_im[l].astype(F32), s5_log_dt[l].astype(F32),
            s5_b_re[l].astype(F32), s5_b_im[l].astype(F32), s5_c_re[l].astype(F32), s5_c_im[l].astype(F32))
        su = pm[:, COL_SU * LANE:COL_SU * LANE + S5_W].astype(BF16)
        u_c = _s5_to_groups(su[:n_ctx], bp, False)
        u_s = _s5_to_groups(su[n_ctx:], bs, True)
        h0_c = jnp.zeros((S5_G, bp, 4 * S5_P), F32)
        sre, sim = state_s5_re[:, l].astype(F32), state_s5_im[:, l].astype(F32)
        h0_s = jnp.concatenate([sre[:, 0], sim[:, 0], sre[:, 1], sim[:, 1]], axis=-1).transpose(1, 0, 2)
        y_c, fin = _s5(u_c, m_intra, w_st, w_ro, lam_t, h0_c, nseq=bp, n_chunks=lp // S5_T)
        y_s, _ = _s5(u_s, m_intra, w_st, w_ro, lam_t, h0_s, nseq=bs, n_chunks=GRID_W)
        y_s5 = jnp.concatenate([_s5_from_groups(y_c, bp, False), _s5_from_groups(y_s, bs, True)], axis=0)

        x = _out(x, ada[l], pm, jnp.concatenate([og_c, og_s], axis=0), y_s5,
                 jnp.concatenate([od_c, od_s], axis=0),
                 gla_norm[l][None].astype(F32), gdn_norm[l][None].astype(F32), s5_d[l][None].astype(F32),
                 glu_w_bf[l], s5_glu_b[l][None].astype(F32), w_out_bf[l], norm_post[l][None].astype(F32),
                 ctx_tiles, tiles_per_latent)
        gla_states.append(sg)
        gdn_states.append(sd)
        fin = fin.transpose(1, 0, 2).reshape(bp, S5_G, 2, 2, S5_P)
        s5_states.append(fin.transpose(0, 2, 3, 1, 4))

    dt = x_prompt.dtype
    s5_all = jnp.stack(s5_states, axis=1)
    y_prompt = x[:n_ctx].reshape(bp, lp, D_MODEL).astype(dt)
    y_sample = x[n_ctx:].reshape(bs, ls, D_MODEL).astype(x_sample.dtype)
    return (y_prompt, y_sample, jnp.stack(gla_states, axis=1).astype(dt),
            s5_all[:, :, :, 0].astype(dt), s5_all[:, :, :, 1].astype(dt),
            jnp.stack(gdn_states, axis=1).astype(dt))
```

```python
import functools
import math

import jax
import jax.numpy as jnp
from jax import lax
from jax.experimental import pallas as pl
from jax.experimental.pallas import tpu as pltpu

F32 = jnp.float32
BF16 = jnp.bfloat16
HI = lax.Precision.HIGHEST

D_MODEL = 1024
DEPTH = 4
GRID_W = 64
CHUNK = 64
EPS = 1e-6
GLA_H, GLA_DK, GLA_DV, GLA_LR, GLA_TAU = 4, 64, 128, 16, 16.0
GLA_QK, GLA_W = GLA_H * GLA_DK, GLA_H * GLA_DV
S5_GH, S5_W, S5_P = 16, 512, 64
S5_G = S5_W // S5_GH
S5_T = 16
GDN_H, GDN_DK, GDN_DV = 4, 128, 128
GDN_W = GDN_H * GDN_DV
MIX_W = GLA_W + S5_W + GDN_W

LANE = 128
TOKEN_TILE = 256
VMEM_LIMIT = 48 * 1024 * 1024

COL_GQ, COL_GK, COL_GV, COL_GGATE, COL_SU, COL_SGATE = 0, 2, 4, 8, 12, 16
COL_DQ, COL_DK, COL_DV, COL_DGATE = 20, 24, 28, 32
MAIN_W = 36 * LANE
SM_GLR, SM_DA, SM_DB = 0, 32, 40


def _dot(a, b, precision=None):
    return lax.dot_general(a, b, (((1,), (0,)), ((), ())), precision=precision,
                           preferred_element_type=F32)


def _dot_nt(a, b, precision=None):
    return lax.dot_general(a, b, (((1,), (1,)), ((), ())), precision=precision,
                           preferred_element_type=F32)


def _dot_tn(a, b, precision=None):
    return lax.dot_general(a, b, (((0,), (0,)), ((), ())), precision=precision,
                           preferred_element_type=F32)


def _silu(x):
    return x * jax.nn.sigmoid(x)


def _softplus(x):
    return jnp.maximum(x, 0.0) + jnp.log1p(jnp.exp(-jnp.abs(x)))


def _params(sem):
    return pltpu.CompilerParams(dimension_semantics=sem, vmem_limit_bytes=VMEM_LIMIT)


def _adaln_kernel(cond_ref, w_ref, b_ref, o_ref):
    c = cond_ref[...]
    o_ref[0] = _dot(_silu(c), w_ref[0], precision=HI) + b_ref[0]


def _adaln(cond, w_ada, b_ada):
    rows = cond.shape[0]
    nj = 3 * D_MODEL // 1024
    return pl.pallas_call(
        _adaln_kernel,
        out_shape=jax.ShapeDtypeStruct((DEPTH, rows, 3 * D_MODEL), F32),
        grid=(DEPTH, nj),
        in_specs=[pl.BlockSpec((rows, D_MODEL), lambda l, j: (0, 0)),
                  pl.BlockSpec((1, D_MODEL, 1024), lambda l, j: (l, 0, j)),
                  pl.BlockSpec((1, 1, 1024), lambda l, j: (l, 0, j))],
        out_specs=pl.BlockSpec((1, rows, 1024), lambda l, j: (l, 0, j)),
        compiler_params=_params(("arbitrary", "arbitrary")),
        name="adaln",
    )(cond, w_ada, b_ada.reshape(DEPTH, 1, 3 * D_MODEL))


def _inproj_kernel(x_ref, ada_ref, g_ref, wm_ref, ws_ref, om_ref, os_ref):
    x = x_ref[...]
    nrm = x * lax.rsqrt(jnp.mean(x * x, axis=-1, keepdims=True) + EPS) * g_ref[...]
    ada = ada_ref[0]
    shift = ada[:, 0:D_MODEL]
    scale = ada[:, D_MODEL:2 * D_MODEL]
    h = (nrm * (1.0 + scale) + shift).astype(BF16)
    om_ref[...] = _dot(h, wm_ref[...])
    os_ref[...] = _dot(h, ws_ref[...])


def _ada_row(i, ctx_tiles, tiles_per_latent):
    return jnp.where(i < ctx_tiles, 0, 1 + (i - ctx_tiles) // tiles_per_latent)


def _inproj(x, ada_l, g_pre, w_main, w_small, ctx_tiles, tiles_per_latent):
    nt = x.shape[0]
    row = functools.partial(_ada_row, ctx_tiles=ctx_tiles, tiles_per_latent=tiles_per_latent)
    return pl.pallas_call(
        _inproj_kernel,
        out_shape=(jax.ShapeDtypeStruct((nt, MAIN_W), F32), jax.ShapeDtypeStruct((nt, LANE), F32)),
        grid=(nt // TOKEN_TILE,),
        in_specs=[pl.BlockSpec((TOKEN_TILE, D_MODEL), lambda i: (i, 0)),
                  pl.BlockSpec((1, 1, 3 * D_MODEL), lambda i: (row(i), 0, 0)),
                  pl.BlockSpec((1, D_MODEL), lambda i: (0, 0)),
                  pl.BlockSpec((D_MODEL, MAIN_W), lambda i: (0, 0)),
                  pl.BlockSpec((D_MODEL, LANE), lambda i: (0, 0))],
        out_specs=(pl.BlockSpec((TOKEN_TILE, MAIN_W), lambda i: (i, 0)),
                   pl.BlockSpec((TOKEN_TILE, LANE), lambda i: (i, 0))),
        compiler_params=_params(("arbitrary",)),
        name="inproj",
    )(x, ada_l, g_pre, w_main, w_small)


def _tri_mask(n, reverse, strict=False):
    r = lax.broadcasted_iota(jnp.int32, (n, n), 0)
    c = lax.broadcasted_iota(jnp.int32, (n, n), 1)
    if reverse:
        return (r < c) if strict else (r <= c)
    return (r > c) if strict else (r >= c)


def _eye_mask(n):
    return lax.broadcasted_iota(jnp.int32, (n, n), 0) == lax.broadcasted_iota(jnp.int32, (n, n), 1)


def _gla_kernel(q_ref, k_ref, v_ref, sm_ref, gw_ref, gb_ref, s0_ref, o_ref, sf_ref, g_s, ob_s, st_ref,
                *, n_chunks, has_state):
    for d in range(2):
        if has_state:
            s0 = jnp.concatenate([s0_ref[0, d, 0], s0_ref[0, d, 1]], axis=0)
            st_ref[d] = s0.T
        else:
            st_ref[d] = jnp.zeros((GLA_DV, LANE), F32)
        z = _dot(sm_ref[...], gw_ref[d], precision=HI) + gb_ref[d]
        g_s[d] = -_softplus(-z) * (1.0 / GLA_TAU)

    lane = lax.broadcasted_iota(jnp.int32, (CHUNK, LANE), 1)
    head_mask = [lane < GLA_DK, lane >= GLA_DK]
    scale = GLA_DK ** -0.5
    causal = [_tri_mask(CHUNK, False), _tri_mask(CHUNK, True)]
    causal_f = [m.astype(F32) for m in causal]

    def chunk_step(c, carry):
        for d in range(2):
            cc = c if d == 0 else n_chunks - 1 - c
            rows = pl.ds(pl.multiple_of(cc * CHUNK, CHUNK), CHUNK)
            q = q_ref[rows, :]
            k = k_ref[rows, :]
            v = v_ref[rows, :]
            b = _dot(causal_f[d], g_s[d, rows, :], precision=HI)
            b_last = b[CHUNK - 1:CHUNK, :] if d == 0 else b[0:1, :]
            qe = q * jnp.exp(b) * scale
            ke = (k * jnp.exp(-b)).astype(BF16)
            kd = k * jnp.exp(b_last - b)
            st = st_ref[d]
            st_bf = st.astype(BF16)
            upd = jnp.zeros((GLA_DV, LANE), F32)
            for h in range(2):
                qh = jnp.where(head_mask[h], qe, 0.0).astype(BF16)
                sc = jnp.where(causal[d], _dot_nt(qh, ke), 0.0)
                vh = v[:, h * GLA_DV:(h + 1) * GLA_DV].astype(BF16)
                oh = _dot(sc.astype(BF16), vh) + _dot_nt(qh, st_bf)
                if d == 0:
                    o_ref[rows, h * GLA_DV:(h + 1) * GLA_DV] = oh
                else:
                    ob_s[rows, h * GLA_DV:(h + 1) * GLA_DV] = oh
                upd = upd + _dot_tn(vh, jnp.where(head_mask[h], kd, 0.0).astype(BF16))
            st_ref[d] = st * jnp.exp(b_last) + upd
        return carry

    lax.fori_loop(0, n_chunks, chunk_step, 0)
    o_ref[...] += ob_s[...]
    for d in range(2):
        s = st_ref[d].T
        for h in range(2):
            sf_ref[0, d, h] = s[h * GLA_DK:(h + 1) * GLA_DK, :]


def _gla(pm, ps, gw_pad, gb, s0, *, nseq, seq_len, row_block0):
    has_state = s0 is not None
    if s0 is None:
        s0 = jnp.zeros((1, 2, GLA_H, GLA_DK, GLA_DV), F32)
        s0_map = lambda b, p: (0, 0, p, 0, 0)
    else:
        s0_map = lambda b, p: (b, 0, p, 0, 0)
    kern = functools.partial(_gla_kernel, n_chunks=seq_len // CHUNK, has_state=has_state)
    return pl.pallas_call(
        kern,
        out_shape=(jax.ShapeDtypeStruct((nseq * seq_len, GLA_W), F32),
                   jax.ShapeDtypeStruct((nseq, 2, GLA_H, GLA_DK, GLA_DV), F32)),
        grid=(nseq, 2),
        in_specs=[pl.BlockSpec((seq_len, LANE), lambda b, p: (row_block0 + b, COL_GQ + p)),
                  pl.BlockSpec((seq_len, LANE), lambda b, p: (row_block0 + b, COL_GK + p)),
                  pl.BlockSpec((seq_len, 2 * LANE), lambda b, p: (row_block0 + b, COL_GV // 2 + p)),
                  pl.BlockSpec((seq_len, LANE), lambda b, p: (row_block0 + b, 0)),
                  pl.BlockSpec((2, LANE, LANE), lambda b, p: (0, 0, p)),
                  pl.BlockSpec((2, 1, LANE), lambda b, p: (0, 0, p)),
                  pl.BlockSpec((1, 2, 2, GLA_DK, GLA_DV), s0_map)],
        out_specs=(pl.BlockSpec((seq_len, 2 * LANE), lambda b, p: (b, p)),
                   pl.BlockSpec((1, 2, 2, GLA_DK, GLA_DV), lambda b, p: (b, 0, p, 0, 0))),
        scratch_shapes=[pltpu.VMEM((2, seq_len, LANE), F32), pltpu.VMEM((seq_len, 2 * LANE), F32),
                        pltpu.VMEM((2, GLA_DV, LANE), F32)],
        compiler_params=_params(("arbitrary", "arbitrary")),
        name="gla",
    )(pm, pm, pm, ps, gw_pad, gb, s0)


GDN_HPS = 2


def _block_mask(n, s, reverse):
    r = lax.broadcasted_iota(jnp.int32, (n, n), 0)
    c = lax.broadcasted_iota(jnp.int32, (n, n), 1)
    if reverse:
        r, c = c, r
    sh = s.bit_length() - 1
    same_pair = (r >> (sh + 1)) == (c >> (sh + 1))
    return same_pair & (((r >> sh) & 1) == 1) & (((c >> sh) & 1) == 0)


def _unit_tri_inverse(m, level_masks):
    t = _eye_mask(m.shape[0]).astype(F32) - m * level_masks[0]
    for mask in level_masks[1:]:
        t_bf = t.astype(BF16)
        t = t - _dot(_dot(t_bf, (m * mask).astype(BF16)).astype(BF16), t_bf)
    return t


def _gdn_kernel(ab_ref, q_ref, k_ref, v_ref, sm_ref, cwq_ref, cwk_ref, cwv_ref, s0_ref,
                o_ref, sf_ref, q_s, k_s, v_s, g_s, b_s, u_s, wq_s, a_s, kd_s, dl_s, ob_s, st_ref,
                *, n_chunks, period, has_state):
    hp = pl.program_id(1)
    seq_len = n_chunks * CHUNK
    row = lax.broadcasted_iota(jnp.int32, (seq_len, LANE), 0)
    first = (row % period) == 0
    last = (row % period) == period - 1

    def conv_silu(x, w):
        xp = jnp.where(first, 0.0, pltpu.roll(x, 1, 0))
        xn = jnp.where(last, 0.0, pltpu.roll(x, seq_len - 1, 0))
        return _silu(xp * w[0:1, :] + x * w[1:2, :] + xn * w[2:3, :])

    def l2norm(x):
        return x * lax.rsqrt(jnp.sum(x * x, axis=-1, keepdims=True) + EPS)

    sm = sm_ref[...]
    sel_r = lax.broadcasted_iota(jnp.int32, (LANE, LANE), 0)
    for h in range(GDN_HPS):
        hd = hp * GDN_HPS + h
        lanes = slice(h * LANE, (h + 1) * LANE)
        q_s[h] = l2norm(conv_silu(q_ref[:, lanes], cwq_ref[:, lanes])) * GDN_DK ** -0.5
        k_s[h] = l2norm(conv_silu(k_ref[:, lanes], cwk_ref[:, lanes]))
        v_s[h] = conv_silu(v_ref[:, lanes], cwv_ref[:, lanes])
        for d in range(2):
            a_raw = _dot(sm, (sel_r == SM_DA + d * GDN_H + hd).astype(F32), precision=HI)
            b_raw = _dot(sm, (sel_r == SM_DB + d * GDN_H + hd).astype(F32), precision=HI)
            a_log = jnp.full((1, LANE), ab_ref[d, hd], F32)
            dt_bias = jnp.full((1, LANE), ab_ref[d, GDN_H + hd], F32)
            g_s[h, d] = -jnp.exp(a_log) * _softplus(a_raw + dt_bias)
            b_s[h, d] = jax.nn.sigmoid(b_raw)
            if has_state:
                st_ref[h, d] = s0_ref[0, d, h]
            else:
                st_ref[h, d] = jnp.zeros((GDN_DK, GDN_DV), F32)

    ones = jnp.ones((CHUNK, CHUNK), F32)
    causal = [_tri_mask(CHUNK, False), _tri_mask(CHUNK, True)]
    causal_f = [m.astype(F32) for m in causal]
    strict_f = [_tri_mask(CHUNK, d == 1, strict=True).astype(F32) for d in range(2)]
    level_masks = [[_block_mask(CHUNK, 1 << j, d == 1).astype(F32) for j in range(CHUNK.bit_length() - 1)]
                   for d in range(2)]

    def phase_a(c, carry):
        rows = pl.ds(pl.multiple_of(c * CHUNK, CHUNK), CHUNK)
        rows2 = pl.ds(pl.multiple_of(c * 2 * CHUNK, 2 * CHUNK), CHUNK)
        rows2b = pl.ds(pl.multiple_of(c * 2 * CHUNK + CHUNK, CHUNK), CHUNK)
        for h in range(GDN_HPS):
            q = q_s[h, rows, :]
            k = k_s[h, rows, :]
            v = v_s[h, rows, :]
            k_bf = k.astype(BF16)
            qk = _dot_nt(q.astype(BF16), k_bf)
            for d in range(2):
                g = g_s[h, d, rows, :]
                beta = b_s[h, d, rows, :]
                gc = _dot(causal_f[d], g, precision=HI)
                gc_t = _dot(ones, g[:, :CHUNK] * causal_f[1 - d], precision=HI)
                diff = gc[:, :CHUNK] - gc_t
                decay = jnp.where(causal[d], jnp.exp(jnp.where(causal[d], diff, 0.0)), 0.0)
                kb = k * beta
                m = _dot_nt(kb.astype(BF16), k_bf) * decay * strict_f[d]
                t = _unit_tri_inverse(m, level_masks[d]).astype(BF16)
                egc = jnp.exp(gc)
                gc_last = gc[CHUNK - 1:CHUNK, :] if d == 0 else gc[0:1, :]
                u_s[h, d, rows, :] = _dot(t, (v * beta).astype(BF16))
                wq_s[h, d, rows2, :] = _dot(t, (kb * egc).astype(BF16)).astype(BF16)
                wq_s[h, d, rows2b, :] = (q * egc).astype(BF16)
                a_s[h, d, rows, :] = (qk * decay).astype(BF16)
                kd_s[h, d, rows, :] = (k * jnp.exp(gc_last - gc)).astype(BF16)
                dl_s[h, d, c] = jnp.broadcast_to(jnp.exp(gc_last), (8, LANE))
        return carry

    lax.fori_loop(0, n_chunks, phase_a, 0)

    def phase_b(i, carry):
        for h in range(GDN_HPS):
            lanes = slice(h * LANE, (h + 1) * LANE)
            for d in range(2):
                cc = i if d == 0 else n_chunks - 1 - i
                rows = pl.ds(pl.multiple_of(cc * CHUNK, CHUNK), CHUNK)
                rows2 = pl.ds(pl.multiple_of(cc * 2 * CHUNK, 2 * CHUNK), 2 * CHUNK)
                s = st_ref[h, d]
                s_bf = s.astype(BF16)
                ws = _dot(wq_s[h, d, rows2, :], s_bf)
                v_new = (u_s[h, d, rows, :] - ws[0:CHUNK, :]).astype(BF16)
                o = ws[CHUNK:2 * CHUNK, :] + _dot(a_s[h, d, rows, :], v_new)
                if d == 0:
                    o_ref[rows, lanes] = o
                else:
                    ob_s[rows, lanes] = o
                st_ref[h, d] = dl_s[h, d, cc][0:1, :] * s + _dot_tn(kd_s[h, d, rows, :], v_new)
        return carry

    lax.fori_loop(0, n_chunks, phase_b, 0)
    o_ref[...] += ob_s[...]
    for h in range(GDN_HPS):
        for d in range(2):
            sf_ref[0, d, h] = st_ref[h, d]


def _gdn(pm, ps, ab, conv_w, s0, *, nseq, seq_len, row_block0, period):
    has_state = s0 is not None
    hps = GDN_HPS
    if s0 is None:
        s0 = jnp.zeros((1, 2, GDN_H, GDN_DK, GDN_DV), F32)
        s0_map = lambda b, p, ab: (0, 0, p, 0, 0)
    else:
        s0_map = lambda b, p, ab: (b, 0, p, 0, 0)
    n_chunks = seq_len // CHUNK
    kern = functools.partial(_gdn_kernel, n_chunks=n_chunks, period=period, has_state=has_state)
    wide = hps * LANE
    npair = GDN_H // hps
    grid_spec = pltpu.PrefetchScalarGridSpec(
        num_scalar_prefetch=1,
        grid=(nseq, npair),
        in_specs=[pl.BlockSpec((seq_len, wide), lambda b, p, ab: (row_block0 + b, COL_DQ // hps + p)),
                  pl.BlockSpec((seq_len, wide), lambda b, p, ab: (row_block0 + b, COL_DK // hps + p)),
                  pl.BlockSpec((seq_len, wide), lambda b, p, ab: (row_block0 + b, COL_DV // hps + p)),
                  pl.BlockSpec((seq_len, LANE), lambda b, p, ab: (row_block0 + b, 0)),
                  pl.BlockSpec((3, wide), lambda b, p, ab: (0, p)),
                  pl.BlockSpec((3, wide), lambda b, p, ab: (0, npair + p)),
                  pl.BlockSpec((3, wide), lambda b, p, ab: (0, 2 * npair + p)),
                  pl.BlockSpec((1, 2, hps, GDN_DK, GDN_DV), s0_map)],
        out_specs=(pl.BlockSpec((seq_len, wide), lambda b, p, ab: (b, p)),
                   pl.BlockSpec((1, 2, hps, GDN_DK, GDN_DV), lambda b, p, ab: (b, 0, p, 0, 0))),
        scratch_shapes=[pltpu.VMEM((hps, seq_len, LANE), F32),
                        pltpu.VMEM((hps, seq_len, LANE), F32),
                        pltpu.VMEM((hps, seq_len, LANE), F32),
                        pltpu.VMEM((hps, 2, seq_len, LANE), F32),
                        pltpu.VMEM((hps, 2, seq_len, LANE), F32),
                        pltpu.VMEM((hps, 2, seq_len, GDN_DV), F32),
                        pltpu.VMEM((hps, 2, 2 * seq_len, GDN_DK), BF16),
                        pltpu.VMEM((hps, 2, seq_len, CHUNK), BF16),
                        pltpu.VMEM((hps, 2, seq_len, GDN_DK), BF16),
                        pltpu.VMEM((hps, 2, n_chunks, 8, LANE), F32),
                        pltpu.VMEM((seq_len, wide), F32),
                        pltpu.VMEM((hps, 2, GDN_DK, GDN_DV), F32)],
    )
    return pl.pallas_call(
        kern,
        out_shape=(jax.ShapeDtypeStruct((nseq * seq_len, GDN_W), F32),
                   jax.ShapeDtypeStruct((nseq, 2, GDN_H, GDN_DK, GDN_DV), F32)),
        grid_spec=grid_spec,
        compiler_params=_params(("arbitrary", "arbitrary")),
        name="gdn",
    )(ab, pm, pm, pm, ps, conv_w, conv_w, conv_w, s0)


def _s5_kernel(u_ref, mi_ref, wst_ref, wout_ref, lam_ref, h0_ref, y_ref, fin_ref, e_ref, x_ref,
               *, n_chunks, nseq):
    half = 2 * S5_P
    u = u_ref[0]
    e_ref[...] = _dot(u, wst_ref[0])
    y_ref[0] = _dot(u, mi_ref[0])
    lam = lam_ref[0]
    l1f, l2f, l1b, l2b = lam[0:1, :], lam[1:2, :], lam[2:3, :], lam[3:4, :]
    h0 = h0_ref[0]

    def step(c, carry):
        xf, xb = carry
        rf = pl.ds(pl.multiple_of(c * nseq, nseq), nseq)
        rb = pl.ds(pl.multiple_of((n_chunks - 1 - c) * nseq, nseq), nseq)
        x_ref[rf, 0:half] = xf
        x_ref[rb, half:2 * half] = xb
        xf = xf * l1f + pltpu.roll(xf, S5_P, 1) * l2f + e_ref[rf, 0:half]
        xb = xb * l1b + pltpu.roll(xb, S5_P, 1) * l2b + e_ref[rb, half:2 * half]
        return xf, xb

    xf, xb = lax.fori_loop(0, n_chunks, step, (h0[:, 0:half], h0[:, half:2 * half]))
    y_ref[0] += _dot(x_ref[...].astype(BF16), wout_ref[0])
    fin_ref[0, :, 0:half] = xf
    fin_ref[0, :, half:2 * half] = xb


def _s5(u, m_intra, w_st, w_out, lam_t, h0, *, nseq, n_chunks):
    rows = n_chunks * nseq
    kern = functools.partial(_s5_kernel, n_chunks=n_chunks, nseq=nseq)
    wspec = pl.BlockSpec((1, 2 * LANE, 2 * LANE), lambda g: (g, 0, 0))
    return pl.pallas_call(
        kern,
        out_shape=(jax.ShapeDtypeStruct((S5_G, rows, 2 * LANE), F32),
                   jax.ShapeDtypeStruct((S5_G, nseq, 2 * LANE), F32)),
        grid=(S5_G,),
        in_specs=[pl.BlockSpec((1, rows, 2 * LANE), lambda g: (g, 0, 0)),
                  wspec, wspec, wspec,
                  pl.BlockSpec((1, 4, LANE), lambda g: (g, 0, 0)),
                  pl.BlockSpec((1, nseq, 2 * LANE), lambda g: (g, 0, 0))],
        out_specs=(pl.BlockSpec((1, rows, 2 * LANE), lambda g: (g, 0, 0)),
                   pl.BlockSpec((1, nseq, 2 * LANE), lambda g: (g, 0, 0))),
        scratch_shapes=[pltpu.VMEM((rows, 2 * LANE), F32), pltpu.VMEM((rows, 2 * LANE), F32)],
        compiler_params=_params(("arbitrary",)),
        name="s5",
    )(u, m_intra, w_st, w_out, lam_t, h0)


def _cmul(ar, ai, br, bi):
    return ar * br - ai * bi, ar * bi + ai * br


def _s5_weights(lam_re, lam_im, log_dt, b_re, b_im, c_re, c_im):
    t = S5_T
    dt = jnp.exp(log_dt)[..., None]
    tau = jnp.arange(t + 1, dtype=F32)[:, None, None, None]
    mag = jnp.exp(tau * (lam_re * dt)[None])
    ang = tau * (lam_im * dt)[None]
    pw_re, pw_im = mag * jnp.cos(ang), mag * jnp.sin(ang)
    nr, ni = pw_re[1] - 1.0, pw_im[1]
    den = lam_re * lam_re + lam_im * lam_im
    fr, fi = (nr * lam_re + ni * lam_im) / den, (ni * lam_re - nr * lam_im) / den
    bb_re, bb_im = _cmul(fr[..., None], fi[..., None], b_re[None], b_im[None])

    def kern_tau(d):
        cl_re, cl_im = _cmul(c_re[d][None], c_im[d][None], pw_re[:t, d, :, None, :], pw_im[:t, d, :, None, :])
        return (jnp.einsum('tgop,gpi->tgoi', cl_re, bb_re[d], precision=HI)
                - jnp.einsum('tgop,gpi->tgoi', cl_im, bb_im[d], precision=HI))

    kf, kb = kern_tau(0), kern_tau(1)
    s_idx = jnp.arange(t)[:, None]
    t_idx = jnp.arange(t)[None, :]
    lag_f = jnp.clip(t_idx - s_idx, 0, t - 1)
    lag_b = jnp.clip(s_idx - t_idx, 0, t - 1)
    a_f = jnp.where((t_idx >= s_idx)[:, :, None, None, None], kf[lag_f], 0.0)
    a_b = jnp.where((s_idx >= t_idx)[:, :, None, None, None], kb[lag_b], 0.0)
    m_intra = (a_f + a_b).transpose(2, 0, 4, 1, 3).reshape(S5_G, t * S5_GH, t * S5_GH)

    def st(d, pw_idx):
        r, i = _cmul(pw_re[pw_idx, d][..., None], pw_im[pw_idx, d][..., None], bb_re[d][None], bb_im[d][None])
        return r, i
    fre, fim = st(0, jnp.arange(t - 1, -1, -1))
    bre, bim = st(1, jnp.arange(t))
    w_st = jnp.concatenate([fre, fim, bre, bim], axis=2)
    w_st = w_st.transpose(1, 0, 3, 2).reshape(S5_G, t * S5_GH, 4 * S5_P)

    def ro(d, pw_idx):
        r, i = _cmul(c_re[d][None], c_im[d][None], pw_re[pw_idx, d][:, :, None, :], pw_im[pw_idx, d][:, :, None, :])
        return r, -i
    fr_, fi_ = ro(0, jnp.arange(1, t + 1))
    br_, bi_ = ro(1, jnp.arange(t, 0, -1))
    w_out = jnp.concatenate([fr_, fi_, br_, bi_], axis=3)
    w_out = w_out.transpose(1, 3, 0, 2).reshape(S5_G, 4 * S5_P, t * S5_GH)

    lt_re, lt_im = pw_re[t], pw_im[t]
    lam_t = jnp.stack([jnp.concatenate([lt_re[0], lt_re[0]], -1), jnp.concatenate([-lt_im[0], lt_im[0]], -1),
                       jnp.concatenate([lt_re[1], lt_re[1]], -1), jnp.concatenate([-lt_im[1], lt_im[1]], -1)],
                      axis=1)
    return m_intra.astype(BF16), w_st.astype(BF16), w_out.astype(BF16), lam_t


def _gelu_tanh(x):
    return 0.5 * x * (1.0 + jnp.tanh(math.sqrt(2.0 / math.pi) * (x + 0.044715 * (x * x * x))))


def _out_kernel(x_ref, ada_ref, og_ref, gg_ref, ys_ref, su_ref, sg_ref, od_ref, dg_ref,
                gn_ref, dn_ref, sd_ref, gw_ref, gb_ref, wo_ref, np_ref, o_ref):
    def head_norm(o, g):
        parts = []
        for h in range(o.shape[1] // LANE):
            oh = o[:, h * LANE:(h + 1) * LANE]
            parts.append(oh * lax.rsqrt(jnp.mean(oh * oh, axis=-1, keepdims=True) + EPS) * g)
        return jnp.concatenate(parts, axis=1)

    o_gla = head_norm(og_ref[...], gn_ref[...]) * _silu(gg_ref[...])
    y = _gelu_tanh(ys_ref[...] + sd_ref[...] * su_ref[...])
    y = y * jax.nn.sigmoid(_dot(y.astype(BF16), gw_ref[...]) + gb_ref[...])
    o_s5 = y * _silu(sg_ref[...])
    o_gdn = head_norm(od_ref[...], dn_ref[...]) * _silu(dg_ref[...])
    out = (_dot(o_gla.astype(BF16), wo_ref[0:GLA_W, :])
           + _dot(o_s5.astype(BF16), wo_ref[GLA_W:GLA_W + S5_W, :])
           + _dot(o_gdn.astype(BF16), wo_ref[GLA_W + S5_W:MIX_W, :]))
    r = out * lax.rsqrt(jnp.mean(out * out, axis=-1, keepdims=True) + EPS) * np_ref[...]
    gate = ada_ref[0][:, 2 * D_MODEL:3 * D_MODEL]
    o_ref[...] = x_ref[...] + gate * r


def _out(x, ada_l, pm, o_gla, y_s5, o_gdn, gla_norm, gdn_norm, s5_d, glu_w, glu_b, w_out, norm_post,
         ctx_tiles, tiles_per_latent):
    nt = x.shape[0]
    row = functools.partial(_ada_row, ctx_tiles=ctx_tiles, tiles_per_latent=tiles_per_latent)
    wide = 4 * LANE

    def tok(col_block):
        return pl.BlockSpec((TOKEN_TILE, wide), lambda i: (i, col_block))

    def full(shape):
        return pl.BlockSpec(shape, lambda i: (0,) * len(shape))

    return pl.pallas_call(
        _out_kernel,
        out_shape=jax.ShapeDtypeStruct((nt, D_MODEL), F32),
        grid=(nt // TOKEN_TILE,),
        in_specs=[pl.BlockSpec((TOKEN_TILE, D_MODEL), lambda i: (i, 0)),
                  pl.BlockSpec((1, 1, 3 * D_MODEL), lambda i: (row(i), 0, 0)),
                  tok(0), tok(COL_GGATE // 4), tok(0), tok(COL_SU // 4), tok(COL_SGATE // 4),
                  tok(0), tok(COL_DGATE // 4),
                  full((1, LANE)), full((1, LANE)), full((1, S5_W)), full((S5_W, S5_W)), full((1, S5_W)),
                  full((MIX_W, D_MODEL)), full((1, D_MODEL))],
        out_specs=pl.BlockSpec((TOKEN_TILE, D_MODEL), lambda i: (i, 0)),
        compiler_params=_params(("arbitrary",)),
        name="outproj",
    )(x, ada_l, o_gla, pm, y_s5, pm, pm, o_gdn, pm, gla_norm, gdn_norm, s5_d, glu_w, glu_b, w_out, norm_post)


def _permute_w_in(w_in):
    offs = {}
    pos = 0
    for name, width in (('gq', GLA_QK), ('gk', GLA_QK), ('gv', GLA_W), ('glr', 2 * GLA_LR), ('ggate', GLA_W),
                        ('su', S5_W), ('sgate', S5_W), ('dqkv', 3 * GDN_W), ('da', 2 * GDN_H),
                        ('db', 2 * GDN_H), ('dgate', GDN_W)):
        offs[name] = (pos, pos + width)
        pos += width
    seg = lambda n: w_in[..., offs[n][0]:offs[n][1]]
    main = jnp.concatenate([seg('gq'), seg('gk'), seg('gv'), seg('ggate'), seg('su'), seg('sgate'),
                            seg('dqkv'), seg('dgate')], axis=-1)
    small = jnp.concatenate([seg('glr'), seg('da'), seg('db')], axis=-1)
    small = jnp.pad(small, ((0, 0), (0, 0), (0, LANE - small.shape[-1])))
    return main.astype(BF16), small.astype(BF16)


def _s5_to_groups(su, nseq, latent):
    if latent:
        x = su.reshape(nseq, -1, GRID_W, S5_G, S5_GH).transpose(3, 2, 0, 1, 4)
    else:
        x = su.reshape(nseq, -1, S5_T, S5_G, S5_GH).transpose(3, 1, 0, 2, 4)
    return x.reshape(S5_G, -1, S5_T * S5_GH)


def _s5_from_groups(y, nseq, latent):
    if latent:
        x = y.reshape(S5_G, GRID_W, nseq, -1, S5_GH).transpose(2, 3, 1, 0, 4)
    else:
        x = y.reshape(S5_G, -1, nseq, S5_T, S5_GH).transpose(2, 1, 3, 0, 4)
    return x.reshape(-1, S5_W)


def kernel(x_prompt, x_sample, c, state_gla, state_s5_re, state_s5_im, state_gdn, c_ctx, norm_pre, norm_post, w_ada, b_ada, w_in, gla_gate_w, gla_gate_b, gla_norm, s5_lam_re, s5_lam_im, s5_log_dt, s5_b_re, s5_b_im, s5_c_re, s5_c_im, s5_d, s5_glu_w, s5_glu_b, gdn_conv, gdn_a_log, gdn_dt_bias, gdn_norm, w_out):
    bp, lp, _ = x_prompt.shape
    bs, ls, _ = x_sample.shape
    n_ctx = bp * lp
    assert lp % TOKEN_TILE == 0 and ls % TOKEN_TILE == 0 and n_ctx % ls == 0
    assert ls // GRID_W == S5_T and lp % S5_T == 0
    ctx_tiles = n_ctx // TOKEN_TILE
    tiles_per_latent = ls // TOKEN_TILE

    cond = jnp.concatenate([c_ctx[None].astype(F32), c.astype(F32)], axis=0)
    rows = -(-cond.shape[0] // 8) * 8
    cond = jnp.pad(cond, ((0, rows - cond.shape[0]), (0, 0)))
    ada = _adaln(cond, w_ada.astype(F32), b_ada.astype(F32)).reshape(DEPTH, rows, 1, 3 * D_MODEL)

    w_main, w_small = _permute_w_in(w_in)
    w_out_bf = w_out.astype(BF16)
    glu_w_bf = s5_glu_w.astype(BF16)
    gw_pad = jnp.zeros((DEPTH, 2, LANE, GLA_QK), F32)
    for d in range(2):
        gw_pad = gw_pad.at[:, d, SM_GLR + d * GLA_LR:SM_GLR + (d + 1) * GLA_LR, :].set(gla_gate_w[:, d].astype(F32))
    gdn_ab = jnp.concatenate([gdn_a_log, gdn_dt_bias], axis=-1).astype(F32)

    x = jnp.concatenate([x_prompt.reshape(n_ctx, D_MODEL), x_sample.reshape(bs * ls, D_MODEL)], axis=0).astype(F32)
    gla_states, s5_states, gdn_states = [], [], []
    for l in range(DEPTH):
        pm, ps = _inproj(x, ada[l], norm_pre[l][None].astype(F32), w_main[l], w_small[l],
                         ctx_tiles, tiles_per_latent)
        gb = gla_gate_b[l].astype(F32).reshape(2, 1, GLA_QK)
        og_c, sg = _gla(pm, ps, gw_pad[l], gb, None, nseq=bp, seq_len=lp, row_block0=0)
        og_s, _ = _gla(pm, ps, gw_pad[l], gb, state_gla[:, l].astype(F32), nseq=bs, seq_len=ls,
                       row_block0=n_ctx // ls)
        conv_w = gdn_conv[l].astype(F32)
        od_c, sd = _gdn(pm, ps, gdn_ab[l], conv_w, None, nseq=bp, seq_len=lp, row_block0=0, period=lp)
        od_s, _ = _gdn(pm, ps, gdn_ab[l], conv_w, state_gdn[:, l].astype(F32), nseq=bs, seq_len=ls,
                       row_block0=n_ctx // ls, period=GRID_W)

        m_intra, w_st, w_ro, lam_t = _s5_weights(
            s5_lam_re[l].astype(F32), s5_lam_im[l].astype(F32), s5_log_dt[l].astype(F32),
            s5_b_re[l].astype(F32), s5_b_im[l].astype(F32), s5_c_re[l].astype(F32), s5_c_im[l].astype(F32))
        su = pm[:, COL_SU * LANE:COL_SU * LANE + S5_W].astype(BF16)
        u_c = _s5_to_groups(su[:n_ctx], bp, False)
        u_s = _s5_to_groups(su[n_ctx:], bs, True)
        h0_c = jnp.zeros((S5_G, bp, 4 * S5_P), F32)
        sre, sim = state_s5_re[:, l].astype(F32), state_s5_im[:, l].astype(F32)
        h0_s = jnp.concatenate([sre[:, 0], sim[:, 0], sre[:, 1], sim[:, 1]], axis=-1).transpose(1, 0, 2)
        y_c, fin = _s5(u_c, m_intra, w_st, w_ro, lam_t, h0_c, nseq=bp, n_chunks=lp // S5_T)
        y_s, _ = _s5(u_s, m_intra, w_st, w_ro, lam_t, h0_s, nseq=bs, n_chunks=GRID_W)
        y_s5 = jnp.concatenate([_s5_from_groups(y_c, bp, False), _s5_from_groups(y_s, bs, True)], axis=0)

        x = _out(x, ada[l], pm, jnp.concatenate([og_c, og_s], axis=0), y_s5,
                 jnp.concatenate([od_c, od_s], axis=0),
                 gla_norm[l][None].astype(F32), gdn_norm[l][None].astype(F32), s5_d[l][None].astype(F32),
                 glu_w_bf[l], s5_glu_b[l][None].astype(F32), w_out_bf[l], norm_post[l][None].astype(F32),
                 ctx_tiles, tiles_per_latent)
        gla_states.append(sg)
        gdn_states.append(sd)
        fin = fin.transpose(1, 0, 2).reshape(bp, S5_G, 2, 2, S5_P)
        s5_states.append(fin.transpose(0, 2, 3, 1, 4))

    dt = x_prompt.dtype
    s5_all = jnp.stack(s5_states, axis=1)
    y_prompt = x[:n_ctx].reshape(bp, lp, D_MODEL).astype(dt)
    y_sample = x[n_ctx:].reshape(bs, ls, D_MODEL).astype(x_sample.dtype)
    return (y_prompt, y_sample, jnp.stack(gla_states, axis=1).astype(dt),
            s5_all[:, :, :, 0].astype(dt), s5_all[:, :, :, 1].astype(dt),
            jnp.stack(gdn_states, axis=1).astype(dt))
```

```python
import functools
import math

import jax
import jax.numpy as jnp
import numpy as np
from jax import lax
from jax.experimental import pallas as pl
from jax.experimental.pallas import tpu as pltpu

F32 = jnp.float32
BF16 = jnp.bfloat16
HI = lax.Precision.HIGHEST

D_MODEL = 1024
DEPTH = 4
GRID_W = 64
CHUNK = 64
EPS = 1e-6
GLA_H, GLA_DK, GLA_DV, GLA_LR, GLA_TAU = 4, 64, 128, 16, 16.0
GLA_QK, GLA_W = GLA_H * GLA_DK, GLA_H * GLA_DV
S5_GH, S5_W, S5_P = 16, 512, 64
S5_G = S5_W // S5_GH
S5_T = 16
GDN_H, GDN_DK, GDN_DV = 4, 128, 128
GDN_W = GDN_H * GDN_DV
MIX_W = GLA_W + S5_W + GDN_W

LANE = 128
TOKEN_TILE = 256
VMEM_LIMIT = 48 * 1024 * 1024

COL_GQ, COL_GK, COL_GV, COL_GGATE, COL_SU, COL_SGATE = 0, 2, 4, 8, 12, 16
COL_DQ, COL_DK, COL_DV, COL_DGATE = 20, 24, 28, 32
MAIN_W = 36 * LANE
SM_GLR, SM_DA, SM_DB = 0, 32, 40


def _dot(a, b, precision=None):
    return lax.dot_general(a, b, (((1,), (0,)), ((), ())), precision=precision,
                           preferred_element_type=F32)


def _dot_nt(a, b, precision=None):
    return lax.dot_general(a, b, (((1,), (1,)), ((), ())), precision=precision,
                           preferred_element_type=F32)


def _dot_tn(a, b, precision=None):
    return lax.dot_general(a, b, (((0,), (0,)), ((), ())), precision=precision,
                           preferred_element_type=F32)


def _silu(x):
    return x * jax.nn.sigmoid(x)


def _softplus(x):
    return jnp.maximum(x, 0.0) + jnp.log1p(jnp.exp(-jnp.abs(x)))


def _params(sem):
    return pltpu.CompilerParams(dimension_semantics=sem, vmem_limit_bytes=VMEM_LIMIT)


def _adaln_kernel(cond_ref, w_ref, b_ref, o_ref):
    c = cond_ref[...]
    o_ref[0] = _dot(_silu(c), w_ref[0], precision=HI) + b_ref[0]


def _adaln(cond, w_ada, b_ada):
    rows = cond.shape[0]
    nj = 3 * D_MODEL // 1024
    return pl.pallas_call(
        _adaln_kernel,
        out_shape=jax.ShapeDtypeStruct((DEPTH, rows, 3 * D_MODEL), F32),
        grid=(DEPTH, nj),
        in_specs=[pl.BlockSpec((rows, D_MODEL), lambda l, j: (0, 0)),
                  pl.BlockSpec((1, D_MODEL, 1024), lambda l, j: (l, 0, j)),
                  pl.BlockSpec((1, 1, 1024), lambda l, j: (l, 0, j))],
        out_specs=pl.BlockSpec((1, rows, 1024), lambda l, j: (l, 0, j)),
        compiler_params=_params(("arbitrary", "arbitrary")),
        name="adaln",
    )(cond, w_ada, b_ada.reshape(DEPTH, 1, 3 * D_MODEL))


def _wprep_kernel(w_ref, m_ref, s_ref):
    x = w_ref[0]
    o = _W_IN_OFFS
    main = jnp.concatenate([x[:, o['gq'][0]:o['gv'][1]], x[:, o['ggate'][0]:o['dqkv'][1]],
                            x[:, o['dgate'][0]:o['dgate'][1]]], axis=1)
    small = jnp.concatenate([x[:, o['glr'][0]:o['glr'][1]], x[:, o['da'][0]:o['db'][1]],
                             jnp.zeros((x.shape[0], LANE - 2 * GLA_LR - 4 * GDN_H), x.dtype)], axis=1)
    m_ref[0] = main.astype(BF16)
    s_ref[0] = small.astype(BF16)


def _w_in_offsets():
    offs, pos = {}, 0
    for name, width in (('gq', GLA_QK), ('gk', GLA_QK), ('gv', GLA_W), ('glr', 2 * GLA_LR), ('ggate', GLA_W),
                        ('su', S5_W), ('sgate', S5_W), ('dqkv', 3 * GDN_W), ('da', 2 * GDN_H),
                        ('db', 2 * GDN_H), ('dgate', GDN_W)):
        offs[name] = (pos, pos + width)
        pos += width
    return offs


_W_IN_OFFS = _w_in_offsets()


def _wprep(w_in):
    depth, d_model, in_dim = w_in.shape
    rows = 256
    return pl.pallas_call(
        _wprep_kernel,
        out_shape=(jax.ShapeDtypeStruct((depth, d_model, MAIN_W), BF16),
                   jax.ShapeDtypeStruct((depth, d_model, LANE), BF16)),
        grid=(depth, d_model // rows),
        in_specs=[pl.BlockSpec((1, rows, in_dim), lambda l, i: (l, i, 0))],
        out_specs=(pl.BlockSpec((1, rows, MAIN_W), lambda l, i: (l, i, 0)),
                   pl.BlockSpec((1, rows, LANE), lambda l, i: (l, i, 0))),
        compiler_params=_params(("arbitrary", "arbitrary")),
        name="wprep",
    )(w_in)


def _inproj_kernel(xc_ref, xs_ref, ada_ref, g_ref, wm_ref, ws_ref, om_ref, os_ref, *, ctx_tiles):
    x = jnp.where(pl.program_id(0) < ctx_tiles, xc_ref[...], xs_ref[...])
    nrm = x * lax.rsqrt(jnp.mean(x * x, axis=-1, keepdims=True) + EPS) * g_ref[...]
    ada = ada_ref[0]
    shift = ada[:, 0:D_MODEL]
    scale = ada[:, D_MODEL:2 * D_MODEL]
    h = (nrm * (1.0 + scale) + shift).astype(BF16)
    om_ref[...] = _dot(h, wm_ref[...])
    os_ref[...] = _dot(h, ws_ref[...])


def _ada_row(i, ctx_tiles, tiles_per_latent):
    return jnp.where(i < ctx_tiles, 0, 1 + (i - ctx_tiles) // tiles_per_latent)


def _ctx_tile(i, ctx_tiles):
    return jnp.minimum(i, ctx_tiles - 1)


def _lat_tile(i, ctx_tiles):
    return jnp.maximum(i - ctx_tiles, 0)


def _inproj(x_c, x_s, ada_l, g_pre, w_main, w_small, ctx_tiles, tiles_per_latent):
    nt = x_c.shape[0] + x_s.shape[0]
    row = functools.partial(_ada_row, ctx_tiles=ctx_tiles, tiles_per_latent=tiles_per_latent)
    return pl.pallas_call(
        functools.partial(_inproj_kernel, ctx_tiles=ctx_tiles),
        out_shape=(jax.ShapeDtypeStruct((nt, MAIN_W), F32), jax.ShapeDtypeStruct((nt, LANE), F32)),
        grid=(nt // TOKEN_TILE,),
        in_specs=[pl.BlockSpec((TOKEN_TILE, D_MODEL), lambda i: (_ctx_tile(i, ctx_tiles), 0)),
                  pl.BlockSpec((TOKEN_TILE, D_MODEL), lambda i: (_lat_tile(i, ctx_tiles), 0)),
                  pl.BlockSpec((1, 1, 3 * D_MODEL), lambda i: (row(i), 0, 0)),
                  pl.BlockSpec((1, D_MODEL), lambda i: (0, 0)),
                  pl.BlockSpec((D_MODEL, MAIN_W), lambda i: (0, 0)),
                  pl.BlockSpec((D_MODEL, LANE), lambda i: (0, 0))],
        out_specs=(pl.BlockSpec((TOKEN_TILE, MAIN_W), lambda i: (i, 0)),
                   pl.BlockSpec((TOKEN_TILE, LANE), lambda i: (i, 0))),
        compiler_params=_params(("arbitrary",)),
        name="inproj",
    )(x_c, x_s, ada_l, g_pre, w_main, w_small)


def _tri_mask(n, reverse, strict=False):
    r = lax.broadcasted_iota(jnp.int32, (n, n), 0)
    c = lax.broadcasted_iota(jnp.int32, (n, n), 1)
    if reverse:
        return (r < c) if strict else (r <= c)
    return (r > c) if strict else (r >= c)


def _eye_mask(n):
    return lax.broadcasted_iota(jnp.int32, (n, n), 0) == lax.broadcasted_iota(jnp.int32, (n, n), 1)


GLA_CPI = 2


def _gla_kernel(q_ref, k_ref, v_ref, sm_ref, gw_ref, gb_ref, s0_ref, o_ref, sf_ref, g_s, ob_s, st_ref,
                *, n_chunks, has_state):
    for d in range(2):
        if has_state:
            s0 = jnp.concatenate([s0_ref[0, d, 0], s0_ref[0, d, 1]], axis=0)
            st_ref[d] = s0.T
        else:
            st_ref[d] = jnp.zeros((GLA_DV, LANE), F32)
        z = _dot(sm_ref[...], gw_ref[d], precision=HI) + gb_ref[d]
        g_s[d] = -_softplus(-z) * (1.0 / GLA_TAU)

    lane = lax.broadcasted_iota(jnp.int32, (CHUNK, LANE), 1)
    head_mask = [lane < GLA_DK, lane >= GLA_DK]
    scale = GLA_DK ** -0.5
    causal = [_tri_mask(CHUNK, False), _tri_mask(CHUNK, True)]
    causal_bf = [m.astype(BF16) for m in causal]

    def chunk_step(it, carry):
        jd = [(j, d) for j in range(GLA_CPI) for d in range(2)]
        jdh = [(j, d, h) for (j, d) in jd for h in range(2)]
        cc = {(j, d): (it * GLA_CPI + j) if d == 0 else n_chunks - 1 - (it * GLA_CPI + j) for (j, d) in jd}
        rows = {u: pl.ds(pl.multiple_of(cc[u] * CHUNK, CHUNK), CHUNK) for u in jd}
        q = {u: q_ref[rows[u], :] for u in jd}
        k = {u: k_ref[rows[u], :] for u in jd}
        v = {u: v_ref[rows[u], :] for u in jd}
        g = {(j, d): g_s[d, rows[j, d], :] for (j, d) in jd}
        g_hi = {u: g[u].astype(BF16) for u in jd}
        g_lo = {u: (g[u] - g_hi[u].astype(F32)).astype(BF16) for u in jd}
        b = {u: _dot(causal_bf[u[1]], g_hi[u]) + _dot(causal_bf[u[1]], g_lo[u]) for u in jd}
        b_last = {(j, d): b[j, d][CHUNK - 1:CHUNK, :] if d == 0 else b[j, d][0:1, :] for (j, d) in jd}
        qe = {u: q[u] * jnp.exp(b[u]) * scale for u in jd}
        ke = {u: (k[u] * jnp.exp(-b[u])).astype(BF16) for u in jd}
        kd = {u: k[u] * jnp.exp(b_last[u] - b[u]) for u in jd}
        qh = {(j, d, h): jnp.where(head_mask[h], qe[j, d], 0.0).astype(BF16) for (j, d, h) in jdh}
        vh = {(j, d, h): v[j, d][:, h * GLA_DV:(h + 1) * GLA_DV].astype(BF16) for (j, d, h) in jdh}
        sc = {(j, d, h): jnp.where(causal[d], _dot_nt(qh[j, d, h], ke[j, d]), 0.0).astype(BF16)
              for (j, d, h) in jdh}
        upd = {(j, d, h): _dot_tn(vh[j, d, h], jnp.where(head_mask[h], kd[j, d], 0.0).astype(BF16))
               for (j, d, h) in jdh}
        o_intra = {u: _dot(sc[u], vh[u]) for u in jdh}
        for j in range(GLA_CPI):
            st_bf = {d: st_ref[d].astype(BF16) for d in range(2)}
            for d in range(2):
                for h in range(2):
                    oh = o_intra[j, d, h] + _dot_nt(qh[j, d, h], st_bf[d])
                    if d == 0:
                        o_ref[rows[j, d], h * GLA_DV:(h + 1) * GLA_DV] = oh
                    else:
                        ob_s[rows[j, d], h * GLA_DV:(h + 1) * GLA_DV] = oh
                st_ref[d] = st_ref[d] * jnp.exp(b_last[j, d]) + upd[j, d, 0] + upd[j, d, 1]
        return carry

    lax.fori_loop(0, n_chunks // GLA_CPI, chunk_step, 0)
    o_ref[...] += ob_s[...]
    for d in range(2):
        s = st_ref[d].T
        for h in range(2):
            sf_ref[0, d, h] = s[h * GLA_DK:(h + 1) * GLA_DK, :]


def _gla(pm, ps, gw_pad, gb, s0, *, nseq, seq_len, row_block0):
    has_state = s0 is not None
    if s0 is None:
        s0 = jnp.zeros((1, 2, GLA_H, GLA_DK, GLA_DV), F32)
        s0_map = lambda b, p: (0, 0, p, 0, 0)
    else:
        s0_map = lambda b, p: (b, 0, p, 0, 0)
    kern = functools.partial(_gla_kernel, n_chunks=seq_len // CHUNK, has_state=has_state)
    return pl.pallas_call(
        kern,
        out_shape=(jax.ShapeDtypeStruct((nseq * seq_len, GLA_W), F32),
                   jax.ShapeDtypeStruct((nseq, 2, GLA_H, GLA_DK, GLA_DV), F32)),
        grid=(nseq, 2),
        in_specs=[pl.BlockSpec((seq_len, LANE), lambda b, p: (row_block0 + b, COL_GQ + p)),
                  pl.BlockSpec((seq_len, LANE), lambda b, p: (row_block0 + b, COL_GK + p)),
                  pl.BlockSpec((seq_len, 2 * LANE), lambda b, p: (row_block0 + b, COL_GV // 2 + p)),
                  pl.BlockSpec((seq_len, LANE), lambda b, p: (row_block0 + b, 0)),
                  pl.BlockSpec((2, LANE, LANE), lambda b, p: (0, 0, p)),
                  pl.BlockSpec((2, 1, LANE), lambda b, p: (0, 0, p)),
                  pl.BlockSpec((1, 2, 2, GLA_DK, GLA_DV), s0_map)],
        out_specs=(pl.BlockSpec((seq_len, 2 * LANE), lambda b, p: (b, p)),
                   pl.BlockSpec((1, 2, 2, GLA_DK, GLA_DV), lambda b, p: (b, 0, p, 0, 0))),
        scratch_shapes=[pltpu.VMEM((2, seq_len, LANE), F32), pltpu.VMEM((seq_len, 2 * LANE), F32),
                        pltpu.VMEM((2, GLA_DV, LANE), F32)],
        compiler_params=_params(("arbitrary", "arbitrary")),
        name="gla",
    )(pm, pm, pm, ps, gw_pad, gb, s0)


GDN_HPS = 2
GDN_CPI = 2


def _block_mask(n, s, reverse):
    r = lax.broadcasted_iota(jnp.int32, (n, n), 0)
    c = lax.broadcasted_iota(jnp.int32, (n, n), 1)
    if reverse:
        r, c = c, r
    sh = s.bit_length() - 1
    same_pair = (r >> (sh + 1)) == (c >> (sh + 1))
    return same_pair & (((r >> sh) & 1) == 1) & (((c >> sh) & 1) == 0)


def _gdn_kernel(ab_ref, q_ref, k_ref, v_ref, sm_ref, cwq_ref, cwk_ref, cwv_ref, s0_ref,
                o_ref, sf_ref, q_s, k_s, v_s, g_s, b_s, u_s, wq_s, a_s, kd_s, dl_s, ob_s, st_ref,
                *, n_chunks, period, has_state):
    hp = pl.program_id(1)
    seq_len = n_chunks * CHUNK
    row = lax.broadcasted_iota(jnp.int32, (seq_len, LANE), 0)
    first = (row % period) == 0
    last = (row % period) == period - 1

    def conv_silu(x, w):
        xp = jnp.where(first, 0.0, pltpu.roll(x, 1, 0))
        xn = jnp.where(last, 0.0, pltpu.roll(x, seq_len - 1, 0))
        return _silu(xp * w[0:1, :] + x * w[1:2, :] + xn * w[2:3, :])

    def l2norm(x):
        return x * lax.rsqrt(jnp.sum(x * x, axis=-1, keepdims=True) + EPS)

    lane1 = lax.broadcasted_iota(jnp.int32, (1, LANE), 1)
    a_log = jnp.zeros((1, LANE), F32)
    dt_bias = jnp.zeros((1, LANE), F32)
    for d in range(2):
        for hh in range(GDN_H):
            a_log = jnp.where(lane1 == SM_DA + d * GDN_H + hh, ab_ref[d, hh], a_log)
            dt_bias = jnp.where(lane1 == SM_DA + d * GDN_H + hh, ab_ref[d, GDN_H + hh], dt_bias)
    sm = sm_ref[...]
    g_all = -jnp.exp(a_log) * _softplus(sm + dt_bias)
    b_all = jax.nn.sigmoid(sm)
    lane = lax.broadcasted_iota(jnp.int32, (seq_len, LANE), 1)

    def lane_bcast(x, j):
        col = jnp.sum(jnp.where(lane == j, x, 0.0), axis=1, keepdims=True)
        return jnp.broadcast_to(col, (seq_len, LANE))

    for h in range(GDN_HPS):
        hd = hp * GDN_HPS + h
        lanes = slice(h * LANE, (h + 1) * LANE)
        q_s[h] = l2norm(conv_silu(q_ref[:, lanes], cwq_ref[:, lanes])) * GDN_DK ** -0.5
        k_s[h] = l2norm(conv_silu(k_ref[:, lanes], cwk_ref[:, lanes]))
        v_s[h] = conv_silu(v_ref[:, lanes], cwv_ref[:, lanes])
        for d in range(2):
            g_s[h, d] = lane_bcast(g_all, SM_DA + d * GDN_H + hd)
            b_s[h, d] = lane_bcast(b_all, SM_DB + d * GDN_H + hd)
            if has_state:
                st_ref[h, d] = s0_ref[0, d, h]
            else:
                st_ref[h, d] = jnp.zeros((GDN_DK, GDN_DV), F32)

    ones = jnp.ones((CHUNK, CHUNK), F32)
    causal = [_tri_mask(CHUNK, False), _tri_mask(CHUNK, True)]
    causal_f = [m.astype(F32) for m in causal]
    strict_f = [_tri_mask(CHUNK, d == 1, strict=True).astype(F32) for d in range(2)]
    level_masks = [[_block_mask(CHUNK, 1 << j, d == 1).astype(F32) for j in range(CHUNK.bit_length() - 1)]
                   for d in range(2)]

    causal_bf = [m.astype(BF16) for m in causal]
    ones_bf = ones.astype(BF16)
    eye_f = _eye_mask(CHUNK).astype(F32)

    def phase_a(it, carry):
        cs = [it * GDN_CPI + j for j in range(GDN_CPI)]
        rows = [pl.ds(pl.multiple_of(c * CHUNK, CHUNK), CHUNK) for c in cs]
        rows2 = [pl.ds(pl.multiple_of(c * 2 * CHUNK, 2 * CHUNK), CHUNK) for c in cs]
        rows2b = [pl.ds(pl.multiple_of(c * 2 * CHUNK + CHUNK, CHUNK), CHUNK) for c in cs]
        pairs = [(j, h) for j in range(GDN_CPI) for h in range(GDN_HPS)]
        units = [(j, h, d) for (j, h) in pairs for d in range(2)]
        q = {(j, h): q_s[h, rows[j], :] for (j, h) in pairs}
        k = {(j, h): k_s[h, rows[j], :] for (j, h) in pairs}
        v = {(j, h): v_s[h, rows[j], :] for (j, h) in pairs}
        k_bf = {p: k[p].astype(BF16) for p in pairs}
        g = {(j, h, d): g_s[h, d, rows[j], :] for (j, h, d) in units}
        beta = {(j, h, d): b_s[h, d, rows[j], :] for (j, h, d) in units}
        g_hi = {u: g[u].astype(BF16) for u in units}
        g_lo = {u: (g[u] - g_hi[u].astype(F32)).astype(BF16) for u in units}
        gm = {u: g[u][:, :CHUNK] * causal_f[1 - u[2]] for u in units}
        gm_hi = {u: gm[u].astype(BF16) for u in units}
        gm_lo = {u: (gm[u] - gm_hi[u].astype(F32)).astype(BF16) for u in units}
        gc = {u: _dot(causal_bf[u[2]], g_hi[u]) + _dot(causal_bf[u[2]], g_lo[u]) for u in units}
        gc_t = {u: _dot(ones_bf, gm_hi[u]) + _dot(ones_bf, gm_lo[u]) for u in units}
        qk = {p: _dot_nt(q[p].astype(BF16), k_bf[p]) for p in pairs}
        kk = {p: _dot_nt(k_bf[p], k_bf[p]) for p in pairs}
        decay, m, t = {}, {}, {}
        for u in units:
            j, h, d = u
            diff = gc[u][:, :CHUNK] - gc_t[u]
            decay[u] = jnp.where(causal[d], jnp.exp(jnp.where(causal[d], diff, 0.0)), 0.0)
            m[u] = kk[j, h] * beta[u][:, :CHUNK] * decay[u] * strict_f[d]
            t[u] = eye_f - m[u] * level_masks[d][0]
        for lvl in range(1, len(level_masks[0])):
            t_bf = {u: t[u].astype(BF16) for u in units}
            p1 = {u: _dot(t_bf[u], (m[u] * level_masks[u[2]][lvl]).astype(BF16)).astype(BF16) for u in units}
            t = {u: t[u] - _dot(p1[u], t_bf[u]) for u in units}
        egc = {u: jnp.exp(gc[u]) for u in units}
        rhs = {(j, h, d): jnp.concatenate([v[j, h] * beta[j, h, d], k[j, h] * (beta[j, h, d] * egc[j, h, d])],
                                          axis=1).astype(BF16) for (j, h, d) in units}
        uw = {u: _dot(t[u].astype(BF16), rhs[u]) for u in units}
        for u in units:
            j, h, d = u
            gc_last = gc[u][CHUNK - 1:CHUNK, :] if d == 0 else gc[u][0:1, :]
            u_s[h, d, rows[j], :] = uw[u][:, 0:GDN_DV]
            wq_s[h, d, rows2[j], :] = uw[u][:, GDN_DV:GDN_DV + GDN_DK].astype(BF16)
            wq_s[h, d, rows2b[j], :] = (q[j, h] * egc[u]).astype(BF16)
            a_s[h, d, rows[j], :] = (qk[j, h] * decay[u]).astype(BF16)
            kd_s[h, d, rows[j], :] = (k[j, h] * jnp.exp(gc_last - gc[u])).astype(BF16)
            dl_s[h, d, cs[j]] = jnp.broadcast_to(jnp.exp(gc_last), (8, LANE))
        return carry

    lax.fori_loop(0, n_chunks // GDN_CPI, phase_a, 0)

    def phase_b(i, carry):
        chains = [(h, d) for h in range(GDN_HPS) for d in range(2)]
        cc = {0: i, 1: n_chunks - 1 - i}
        rows = {d: pl.ds(pl.multiple_of(cc[d] * CHUNK, CHUNK), CHUNK) for d in range(2)}
        rows2 = {d: pl.ds(pl.multiple_of(cc[d] * 2 * CHUNK, 2 * CHUNK), 2 * CHUNK) for d in range(2)}
        s = {hd: st_ref[hd[0], hd[1]] for hd in chains}
        ws = {(h, d): _dot(wq_s[h, d, rows2[d], :], s[h, d].astype(BF16)) for (h, d) in chains}
        v_new = {(h, d): (u_s[h, d, rows[d], :] - ws[h, d][0:CHUNK, :]).astype(BF16) for (h, d) in chains}
        o = {(h, d): ws[h, d][CHUNK:2 * CHUNK, :] + _dot(a_s[h, d, rows[d], :], v_new[h, d]) for (h, d) in chains}
        upd = {(h, d): _dot_tn(kd_s[h, d, rows[d], :], v_new[h, d]) for (h, d) in chains}
        for (h, d) in chains:
            lanes = slice(h * LANE, (h + 1) * LANE)
            if d == 0:
                o_ref[rows[d], lanes] = o[h, d]
            else:
                ob_s[rows[d], lanes] = o[h, d]
            st_ref[h, d] = dl_s[h, d, cc[d]][0:1, :] * s[h, d] + upd[h, d]
        return carry

    lax.fori_loop(0, n_chunks, phase_b, 0)
    o_ref[...] += ob_s[...]
    for h in range(GDN_HPS):
        for d in range(2):
            sf_ref[0, d, h] = st_ref[h, d]


def _gdn(pm, ps, ab, conv_w, s0, *, nseq, seq_len, row_block0, period):
    has_state = s0 is not None
    hps = GDN_HPS
    if s0 is None:
        s0 = jnp.zeros((1, 2, GDN_H, GDN_DK, GDN_DV), F32)
        s0_map = lambda b, p, ab: (0, 0, p, 0, 0)
    else:
        s0_map = lambda b, p, ab: (b, 0, p, 0, 0)
    n_chunks = seq_len // CHUNK
    kern = functools.partial(_gdn_kernel, n_chunks=n_chunks, period=period, has_state=has_state)
    wide = hps * LANE
    npair = GDN_H // hps
    grid_spec = pltpu.PrefetchScalarGridSpec(
        num_scalar_prefetch=1,
        grid=(nseq, npair),
        in_specs=[pl.BlockSpec((seq_len, wide), lambda b, p, ab: (row_block0 + b, COL_DQ // hps + p)),
                  pl.BlockSpec((seq_len, wide), lambda b, p, ab: (row_block0 + b, COL_DK // hps + p)),
                  pl.BlockSpec((seq_len, wide), lambda b, p, ab: (row_block0 + b, COL_DV // hps + p)),
                  pl.BlockSpec((seq_len, LANE), lambda b, p, ab: (row_block0 + b, 0)),
                  pl.BlockSpec((3, wide), lambda b, p, ab: (0, p)),
                  pl.BlockSpec((3, wide), lambda b, p, ab: (0, npair + p)),
                  pl.BlockSpec((3, wide), lambda b, p, ab: (0, 2 * npair + p)),
                  pl.BlockSpec((1, 2, hps, GDN_DK, GDN_DV), s0_map)],
        out_specs=(pl.BlockSpec((seq_len, wide), lambda b, p, ab: (b, p)),
                   pl.BlockSpec((1, 2, hps, GDN_DK, GDN_DV), lambda b, p, ab: (b, 0, p, 0, 0))),
        scratch_shapes=[pltpu.VMEM((hps, seq_len, LANE), F32),
                        pltpu.VMEM((hps, seq_len, LANE), F32),
                        pltpu.VMEM((hps, seq_len, LANE), F32),
                        pltpu.VMEM((hps, 2, seq_len, LANE), F32),
                        pltpu.VMEM((hps, 2, seq_len, LANE), F32),
                        pltpu.VMEM((hps, 2, seq_len, GDN_DV), F32),
                        pltpu.VMEM((hps, 2, 2 * seq_len, GDN_DK), BF16),
                        pltpu.VMEM((hps, 2, seq_len, CHUNK), BF16),
                        pltpu.VMEM((hps, 2, seq_len, GDN_DK), BF16),
                        pltpu.VMEM((hps, 2, n_chunks, 8, LANE), F32),
                        pltpu.VMEM((seq_len, wide), F32),
                        pltpu.VMEM((hps, 2, GDN_DK, GDN_DV), F32)],
    )
    return pl.pallas_call(
        kern,
        out_shape=(jax.ShapeDtypeStruct((nseq * seq_len, GDN_W), F32),
                   jax.ShapeDtypeStruct((nseq, 2, GDN_H, GDN_DK, GDN_DV), F32)),
        grid_spec=grid_spec,
        compiler_params=_params(("arbitrary", "arbitrary")),
        name="gdn",
    )(ab, pm, pm, pm, ps, conv_w, conv_w, conv_w, s0)


def _s5_kernel(u_ref, mi_ref, wst_ref, wout_ref, lam_ref, h0_ref, y_ref, fin_ref, e_ref, x_ref,
               *, n_chunks, nseq):
    half = 2 * S5_P
    u = u_ref[0]
    e_ref[...] = _dot(u, wst_ref[0])
    y_ref[0] = _dot(u, mi_ref[0])
    lam = lam_ref[0]
    l1f, l2f, l1b, l2b = lam[0:1, :], lam[1:2, :], lam[2:3, :], lam[3:4, :]
    h0 = h0_ref[0]

    def step(c, carry):
        xf, xb = carry
        rf = pl.ds(pl.multiple_of(c * nseq, nseq), nseq)
        rb = pl.ds(pl.multiple_of((n_chunks - 1 - c) * nseq, nseq), nseq)
        x_ref[rf, 0:half] = xf
        x_ref[rb, half:2 * half] = xb
        xf = xf * l1f + pltpu.roll(xf, S5_P, 1) * l2f + e_ref[rf, 0:half]
        xb = xb * l1b + pltpu.roll(xb, S5_P, 1) * l2b + e_ref[rb, half:2 * half]
        return xf, xb

    xf, xb = lax.fori_loop(0, n_chunks, step, (h0[:, 0:half], h0[:, half:2 * half]))
    y_ref[0] += _dot(x_ref[...].astype(BF16), wout_ref[0])
    fin_ref[0, :, 0:half] = xf
    fin_ref[0, :, half:2 * half] = xb


def _s5(u, m_intra, w_st, w_out, lam_t, h0, *, nseq, n_chunks):
    rows = n_chunks * nseq
    kern = functools.partial(_s5_kernel, n_chunks=n_chunks, nseq=nseq)
    wspec = pl.BlockSpec((1, 2 * LANE, 2 * LANE), lambda g: (g, 0, 0))
    return pl.pallas_call(
        kern,
        out_shape=(jax.ShapeDtypeStruct((S5_G, rows, 2 * LANE), F32),
                   jax.ShapeDtypeStruct((S5_G, nseq, 2 * LANE), F32)),
        grid=(S5_G,),
        in_specs=[pl.BlockSpec((1, rows, 2 * LANE), lambda g: (g, 0, 0)),
                  wspec, wspec, wspec,
                  pl.BlockSpec((1, 4, LANE), lambda g: (g, 0, 0)),
                  pl.BlockSpec((1, nseq, 2 * LANE), lambda g: (g, 0, 0))],
        out_specs=(pl.BlockSpec((1, rows, 2 * LANE), lambda g: (g, 0, 0)),
                   pl.BlockSpec((1, nseq, 2 * LANE), lambda g: (g, 0, 0))),
        scratch_shapes=[pltpu.VMEM((rows, 2 * LANE), F32), pltpu.VMEM((rows, 2 * LANE), F32)],
        compiler_params=_params(("arbitrary",)),
        name="s5",
    )(u, m_intra, w_st, w_out, lam_t, h0)


def _cmul(ar, ai, br, bi):
    return ar * br - ai * bi, ar * bi + ai * br


def _s5_weights(lam_re, lam_im, log_dt, b_re, b_im, c_re, c_im):
    t = S5_T
    dt = jnp.exp(log_dt)[..., None]
    tau = jnp.arange(t + 1, dtype=F32)[:, None, None, None]
    mag = jnp.exp(tau * (lam_re * dt)[None])
    ang = tau * (lam_im * dt)[None]
    pw_re, pw_im = mag * jnp.cos(ang), mag * jnp.sin(ang)
    nr, ni = pw_re[1] - 1.0, pw_im[1]
    den = lam_re * lam_re + lam_im * lam_im
    fr, fi = (nr * lam_re + ni * lam_im) / den, (ni * lam_re - nr * lam_im) / den
    bb_re, bb_im = _cmul(fr[..., None], fi[..., None], b_re[None], b_im[None])

    def kern_tau(d):
        cl_re, cl_im = _cmul(c_re[d][None], c_im[d][None], pw_re[:t, d, :, None, :], pw_im[:t, d, :, None, :])
        return (jnp.einsum('tgop,gpi->tgoi', cl_re, bb_re[d], precision=HI)
                - jnp.einsum('tgop,gpi->tgoi', cl_im, bb_im[d], precision=HI))

    kf, kb = kern_tau(0), kern_tau(1)
    lag = np.arange(t)[None, :] - np.arange(t)[:, None]
    sel_f = jnp.asarray(lag[..., None] == np.arange(t), F32)
    sel_b = jnp.asarray(-lag[..., None] == np.arange(t), F32)
    a = (jnp.einsum('stk,kgoi->gsito', sel_f, kf, precision=HI)
         + jnp.einsum('stk,kgoi->gsito', sel_b, kb, precision=HI))
    m_intra = a.reshape(S5_G, t * S5_GH, t * S5_GH)

    def st(d, pw_idx):
        r, i = _cmul(pw_re[pw_idx, d][..., None], pw_im[pw_idx, d][..., None], bb_re[d][None], bb_im[d][None])
        return r, i
    fre, fim = st(0, jnp.arange(t - 1, -1, -1))
    bre, bim = st(1, jnp.arange(t))
    w_st = jnp.concatenate([fre, fim, bre, bim], axis=2)
    w_st = w_st.transpose(1, 0, 3, 2).reshape(S5_G, t * S5_GH, 4 * S5_P)

    def ro(d, pw_idx):
        r, i = _cmul(c_re[d][None], c_im[d][None], pw_re[pw_idx, d][:, :, None, :], pw_im[pw_idx, d][:, :, None, :])
        return r, -i
    fr_, fi_ = ro(0, jnp.arange(1, t + 1))
    br_, bi_ = ro(1, jnp.arange(t, 0, -1))
    w_out = jnp.concatenate([fr_, fi_, br_, bi_], axis=3)
    w_out = w_out.transpose(1, 3, 0, 2).reshape(S5_G, 4 * S5_P, t * S5_GH)

    lt_re, lt_im = pw_re[t], pw_im[t]
    lam_t = jnp.stack([jnp.concatenate([lt_re[0], lt_re[0]], -1), jnp.concatenate([-lt_im[0], lt_im[0]], -1),
                       jnp.concatenate([lt_re[1], lt_re[1]], -1), jnp.concatenate([-lt_im[1], lt_im[1]], -1)],
                      axis=1)
    return m_intra.astype(BF16), w_st.astype(BF16), w_out.astype(BF16), lam_t


def _gelu_tanh(x):
    return 0.5 * x * (1.0 + jnp.tanh(math.sqrt(2.0 / math.pi) * (x + 0.044715 * (x * x * x))))


def _out_kernel(xc_ref, xs_ref, ada_ref, ogc_ref, ogs_ref, gg_ref, ysc_ref, yss_ref, su_ref, sg_ref,
                odc_ref, ods_ref, dg_ref, gn_ref, dn_ref, sd_ref, gw_ref, gb_ref, wo_ref, np_ref,
                oc_ref, os_ref, *, ctx_tiles):
    is_ctx = pl.program_id(0) < ctx_tiles

    def pick(c_ref, s_ref):
        return jnp.where(is_ctx, c_ref[...], s_ref[...])

    def head_norm(o, g):
        parts = []
        for h in range(o.shape[1] // LANE):
            oh = o[:, h * LANE:(h + 1) * LANE]
            parts.append(oh * lax.rsqrt(jnp.mean(oh * oh, axis=-1, keepdims=True) + EPS) * g)
        return jnp.concatenate(parts, axis=1)

    o_gla = head_norm(pick(ogc_ref, ogs_ref), gn_ref[...]) * _silu(gg_ref[...])
    y = _gelu_tanh(pick(ysc_ref, yss_ref) + sd_ref[...] * su_ref[...])
    y = y * jax.nn.sigmoid(_dot(y.astype(BF16), gw_ref[...]) + gb_ref[...])
    o_s5 = y * _silu(sg_ref[...])
    o_gdn = head_norm(pick(odc_ref, ods_ref), dn_ref[...]) * _silu(dg_ref[...])
    out = (_dot(o_gla.astype(BF16), wo_ref[0:GLA_W, :])
           + _dot(o_s5.astype(BF16), wo_ref[GLA_W:GLA_W + S5_W, :])
           + _dot(o_gdn.astype(BF16), wo_ref[GLA_W + S5_W:MIX_W, :]))
    r = out * lax.rsqrt(jnp.mean(out * out, axis=-1, keepdims=True) + EPS) * np_ref[...]
    gate = ada_ref[0][:, 2 * D_MODEL:3 * D_MODEL]
    x_new = pick(xc_ref, xs_ref) + gate * r

    @pl.when(is_ctx)
    def _():
        oc_ref[...] = x_new

    @pl.when(jnp.logical_not(is_ctx))
    def _():
        os_ref[...] = x_new


def _out(x_c, x_s, ada_l, pm, og_c, og_s, y_c, y_s, od_c, od_s, gla_norm, gdn_norm, s5_d, glu_w, glu_b, w_out,
         norm_post, ctx_tiles, tiles_per_latent):
    nt = x_c.shape[0] + x_s.shape[0]
    row = functools.partial(_ada_row, ctx_tiles=ctx_tiles, tiles_per_latent=tiles_per_latent)
    wide = 4 * LANE

    def tok(col_block):
        return pl.BlockSpec((TOKEN_TILE, wide), lambda i: (i, col_block))

    def ctx(width):
        return pl.BlockSpec((TOKEN_TILE, width), lambda i: (_ctx_tile(i, ctx_tiles), 0))

    def lat(width):
        return pl.BlockSpec((TOKEN_TILE, width), lambda i: (_lat_tile(i, ctx_tiles), 0))

    def full(shape):
        return pl.BlockSpec(shape, lambda i: (0,) * len(shape))

    return pl.pallas_call(
        functools.partial(_out_kernel, ctx_tiles=ctx_tiles),
        out_shape=(jax.ShapeDtypeStruct(x_c.shape, F32), jax.ShapeDtypeStruct(x_s.shape, F32)),
        grid=(nt // TOKEN_TILE,),
        in_specs=[ctx(D_MODEL), lat(D_MODEL),
                  pl.BlockSpec((1, 1, 3 * D_MODEL), lambda i: (row(i), 0, 0)),
                  ctx(wide), lat(wide), tok(COL_GGATE // 4),
                  ctx(wide), lat(wide), tok(COL_SU // 4), tok(COL_SGATE // 4),
                  ctx(wide), lat(wide), tok(COL_DGATE // 4),
                  full((1, LANE)), full((1, LANE)), full((1, S5_W)), full((S5_W, S5_W)), full((1, S5_W)),
                  full((MIX_W, D_MODEL)), full((1, D_MODEL))],
        out_specs=(ctx(D_MODEL), lat(D_MODEL)),
        compiler_params=_params(("arbitrary",)),
        name="outproj",
    )(x_c, x_s, ada_l, og_c, og_s, pm, y_c, y_s, pm, pm, od_c, od_s, pm, gla_norm, gdn_norm, s5_d, glu_w, glu_b,
      w_out, norm_post)


def _s5_to_groups(su, nseq, latent):
    if latent:
        x = su.reshape(nseq, -1, GRID_W, S5_G, S5_GH).transpose(3, 2, 0, 1, 4)
    else:
        x = su.reshape(nseq, -1, S5_T, S5_G, S5_GH).transpose(3, 1, 0, 2, 4)
    return x.reshape(S5_G, -1, S5_T * S5_GH)


def _s5_from_groups(y, nseq, latent):
    if latent:
        x = y.reshape(S5_G, GRID_W, nseq, -1, S5_GH).transpose(2, 3, 1, 0, 4)
    else:
        x = y.reshape(S5_G, -1, nseq, S5_T, S5_GH).transpose(2, 1, 3, 0, 4)
    return x.reshape(-1, S5_W)


def kernel(x_prompt, x_sample, c, state_gla, state_s5_re, state_s5_im, state_gdn, c_ctx, norm_pre, norm_post, w_ada, b_ada, w_in, gla_gate_w, gla_gate_b, gla_norm, s5_lam_re, s5_lam_im, s5_log_dt, s5_b_re, s5_b_im, s5_c_re, s5_c_im, s5_d, s5_glu_w, s5_glu_b, gdn_conv, gdn_a_log, gdn_dt_bias, gdn_norm, w_out):
    bp, lp, _ = x_prompt.shape
    bs, ls, _ = x_sample.shape
    n_ctx = bp * lp
    assert lp % TOKEN_TILE == 0 and ls % TOKEN_TILE == 0 and n_ctx % ls == 0
    assert ls // GRID_W == S5_T and lp % S5_T == 0
    ctx_tiles = n_ctx // TOKEN_TILE
    tiles_per_latent = ls // TOKEN_TILE

    cond = jnp.concatenate([c_ctx[None].astype(F32), c.astype(F32)], axis=0)
    rows = -(-cond.shape[0] // 8) * 8
    cond = jnp.pad(cond, ((0, rows - cond.shape[0]), (0, 0)))
    ada = _adaln(cond, w_ada.astype(F32), b_ada.astype(F32)).reshape(DEPTH, rows, 1, 3 * D_MODEL)

    w_main, w_small = _wprep(w_in.astype(F32))
    w_out_bf = w_out.astype(BF16)
    glu_w_bf = s5_glu_w.astype(BF16)
    gw_pad = jnp.zeros((DEPTH, 2, LANE, GLA_QK), F32)
    for d in range(2):
        gw_pad = gw_pad.at[:, d, SM_GLR + d * GLA_LR:SM_GLR + (d + 1) * GLA_LR, :].set(gla_gate_w[:, d].astype(F32))
    gdn_ab = jnp.concatenate([gdn_a_log, gdn_dt_bias], axis=-1).astype(F32)
    s5_w = jax.vmap(_s5_weights)(*(p.astype(F32) for p in (s5_lam_re, s5_lam_im, s5_log_dt, s5_b_re, s5_b_im,
                                                            s5_c_re, s5_c_im)))

    x_c = x_prompt.reshape(n_ctx, D_MODEL).astype(F32)
    x_s = x_sample.reshape(bs * ls, D_MODEL).astype(F32)
    gla_states, s5_states, gdn_states = [], [], []
    for l in range(DEPTH):
        pm, ps = _inproj(x_c, x_s, ada[l], norm_pre[l][None].astype(F32), w_main[l], w_small[l],
                         ctx_tiles, tiles_per_latent)
        gb = gla_gate_b[l].astype(F32).reshape(2, 1, GLA_QK)
        og_c, sg = _gla(pm, ps, gw_pad[l], gb, None, nseq=bp, seq_len=lp, row_block0=0)
        og_s, _ = _gla(pm, ps, gw_pad[l], gb, state_gla[:, l].astype(F32), nseq=bs, seq_len=ls,
                       row_block0=n_ctx // ls)
        conv_w = gdn_conv[l].astype(F32)
        od_c, sd = _gdn(pm, ps, gdn_ab[l], conv_w, None, nseq=bp, seq_len=lp, row_block0=0, period=lp)
        od_s, _ = _gdn(pm, ps, gdn_ab[l], conv_w, state_gdn[:, l].astype(F32), nseq=bs, seq_len=ls,
                       row_block0=n_ctx // ls, period=GRID_W)

        m_intra, w_st, w_ro, lam_t = s5_w[0][l], s5_w[1][l], s5_w[2][l], s5_w[3][l]
        su = pm[:, COL_SU * LANE:COL_SU * LANE + S5_W].astype(BF16)
        u_c = _s5_to_groups(su[:n_ctx], bp, False)
        u_s = _s5_to_groups(su[n_ctx:], bs, True)
        h0_c = jnp.zeros((S5_G, bp, 4 * S5_P), F32)
        sre, sim = state_s5_re[:, l].astype(F32), state_s5_im[:, l].astype(F32)
        h0_s = jnp.concatenate([sre[:, 0], sim[:, 0], sre[:, 1], sim[:, 1]], axis=-1).transpose(1, 0, 2)
        y_c, fin = _s5(u_c, m_intra, w_st, w_ro, lam_t, h0_c, nseq=bp, n_chunks=lp // S5_T)
        y_s, _ = _s5(u_s, m_intra, w_st, w_ro, lam_t, h0_s, nseq=bs, n_chunks=GRID_W)

        x_c, x_s = _out(x_c, x_s, ada[l], pm, og_c, og_s, _s5_from_groups(y_c, bp, False),
                        _s5_from_groups(y_s, bs, True), od_c, od_s,
                        gla_norm[l][None].astype(F32), gdn_norm[l][None].astype(F32), s5_d[l][None].astype(F32),
                        glu_w_bf[l], s5_glu_b[l][None].astype(F32), w_out_bf[l], norm_post[l][None].astype(F32),
                        ctx_tiles, tiles_per_latent)
        gla_states.append(sg)
        gdn_states.append(sd)
        fin = fin.transpose(1, 0, 2).reshape(bp, S5_G, 2, 2, S5_P)
        s5_states.append(fin.transpose(0, 2, 3, 1, 4))

    dt = x_prompt.dtype
    s5_all = jnp.stack(s5_states, axis=1)
    y_prompt = x_c.reshape(bp, lp, D_MODEL).astype(dt)
    y_sample = x_s.reshape(bs, ls, D_MODEL).astype(x_sample.dtype)
    return (y_prompt, y_sample, jnp.stack(gla_states, axis=1).astype(dt),
            s5_all[:, :, :, 0].astype(dt), s5_all[:, :, :, 1].astype(dt),
            jnp.stack(gdn_states, axis=1).astype(dt))
```

```python
import functools
import math

import jax
import jax.numpy as jnp
from jax import lax
from jax.experimental import pallas as pl
from jax.experimental.pallas import tpu as pltpu

F32 = jnp.float32
BF16 = jnp.bfloat16
HI = lax.Precision.HIGHEST

D_MODEL = 1024
DEPTH = 4
GRID_W = 64
CHUNK = 64
EPS = 1e-6
GLA_H, GLA_DK, GLA_DV, GLA_LR, GLA_TAU = 4, 64, 128, 16, 16.0
GLA_QK, GLA_W = GLA_H * GLA_DK, GLA_H * GLA_DV
S5_GH, S5_W, S5_P = 16, 512, 64
S5_G = S5_W // S5_GH
S5_T = 16
GDN_H, GDN_DK, GDN_DV = 4, 128, 128
GDN_W = GDN_H * GDN_DV
MIX_W = GLA_W + S5_W + GDN_W

LANE = 128
TOKEN_TILE = 256
VMEM_LIMIT = 48 * 1024 * 1024

COL_GQ, COL_GK, COL_GV, COL_GGATE, COL_SU, COL_SGATE = 0, 2, 4, 8, 12, 16
COL_DQ, COL_DK, COL_DV, COL_DGATE = 20, 24, 28, 32
MAIN_W = 36 * LANE
SM_GLR, SM_DA, SM_DB = 0, 32, 40


def _dot(a, b, precision=None):
    return lax.dot_general(a, b, (((1,), (0,)), ((), ())), precision=precision,
                           preferred_element_type=F32)


def _dot_nt(a, b, precision=None):
    return lax.dot_general(a, b, (((1,), (1,)), ((), ())), precision=precision,
                           preferred_element_type=F32)


def _dot_tn(a, b, precision=None):
    return lax.dot_general(a, b, (((0,), (0,)), ((), ())), precision=precision,
                           preferred_element_type=F32)


def _split_bf16(x):
    hi = x.astype(BF16)
    return hi, (x - hi.astype(F32)).astype(BF16)


def _dot3(a, b):
    a_hi, a_lo = _split_bf16(a)
    b_hi, b_lo = _split_bf16(b)
    return _dot(a_hi, b_hi) + (_dot(a_hi, b_lo) + _dot(a_lo, b_hi))


def _silu(x):
    return x * jax.nn.sigmoid(x)


def _softplus(x):
    return jnp.maximum(x, 0.0) + jnp.log1p(jnp.exp(-jnp.abs(x)))


def _params(sem):
    return pltpu.CompilerParams(dimension_semantics=sem, vmem_limit_bytes=VMEM_LIMIT)


def _adaln_kernel(cond_ref, w_ref, b_ref, o_ref):
    c = cond_ref[...]
    o_ref[0] = _dot(_silu(c), w_ref[0], precision=HI) + b_ref[0]


def _adaln(cond, w_ada, b_ada):
    rows = cond.shape[0]
    nj = 3 * D_MODEL // 1024
    return pl.pallas_call(
        _adaln_kernel,
        out_shape=jax.ShapeDtypeStruct((DEPTH, rows, 3 * D_MODEL), F32),
        grid=(DEPTH, nj),
        in_specs=[pl.BlockSpec((rows, D_MODEL), lambda l, j: (0, 0)),
                  pl.BlockSpec((1, D_MODEL, 1024), lambda l, j: (l, 0, j)),
                  pl.BlockSpec((1, 1, 1024), lambda l, j: (l, 0, j))],
        out_specs=pl.BlockSpec((1, rows, 1024), lambda l, j: (l, 0, j)),
        compiler_params=_params(("arbitrary", "arbitrary")),
        name="adaln",
    )(cond, w_ada, b_ada.reshape(DEPTH, 1, 3 * D_MODEL))


def _wprep_kernel(w_ref, m_ref, s_ref):
    x = w_ref[0]
    o = _W_IN_OFFS
    main = jnp.concatenate([x[:, o['gq'][0]:o['gv'][1]], x[:, o['ggate'][0]:o['dqkv'][1]],
                            x[:, o['dgate'][0]:o['dgate'][1]]], axis=1)
    small = jnp.concatenate([x[:, o['glr'][0]:o['glr'][1]], x[:, o['da'][0]:o['db'][1]],
                             jnp.zeros((x.shape[0], LANE - 2 * GLA_LR - 4 * GDN_H), x.dtype)], axis=1)
    m_ref[0] = main.astype(BF16)
    s_ref[0] = small.astype(BF16)


def _w_in_offsets():
    offs, pos = {}, 0
    for name, width in (('gq', GLA_QK), ('gk', GLA_QK), ('gv', GLA_W), ('glr', 2 * GLA_LR), ('ggate', GLA_W),
                        ('su', S5_W), ('sgate', S5_W), ('dqkv', 3 * GDN_W), ('da', 2 * GDN_H),
                        ('db', 2 * GDN_H), ('dgate', GDN_W)):
        offs[name] = (pos, pos + width)
        pos += width
    return offs


_W_IN_OFFS = _w_in_offsets()


def _wprep(w_in):
    depth, d_model, in_dim = w_in.shape
    rows = 256
    return pl.pallas_call(
        _wprep_kernel,
        out_shape=(jax.ShapeDtypeStruct((depth, d_model, MAIN_W), BF16),
                   jax.ShapeDtypeStruct((depth, d_model, LANE), BF16)),
        grid=(depth, d_model // rows),
        in_specs=[pl.BlockSpec((1, rows, in_dim), lambda l, i: (l, i, 0))],
        out_specs=(pl.BlockSpec((1, rows, MAIN_W), lambda l, i: (l, i, 0)),
                   pl.BlockSpec((1, rows, LANE), lambda l, i: (l, i, 0))),
        compiler_params=_params(("arbitrary", "arbitrary")),
        name="wprep",
    )(w_in)


def _inproj_kernel(xc_ref, xs_ref, ada_ref, g_ref, wm_ref, ws_ref, om_ref, os_ref, *, ctx_tiles):
    x = jnp.where(pl.program_id(0) < ctx_tiles, xc_ref[...], xs_ref[...])
    nrm = x * lax.rsqrt(jnp.mean(x * x, axis=-1, keepdims=True) + EPS) * g_ref[...]
    ada = ada_ref[0]
    shift = ada[:, 0:D_MODEL]
    scale = ada[:, D_MODEL:2 * D_MODEL]
    h = (nrm * (1.0 + scale) + shift).astype(BF16)
    om_ref[...] = _dot(h, wm_ref[...])
    os_ref[...] = _dot(h, ws_ref[...])


def _ada_row(i, ctx_tiles, tiles_per_latent):
    return jnp.where(i < ctx_tiles, 0, 1 + (i - ctx_tiles) // tiles_per_latent)


def _ctx_tile(i, ctx_tiles):
    return jnp.minimum(i, ctx_tiles - 1)


def _lat_tile(i, ctx_tiles):
    return jnp.maximum(i - ctx_tiles, 0)


def _inproj(x_c, x_s, ada_l, g_pre, w_main, w_small, ctx_tiles, tiles_per_latent):
    nt = x_c.shape[0] + x_s.shape[0]
    row = functools.partial(_ada_row, ctx_tiles=ctx_tiles, tiles_per_latent=tiles_per_latent)
    return pl.pallas_call(
        functools.partial(_inproj_kernel, ctx_tiles=ctx_tiles),
        out_shape=(jax.ShapeDtypeStruct((nt, MAIN_W), F32), jax.ShapeDtypeStruct((nt, LANE), F32)),
        grid=(nt // TOKEN_TILE,),
        in_specs=[pl.BlockSpec((TOKEN_TILE, D_MODEL), lambda i: (_ctx_tile(i, ctx_tiles), 0)),
                  pl.BlockSpec((TOKEN_TILE, D_MODEL), lambda i: (_lat_tile(i, ctx_tiles), 0)),
                  pl.BlockSpec((1, 1, 3 * D_MODEL), lambda i: (row(i), 0, 0)),
                  pl.BlockSpec((1, D_MODEL), lambda i: (0, 0)),
                  pl.BlockSpec((D_MODEL, MAIN_W), lambda i: (0, 0)),
                  pl.BlockSpec((D_MODEL, LANE), lambda i: (0, 0))],
        out_specs=(pl.BlockSpec((TOKEN_TILE, MAIN_W), lambda i: (i, 0)),
                   pl.BlockSpec((TOKEN_TILE, LANE), lambda i: (i, 0))),
        compiler_params=_params(("arbitrary",)),
        name="inproj",
    )(x_c, x_s, ada_l, g_pre, w_main, w_small)


def _tri_mask(n, reverse, strict=False):
    r = lax.broadcasted_iota(jnp.int32, (n, n), 0)
    c = lax.broadcasted_iota(jnp.int32, (n, n), 1)
    if reverse:
        return (r < c) if strict else (r <= c)
    return (r > c) if strict else (r >= c)


def _eye_mask(n):
    return lax.broadcasted_iota(jnp.int32, (n, n), 0) == lax.broadcasted_iota(jnp.int32, (n, n), 1)


GLA_CPI = 4


def _gla_kernel(q_ref, k_ref, v_ref, sm_ref, gw_ref, gb_ref, s0_ref, o_ref, sf_ref, g_s, ob_s, st_ref,
                *, n_chunks, has_state):
    for d in range(2):
        if has_state:
            s0 = jnp.concatenate([s0_ref[0, d, 0], s0_ref[0, d, 1]], axis=0)
            st_ref[d] = s0.T
        else:
            st_ref[d] = jnp.zeros((GLA_DV, LANE), F32)
        z = _dot3(sm_ref[...], gw_ref[d]) + gb_ref[d]
        g_s[d] = -_softplus(-z) * (1.0 / GLA_TAU)

    lane = lax.broadcasted_iota(jnp.int32, (CHUNK, LANE), 1)
    head_mask = [lane < GLA_DK, lane >= GLA_DK]
    scale = GLA_DK ** -0.5
    causal = [_tri_mask(CHUNK, False), _tri_mask(CHUNK, True)]
    causal_bf = [m.astype(BF16) for m in causal]

    def chunk_step(it, carry):
        jd = [(j, d) for j in range(GLA_CPI) for d in range(2)]
        jdh = [(j, d, h) for (j, d) in jd for h in range(2)]
        cc = {(j, d): (it * GLA_CPI + j) if d == 0 else n_chunks - 1 - (it * GLA_CPI + j) for (j, d) in jd}
        rows = {u: pl.ds(pl.multiple_of(cc[u] * CHUNK, CHUNK), CHUNK) for u in jd}
        q = {u: q_ref[rows[u], :] for u in jd}
        k = {u: k_ref[rows[u], :] for u in jd}
        v = {u: v_ref[rows[u], :] for u in jd}
        g = {(j, d): g_s[d, rows[j, d], :] for (j, d) in jd}
        g_hi = {u: g[u].astype(BF16) for u in jd}
        g_lo = {u: (g[u] - g_hi[u].astype(F32)).astype(BF16) for u in jd}
        b = {u: _dot(causal_bf[u[1]], g_hi[u]) + _dot(causal_bf[u[1]], g_lo[u]) for u in jd}
        b_last = {(j, d): b[j, d][CHUNK - 1:CHUNK, :] if d == 0 else b[j, d][0:1, :] for (j, d) in jd}
        qe = {u: q[u] * jnp.exp(b[u]) * scale for u in jd}
        ke = {u: (k[u] * jnp.exp(-b[u])).astype(BF16) for u in jd}
        kd = {u: k[u] * jnp.exp(b_last[u] - b[u]) for u in jd}
        qh = {(j, d, h): jnp.where(head_mask[h], qe[j, d], 0.0).astype(BF16) for (j, d, h) in jdh}
        vh = {(j, d, h): v[j, d][:, h * GLA_DV:(h + 1) * GLA_DV].astype(BF16) for (j, d, h) in jdh}
        sc = {(j, d, h): jnp.where(causal[d], _dot_nt(qh[j, d, h], ke[j, d]), 0.0).astype(BF16)
              for (j, d, h) in jdh}
        upd = {(j, d, h): _dot_tn(vh[j, d, h], jnp.where(head_mask[h], kd[j, d], 0.0).astype(BF16))
               for (j, d, h) in jdh}
        o_intra = {u: _dot(sc[u], vh[u]) for u in jdh}
        for j in range(GLA_CPI):
            st_bf = {d: st_ref[d].astype(BF16) for d in range(2)}
            for d in range(2):
                for h in range(2):
                    oh = o_intra[j, d, h] + _dot_nt(qh[j, d, h], st_bf[d])
                    if d == 0:
                        o_ref[rows[j, d], h * GLA_DV:(h + 1) * GLA_DV] = oh
                    else:
                        ob_s[rows[j, d], h * GLA_DV:(h + 1) * GLA_DV] = oh
                st_ref[d] = st_ref[d] * jnp.exp(b_last[j, d]) + upd[j, d, 0] + upd[j, d, 1]
        return carry

    lax.fori_loop(0, n_chunks // GLA_CPI, chunk_step, 0)
    o_ref[...] += ob_s[...]
    for d in range(2):
        s = st_ref[d].T
        for h in range(2):
            sf_ref[0, d, h] = s[h * GLA_DK:(h + 1) * GLA_DK, :]


def _gla(pm, ps, gw_pad, gb, s0, *, nseq, seq_len, row_block0):
    has_state = s0 is not None
    if s0 is None:
        s0 = jnp.zeros((1, 2, GLA_H, GLA_DK, GLA_DV), F32)
        s0_map = lambda b, p: (0, 0, p, 0, 0)
    else:
        s0_map = lambda b, p: (b, 0, p, 0, 0)
    kern = functools.partial(_gla_kernel, n_chunks=seq_len // CHUNK, has_state=has_state)
    return pl.pallas_call(
        kern,
        out_shape=(jax.ShapeDtypeStruct((nseq * seq_len, GLA_W), F32),
                   jax.ShapeDtypeStruct((nseq, 2, GLA_H, GLA_DK, GLA_DV), F32)),
        grid=(nseq, 2),
        in_specs=[pl.BlockSpec((seq_len, LANE), lambda b, p: (row_block0 + b, COL_GQ + p)),
                  pl.BlockSpec((seq_len, LANE), lambda b, p: (row_block0 + b, COL_GK + p)),
                  pl.BlockSpec((seq_len, 2 * LANE), lambda b, p: (row_block0 + b, COL_GV // 2 + p)),
                  pl.BlockSpec((seq_len, LANE), lambda b, p: (row_block0 + b, 0)),
                  pl.BlockSpec((2, LANE, LANE), lambda b, p: (0, 0, p)),
                  pl.BlockSpec((2, 1, LANE), lambda b, p: (0, 0, p)),
                  pl.BlockSpec((1, 2, 2, GLA_DK, GLA_DV), s0_map)],
        out_specs=(pl.BlockSpec((seq_len, 2 * LANE), lambda b, p: (b, p)),
                   pl.BlockSpec((1, 2, 2, GLA_DK, GLA_DV), lambda b, p: (b, 0, p, 0, 0))),
        scratch_shapes=[pltpu.VMEM((2, seq_len, LANE), F32), pltpu.VMEM((seq_len, 2 * LANE), F32),
                        pltpu.VMEM((2, GLA_DV, LANE), F32)],
        compiler_params=_params(("arbitrary", "arbitrary")),
        name="gla",
    )(pm, pm, pm, ps, gw_pad, gb, s0)


GDN_HPS = 4
GDN_CPI = 2


def _block_mask(n, s, reverse):
    r = lax.broadcasted_iota(jnp.int32, (n, n), 0)
    c = lax.broadcasted_iota(jnp.int32, (n, n), 1)
    if reverse:
        r, c = c, r
    sh = s.bit_length() - 1
    same_pair = (r >> (sh + 1)) == (c >> (sh + 1))
    return same_pair & (((r >> sh) & 1) == 1) & (((c >> sh) & 1) == 0)


def _gdn_kernel(ab_ref, q_ref, k_ref, v_ref, sm_ref, cwq_ref, cwk_ref, cwv_ref, s0_ref,
                o_ref, sf_ref, q_s, k_s, v_s, g_s, b_s, u_s, wq_s, a_s, kd_s, dl_s, ob_s, st_ref,
                *, n_chunks, period, has_state):
    hp = pl.program_id(1)
    seq_len = n_chunks * CHUNK
    row = lax.broadcasted_iota(jnp.int32, (seq_len, LANE), 0)
    first = (row % period) == 0
    last = (row % period) == period - 1

    def conv_silu(x, w):
        xp = jnp.where(first, 0.0, pltpu.roll(x, 1, 0))
        xn = jnp.where(last, 0.0, pltpu.roll(x, seq_len - 1, 0))
        return _silu(xp * w[0:1, :] + x * w[1:2, :] + xn * w[2:3, :])

    def l2norm(x):
        return x * lax.rsqrt(jnp.sum(x * x, axis=-1, keepdims=True) + EPS)

    lane1 = lax.broadcasted_iota(jnp.int32, (1, LANE), 1)
    a_log = jnp.zeros((1, LANE), F32)
    dt_bias = jnp.zeros((1, LANE), F32)
    for d in range(2):
        for hh in range(GDN_H):
            a_log = jnp.where(lane1 == SM_DA + d * GDN_H + hh, ab_ref[d, hh], a_log)
            dt_bias = jnp.where(lane1 == SM_DA + d * GDN_H + hh, ab_ref[d, GDN_H + hh], dt_bias)
    sm = sm_ref[...]
    g_all = -jnp.exp(a_log) * _softplus(sm + dt_bias)
    b_all = jax.nn.sigmoid(sm)
    lane = lax.broadcasted_iota(jnp.int32, (seq_len, LANE), 1)

    def lane_bcast(x, j):
        col = jnp.sum(jnp.where(lane == j, x, 0.0), axis=1, keepdims=True)
        return jnp.broadcast_to(col, (seq_len, LANE))

    for h in range(GDN_HPS):
        hd = hp * GDN_HPS + h
        lanes = slice(h * LANE, (h + 1) * LANE)
        q_s[h] = l2norm(conv_silu(q_ref[:, lanes], cwq_ref[:, lanes])) * GDN_DK ** -0.5
        k_s[h] = l2norm(conv_silu(k_ref[:, lanes], cwk_ref[:, lanes]))
        v_s[h] = conv_silu(v_ref[:, lanes], cwv_ref[:, lanes])
        for d in range(2):
            g_s[h, d] = lane_bcast(g_all, SM_DA + d * GDN_H + hd)
            b_s[h, d] = lane_bcast(b_all, SM_DB + d * GDN_H + hd)
            if has_state:
                st_ref[h, d] = s0_ref[0, d, h]
            else:
                st_ref[h, d] = jnp.zeros((GDN_DK, GDN_DV), F32)

    ones = jnp.ones((CHUNK, CHUNK), F32)
    causal = [_tri_mask(CHUNK, False), _tri_mask(CHUNK, True)]
    causal_f = [m.astype(F32) for m in causal]
    strict_f = [_tri_mask(CHUNK, d == 1, strict=True).astype(F32) for d in range(2)]
    level_masks = [[_block_mask(CHUNK, 1 << j, d == 1).astype(F32) for j in range(CHUNK.bit_length() - 1)]
                   for d in range(2)]

    causal_bf = [m.astype(BF16) for m in causal]
    ones_bf = ones.astype(BF16)
    eye_f = _eye_mask(CHUNK).astype(F32)

    def phase_a(it, carry):
        cs = [it * GDN_CPI + j for j in range(GDN_CPI)]
        rows = [pl.ds(pl.multiple_of(c * CHUNK, CHUNK), CHUNK) for c in cs]
        rows2 = [pl.ds(pl.multiple_of(c * 2 * CHUNK, 2 * CHUNK), CHUNK) for c in cs]
        rows2b = [pl.ds(pl.multiple_of(c * 2 * CHUNK + CHUNK, CHUNK), CHUNK) for c in cs]
        pairs = [(j, h) for j in range(GDN_CPI) for h in range(GDN_HPS)]
        units = [(j, h, d) for (j, h) in pairs for d in range(2)]
        q = {(j, h): q_s[h, rows[j], :] for (j, h) in pairs}
        k = {(j, h): k_s[h, rows[j], :] for (j, h) in pairs}
        v = {(j, h): v_s[h, rows[j], :] for (j, h) in pairs}
        k_bf = {p: k[p].astype(BF16) for p in pairs}
        g = {(j, h, d): g_s[h, d, rows[j], :] for (j, h, d) in units}
        beta = {(j, h, d): b_s[h, d, rows[j], :] for (j, h, d) in units}
        g_hi = {u: g[u].astype(BF16) for u in units}
        g_lo = {u: (g[u] - g_hi[u].astype(F32)).astype(BF16) for u in units}
        gm = {u: g[u][:, :CHUNK] * causal_f[1 - u[2]] for u in units}
        gm_hi = {u: gm[u].astype(BF16) for u in units}
        gm_lo = {u: (gm[u] - gm_hi[u].astype(F32)).astype(BF16) for u in units}
        gc = {u: _dot(causal_bf[u[2]], g_hi[u]) + _dot(causal_bf[u[2]], g_lo[u]) for u in units}
        gc_t = {u: _dot(ones_bf, gm_hi[u]) + _dot(ones_bf, gm_lo[u]) for u in units}
        qk = {p: _dot_nt(q[p].astype(BF16), k_bf[p]) for p in pairs}
        kk = {p: _dot_nt(k_bf[p], k_bf[p]) for p in pairs}
        decay, m, t = {}, {}, {}
        for u in units:
            j, h, d = u
            diff = gc[u][:, :CHUNK] - gc_t[u]
            decay[u] = jnp.where(causal[d], jnp.exp(jnp.where(causal[d], diff, 0.0)), 0.0)
            m[u] = kk[j, h] * beta[u][:, :CHUNK] * decay[u] * strict_f[d]
            t[u] = eye_f - m[u] * level_masks[d][0]
        for lvl in range(1, len(level_masks[0])):
            t_bf = {u: t[u].astype(BF16) for u in units}
            p1 = {u: _dot(t_bf[u], (m[u] * level_masks[u[2]][lvl]).astype(BF16)).astype(BF16) for u in units}
            t = {u: t[u] - _dot(p1[u], t_bf[u]) for u in units}
        egc = {u: jnp.exp(gc[u]) for u in units}
        rhs = {(j, h, d): jnp.concatenate([v[j, h] * beta[j, h, d], k[j, h] * (beta[j, h, d] * egc[j, h, d])],
                                          axis=1).astype(BF16) for (j, h, d) in units}
        uw = {u: _dot(t[u].astype(BF16), rhs[u]) for u in units}
        for u in units:
            j, h, d = u
            gc_last = gc[u][CHUNK - 1:CHUNK, :] if d == 0 else gc[u][0:1, :]
            u_s[h, d, rows[j], :] = uw[u][:, 0:GDN_DV]
            wq_s[h, d, rows2[j], :] = uw[u][:, GDN_DV:GDN_DV + GDN_DK].astype(BF16)
            wq_s[h, d, rows2b[j], :] = (q[j, h] * egc[u]).astype(BF16)
            a_s[h, d, rows[j], :] = (qk[j, h] * decay[u]).astype(BF16)
            kd_s[h, d, rows[j], :] = (k[j, h] * jnp.exp(gc_last - gc[u])).astype(BF16)
            dl_s[h, d, cs[j]] = jnp.broadcast_to(jnp.exp(gc_last), (8, LANE))
        return carry

    lax.fori_loop(0, n_chunks // GDN_CPI, phase_a, 0)

    def phase_b(i, carry):
        chains = [(h, d) for h in range(GDN_HPS) for d in range(2)]
        cc = {0: i, 1: n_chunks - 1 - i}
        rows = {d: pl.ds(pl.multiple_of(cc[d] * CHUNK, CHUNK), CHUNK) for d in range(2)}
        rows2 = {d: pl.ds(pl.multiple_of(cc[d] * 2 * CHUNK, 2 * CHUNK), 2 * CHUNK) for d in range(2)}
        s = {hd: st_ref[hd[0], hd[1]] for hd in chains}
        ws = {(h, d): _dot(wq_s[h, d, rows2[d], :], s[h, d].astype(BF16)) for (h, d) in chains}
        v_new = {(h, d): (u_s[h, d, rows[d], :] - ws[h, d][0:CHUNK, :]).astype(BF16) for (h, d) in chains}
        o = {(h, d): ws[h, d][CHUNK:2 * CHUNK, :] + _dot(a_s[h, d, rows[d], :], v_new[h, d]) for (h, d) in chains}
        upd = {(h, d): _dot_tn(kd_s[h, d, rows[d], :], v_new[h, d]) for (h, d) in chains}
        for (h, d) in chains:
            lanes = slice(h * LANE, (h + 1) * LANE)
            if d == 0:
                o_ref[rows[d], lanes] = o[h, d]
            else:
                ob_s[rows[d], lanes] = o[h, d]
            st_ref[h, d] = dl_s[h, d, cc[d]][0:1, :] * s[h, d] + upd[h, d]
        return carry

    lax.fori_loop(0, n_chunks, phase_b, 0)
    o_ref[...] += ob_s[...]
    for h in range(GDN_HPS):
        for d in range(2):
            sf_ref[0, d, h] = st_ref[h, d]


def _gdn(pm, ps, ab, conv_w, s0, *, nseq, seq_len, row_block0, period):
    has_state = s0 is not None
    hps = GDN_HPS
    if s0 is None:
        s0 = jnp.zeros((1, 2, GDN_H, GDN_DK, GDN_DV), F32)
        s0_map = lambda b, p, ab: (0, 0, p, 0, 0)
    else:
        s0_map = lambda b, p, ab: (b, 0, p, 0, 0)
    n_chunks = seq_len // CHUNK
    kern = functools.partial(_gdn_kernel, n_chunks=n_chunks, period=period, has_state=has_state)
    wide = hps * LANE
    npair = GDN_H // hps
    grid_spec = pltpu.PrefetchScalarGridSpec(
        num_scalar_prefetch=1,
        grid=(nseq, npair),
        in_specs=[pl.BlockSpec((seq_len, wide), lambda b, p, ab: (row_block0 + b, COL_DQ // hps + p),
                               pipeline_mode=pl.Buffered(1)),
                  pl.BlockSpec((seq_len, wide), lambda b, p, ab: (row_block0 + b, COL_DK // hps + p),
                               pipeline_mode=pl.Buffered(1)),
                  pl.BlockSpec((seq_len, wide), lambda b, p, ab: (row_block0 + b, COL_DV // hps + p),
                               pipeline_mode=pl.Buffered(1)),
                  pl.BlockSpec((seq_len, LANE), lambda b, p, ab: (row_block0 + b, 0)),
                  pl.BlockSpec((3, wide), lambda b, p, ab: (0, p)),
                  pl.BlockSpec((3, wide), lambda b, p, ab: (0, npair + p)),
                  pl.BlockSpec((3, wide), lambda b, p, ab: (0, 2 * npair + p)),
                  pl.BlockSpec((1, 2, hps, GDN_DK, GDN_DV), s0_map)],
        out_specs=(pl.BlockSpec((seq_len, wide), lambda b, p, ab: (b, p)),
                   pl.BlockSpec((1, 2, hps, GDN_DK, GDN_DV), lambda b, p, ab: (b, 0, p, 0, 0))),
        scratch_shapes=[pltpu.VMEM((hps, seq_len, LANE), F32),
                        pltpu.VMEM((hps, seq_len, LANE), F32),
                        pltpu.VMEM((hps, seq_len, LANE), F32),
                        pltpu.VMEM((hps, 2, seq_len, LANE), F32),
                        pltpu.VMEM((hps, 2, seq_len, LANE), F32),
                        pltpu.VMEM((hps, 2, seq_len, GDN_DV), F32),
                        pltpu.VMEM((hps, 2, 2 * seq_len, GDN_DK), BF16),
                        pltpu.VMEM((hps, 2, seq_len, CHUNK), BF16),
                        pltpu.VMEM((hps, 2, seq_len, GDN_DK), BF16),
                        pltpu.VMEM((hps, 2, n_chunks, 8, LANE), F32),
                        pltpu.VMEM((seq_len, wide), F32),
                        pltpu.VMEM((hps, 2, GDN_DK, GDN_DV), F32)],
    )
    return pl.pallas_call(
        kern,
        out_shape=(jax.ShapeDtypeStruct((nseq * seq_len, GDN_W), F32),
                   jax.ShapeDtypeStruct((nseq, 2, GDN_H, GDN_DK, GDN_DV), F32)),
        grid_spec=grid_spec,
        compiler_params=_params(("arbitrary", "arbitrary")),
        name="gdn",
    )(ab, pm, pm, pm, ps, conv_w, conv_w, conv_w, s0)


S5_GPB = LANE // S5_GH
S5_XROWS = 4096


def _s5_kernel(*refs, n_chunks, nseq, latent):
    n_x = len(refs) - 10
    x_refs = refs[:n_x]
    mi_ref, wst_ref, wout_ref, lam_ref, h0_ref, y_ref, fin_ref, u_s, e_s, y_s = refs[n_x:]
    seq_len = n_chunks * S5_T
    seq_per_x = S5_XROWS // seq_len
    lane_grp = lax.broadcasted_iota(jnp.int32, (n_chunks, LANE), 1) >> 4

    def slab_rows(base, t):
        if latent:
            return pl.ds(pl.multiple_of(base + t * GRID_W, GRID_W), n_chunks)
        return pl.ds(base + t, n_chunks, stride=S5_T)

    def relayout_in(x_ref, b0):
        def body(bl, carry):
            base = bl * seq_len
            b = b0 + bl
            for lt in range(2):
                slabs = [x_ref[slab_rows(base, S5_GPB * lt + tt), :] for tt in range(S5_GPB)]
                for gi in range(S5_GPB):
                    acc = slabs[gi]
                    for tt in range(S5_GPB):
                        if tt != gi:
                            acc = jnp.where(lane_grp == tt, pltpu.roll(slabs[tt], ((tt - gi) % S5_GPB) * S5_GH, 1), acc)
                    u_s[gi, lt, pl.ds(b, n_chunks, stride=nseq), :] = acc
            return carry
        lax.fori_loop(0, seq_per_x, body, 0)

    for i, x_ref in enumerate(x_refs):
        relayout_in(x_ref, i * seq_per_x)

    half = 2 * S5_P
    u_bf = [jnp.concatenate([u_s[gi, 0], u_s[gi, 1]], axis=1).astype(BF16) for gi in range(S5_GPB)]
    for gi in range(S5_GPB):
        e = _dot(u_bf[gi], wst_ref[gi])
        e_s[gi, 0] = e[:, 0:half]
        e_s[gi, 1] = e[:, half:2 * half]
    for gi in range(S5_GPB):
        y = _dot(u_bf[gi], mi_ref[gi])
        y_s[gi, 0] = y[:, 0:LANE]
        y_s[gi, 1] = y[:, LANE:2 * LANE]

    lam = [lam_ref[gi] for gi in range(S5_GPB)]

    def step(c, carry):
        rf = pl.ds(pl.multiple_of(c * nseq, nseq), nseq)
        rb = pl.ds(pl.multiple_of((n_chunks - 1 - c) * nseq, nseq), nseq)
        new = []
        for gi in range(S5_GPB):
            xf, xb = carry[2 * gi], carry[2 * gi + 1]
            u_s[gi, 0, rf, :] = xf
            u_s[gi, 1, rb, :] = xb
            l = lam[gi]
            new.append(xf * l[0:1, :] + pltpu.roll(xf, S5_P, 1) * l[1:2, :] + e_s[gi, 0, rf, :])
            new.append(xb * l[2:3, :] + pltpu.roll(xb, S5_P, 1) * l[3:4, :] + e_s[gi, 1, rb, :])
        return tuple(new)

    init = []
    for gi in range(S5_GPB):
        h0 = h0_ref[gi]
        init += [h0[:, 0:half], h0[:, half:2 * half]]
    fin = lax.fori_loop(0, n_chunks, step, tuple(init))
    for gi in range(S5_GPB):
        fin_ref[gi, :, 0:half] = fin[2 * gi]
        fin_ref[gi, :, half:2 * half] = fin[2 * gi + 1]
        x_in = jnp.concatenate([u_s[gi, 0], u_s[gi, 1]], axis=1).astype(BF16)
        y = _dot(x_in, wout_ref[gi])
        y_s[gi, 0] += y[:, 0:LANE]
        y_s[gi, 1] += y[:, LANE:2 * LANE]

    def relayout_out(b, carry):
        base = b * seq_len
        for lt in range(2):
            tiles = [y_s[gi, lt, pl.ds(b, n_chunks, stride=nseq), :] for gi in range(S5_GPB)]
            for tt in range(S5_GPB):
                acc = tiles[tt]
                for gi in range(S5_GPB):
                    if gi != tt:
                        acc = jnp.where(lane_grp == gi, pltpu.roll(tiles[gi], ((gi - tt) % S5_GPB) * S5_GH, 1), acc)
                y_ref[slab_rows(base, S5_GPB * lt + tt), :] = acc
        return carry

    lax.fori_loop(0, nseq, relayout_out, 0)


def _s5(pm, m_intra, w_st, w_out, lam_t, h0, *, nseq, n_chunks, row_block0, latent):
    rows = n_chunks * nseq
    n_tok = rows * S5_T
    n_x = n_tok // S5_XROWS
    kern = functools.partial(_s5_kernel, n_chunks=n_chunks, nseq=nseq, latent=latent)
    wspec = pl.BlockSpec((S5_GPB, 2 * LANE, 2 * LANE), lambda j: (j, 0, 0))
    x_specs = [pl.BlockSpec((S5_XROWS, LANE), functools.partial(lambda j, i: (row_block0 + i, COL_SU + j), i=i))
               for i in range(n_x)]
    plane = pltpu.VMEM((S5_GPB, 2, rows, LANE), F32)
    return pl.pallas_call(
        kern,
        out_shape=(jax.ShapeDtypeStruct((n_tok, S5_W), F32),
                   jax.ShapeDtypeStruct((S5_G, nseq, 2 * LANE), F32)),
        grid=(S5_G // S5_GPB,),
        in_specs=x_specs + [wspec, wspec, wspec,
                            pl.BlockSpec((S5_GPB, 4, LANE), lambda j: (j, 0, 0)),
                            pl.BlockSpec((S5_GPB, nseq, 2 * LANE), lambda j: (j, 0, 0))],
        out_specs=(pl.BlockSpec((n_tok, LANE), lambda j: (0, j)),
                   pl.BlockSpec((S5_GPB, nseq, 2 * LANE), lambda j: (j, 0, 0))),
        scratch_shapes=[plane, plane, plane],
        compiler_params=_params(("arbitrary",)),
        name="s5",
    )(*([pm] * n_x), m_intra, w_st, w_out, lam_t, h0)


def _cmul(ar, ai, br, bi):
    return ar * br - ai * bi, ar * bi + ai * br


def _s5_weights(lam_re, lam_im, log_dt, b_re, b_im, c_re, c_im):
    t = S5_T
    dt = jnp.exp(log_dt)[..., None]
    tau = jnp.arange(t + 1, dtype=F32)[:, None, None, None]
    mag = jnp.exp(tau * (lam_re * dt)[None])
    ang = tau * (lam_im * dt)[None]
    pw_re, pw_im = mag * jnp.cos(ang), mag * jnp.sin(ang)
    nr, ni = pw_re[1] - 1.0, pw_im[1]
    den = lam_re * lam_re + lam_im * lam_im
    fr, fi = (nr * lam_re + ni * lam_im) / den, (ni * lam_re - nr * lam_im) / den
    bb_re, bb_im = _cmul(fr[..., None], fi[..., None], b_re[None], b_im[None])

    bt_re, bt_im = jnp.swapaxes(bb_re, -1, -2), jnp.swapaxes(bb_im, -1, -2)
    ct_re, ct_im = jnp.swapaxes(c_re, -1, -2), jnp.swapaxes(c_im, -1, -2)
    pwt_re, pwt_im = jnp.moveaxis(pw_re, 0, -1), jnp.moveaxis(pw_im, 0, -1)
    cl_re, cl_im = _cmul(pwt_re[..., None], pwt_im[..., None], ct_re[:, :, :, None, :], ct_im[:, :, :, None, :])

    def lanes(x):
        return x.reshape(S5_G, S5_P, t * S5_GH)

    def k_rows(d, taus):
        return (jnp.einsum('gip,gpx->gix', bt_re[d], lanes(cl_re[d][:, :, taus]), precision=HI)
                - jnp.einsum('gip,gpx->gix', bt_im[d], lanes(cl_im[d][:, :, taus]), precision=HI))

    width = t * S5_GH
    pad = (t - 1) * S5_GH
    kf = jnp.pad(k_rows(0, slice(0, t)), ((0, 0), (0, 0), (pad, 0)))
    kb = jnp.pad(k_rows(1, slice(t - 1, None, -1)), ((0, 0), (0, 0), (0, pad)))
    m_intra = jnp.stack([kf[:, :, pad - s * S5_GH:pad - s * S5_GH + width]
                         + kb[:, :, (t - 1 - s) * S5_GH:(t - 1 - s) * S5_GH + width] for s in range(t)], axis=1)
    m_intra = m_intra.reshape(S5_G, width, width)

    def st(d, taus):
        return _cmul(jnp.moveaxis(pw_re[taus, d], 0, 1)[:, :, None, :], jnp.moveaxis(pw_im[taus, d], 0, 1)[:, :, None, :],
                     bt_re[d][:, None], bt_im[d][:, None])
    w_st = jnp.concatenate(st(0, slice(t - 1, None, -1)) + st(1, slice(0, t)), axis=-1)
    w_st = w_st.reshape(S5_G, width, 4 * S5_P)

    w_out = jnp.concatenate([lanes(cl_re[0][:, :, 1:t + 1]), -lanes(cl_im[0][:, :, 1:t + 1]),
                             lanes(cl_re[1][:, :, t:0:-1]), -lanes(cl_im[1][:, :, t:0:-1])], axis=1)


    lt_re, lt_im = pw_re[t], pw_im[t]
    lam_t = jnp.stack([jnp.concatenate([lt_re[0], lt_re[0]], -1), jnp.concatenate([-lt_im[0], lt_im[0]], -1),
                       jnp.concatenate([lt_re[1], lt_re[1]], -1), jnp.concatenate([-lt_im[1], lt_im[1]], -1)],
                      axis=1)
    return m_intra.astype(BF16), w_st.astype(BF16), w_out.astype(BF16), lam_t


def _gelu_tanh(x):
    return 0.5 * x * (1.0 + jnp.tanh(math.sqrt(2.0 / math.pi) * (x + 0.044715 * (x * x * x))))


def _out_kernel(xc_ref, xs_ref, ada_ref, ogc_ref, ogs_ref, gg_ref, ysc_ref, yss_ref, su_ref, sg_ref,
                odc_ref, ods_ref, dg_ref, gn_ref, dn_ref, sd_ref, gw_ref, gb_ref, wo_ref, np_ref,
                oc_ref, os_ref, *, ctx_tiles):
    is_ctx = pl.program_id(0) < ctx_tiles

    def pick(c_ref, s_ref):
        return jnp.where(is_ctx, c_ref[...], s_ref[...])

    def head_norm(o, g):
        parts = []
        for h in range(o.shape[1] // LANE):
            oh = o[:, h * LANE:(h + 1) * LANE]
            parts.append(oh * lax.rsqrt(jnp.mean(oh * oh, axis=-1, keepdims=True) + EPS) * g)
        return jnp.concatenate(parts, axis=1)

    o_gla = head_norm(pick(ogc_ref, ogs_ref), gn_ref[...]) * _silu(gg_ref[...])
    y = _gelu_tanh(pick(ysc_ref, yss_ref) + sd_ref[...] * su_ref[...])
    y = y * jax.nn.sigmoid(_dot(y.astype(BF16), gw_ref[...]) + gb_ref[...])
    o_s5 = y * _silu(sg_ref[...])
    o_gdn = head_norm(pick(odc_ref, ods_ref), dn_ref[...]) * _silu(dg_ref[...])
    out = (_dot(o_gla.astype(BF16), wo_ref[0:GLA_W, :])
           + _dot(o_s5.astype(BF16), wo_ref[GLA_W:GLA_W + S5_W, :])
           + _dot(o_gdn.astype(BF16), wo_ref[GLA_W + S5_W:MIX_W, :]))
    r = out * lax.rsqrt(jnp.mean(out * out, axis=-1, keepdims=True) + EPS) * np_ref[...]
    gate = ada_ref[0][:, 2 * D_MODEL:3 * D_MODEL]
    x_new = pick(xc_ref, xs_ref) + gate * r

    @pl.when(is_ctx)
    def _():
        oc_ref[...] = x_new

    @pl.when(jnp.logical_not(is_ctx))
    def _():
        os_ref[...] = x_new


def _out(x_c, x_s, ada_l, pm, og_c, og_s, y_c, y_s, od_c, od_s, gla_norm, gdn_norm, s5_d, glu_w, glu_b, w_out,
         norm_post, ctx_tiles, tiles_per_latent):
    nt = x_c.shape[0] + x_s.shape[0]
    row = functools.partial(_ada_row, ctx_tiles=ctx_tiles, tiles_per_latent=tiles_per_latent)
    wide = 4 * LANE

    def tok(col_block):
        return pl.BlockSpec((TOKEN_TILE, wide), lambda i: (i, col_block))

    def ctx(width):
        return pl.BlockSpec((TOKEN_TILE, width), lambda i: (_ctx_tile(i, ctx_tiles), 0))

    def lat(width):
        return pl.BlockSpec((TOKEN_TILE, width), lambda i: (_lat_tile(i, ctx_tiles), 0))

    def full(shape):
        return pl.BlockSpec(shape, lambda i: (0,) * len(shape))

    return pl.pallas_call(
        functools.partial(_out_kernel, ctx_tiles=ctx_tiles),
        out_shape=(jax.ShapeDtypeStruct(x_c.shape, F32), jax.ShapeDtypeStruct(x_s.shape, F32)),
        grid=(nt // TOKEN_TILE,),
        in_specs=[ctx(D_MODEL), lat(D_MODEL),
                  pl.BlockSpec((1, 1, 3 * D_MODEL), lambda i: (row(i), 0, 0)),
                  ctx(wide), lat(wide), tok(COL_GGATE // 4),
                  ctx(wide), lat(wide), tok(COL_SU // 4), tok(COL_SGATE // 4),
                  ctx(wide), lat(wide), tok(COL_DGATE // 4),
                  full((1, LANE)), full((1, LANE)), full((1, S5_W)), full((S5_W, S5_W)), full((1, S5_W)),
                  full((MIX_W, D_MODEL)), full((1, D_MODEL))],
        out_specs=(ctx(D_MODEL), lat(D_MODEL)),
        compiler_params=_params(("arbitrary",)),
        name="outproj",
    )(x_c, x_s, ada_l, og_c, og_s, pm, y_c, y_s, pm, pm, od_c, od_s, pm, gla_norm, gdn_norm, s5_d, glu_w, glu_b,
      w_out, norm_post)


def kernel(x_prompt, x_sample, c, state_gla, state_s5_re, state_s5_im, state_gdn, c_ctx, norm_pre, norm_post, w_ada, b_ada, w_in, gla_gate_w, gla_gate_b, gla_norm, s5_lam_re, s5_lam_im, s5_log_dt, s5_b_re, s5_b_im, s5_c_re, s5_c_im, s5_d, s5_glu_w, s5_glu_b, gdn_conv, gdn_a_log, gdn_dt_bias, gdn_norm, w_out):
    bp, lp, _ = x_prompt.shape
    bs, ls, _ = x_sample.shape
    n_ctx = bp * lp
    assert lp % TOKEN_TILE == 0 and ls % TOKEN_TILE == 0 and n_ctx % ls == 0
    assert ls // GRID_W == S5_T and lp % S5_T == 0 and n_ctx % S5_XROWS == 0 and (bs * ls) % S5_XROWS == 0
    ctx_tiles = n_ctx // TOKEN_TILE
    tiles_per_latent = ls // TOKEN_TILE

    cond = jnp.concatenate([c_ctx[None].astype(F32), c.astype(F32)], axis=0)
    rows = -(-cond.shape[0] // 8) * 8
    cond = jnp.pad(cond, ((0, rows - cond.shape[0]), (0, 0)))
    ada = _adaln(cond, w_ada.astype(F32), b_ada.astype(F32)).reshape(DEPTH, rows, 1, 3 * D_MODEL)

    w_main, w_small = _wprep(w_in.astype(F32))
    w_out_bf = w_out.astype(BF16)
    glu_w_bf = s5_glu_w.astype(BF16)
    gw_pad = jnp.zeros((DEPTH, 2, LANE, GLA_QK), F32)
    for d in range(2):
        gw_pad = gw_pad.at[:, d, SM_GLR + d * GLA_LR:SM_GLR + (d + 1) * GLA_LR, :].set(gla_gate_w[:, d].astype(F32))
    gdn_ab = jnp.concatenate([gdn_a_log, gdn_dt_bias], axis=-1).astype(F32)
    s5_w = jax.vmap(_s5_weights)(*(p.astype(F32) for p in (s5_lam_re, s5_lam_im, s5_log_dt, s5_b_re, s5_b_im,
                                                            s5_c_re, s5_c_im)))

    x_c = x_prompt.reshape(n_ctx, D_MODEL).astype(F32)
    x_s = x_sample.reshape(bs * ls, D_MODEL).astype(F32)
    gla_states, s5_states, gdn_states = [], [], []
    for l in range(DEPTH):
        pm, ps = _inproj(x_c, x_s, ada[l], norm_pre[l][None].astype(F32), w_main[l], w_small[l],
                         ctx_tiles, tiles_per_latent)
        gb = gla_gate_b[l].astype(F32).reshape(2, 1, GLA_QK)
        og_c, sg = _gla(pm, ps, gw_pad[l], gb, None, nseq=bp, seq_len=lp, row_block0=0)
        og_s, _ = _gla(pm, ps, gw_pad[l], gb, state_gla[:, l].astype(F32), nseq=bs, seq_len=ls,
                       row_block0=n_ctx // ls)
        conv_w = gdn_conv[l].astype(F32)
        od_c, sd = _gdn(pm, ps, gdn_ab[l], conv_w, None, nseq=bp, seq_len=lp, row_block0=0, period=lp)
        od_s, _ = _gdn(pm, ps, gdn_ab[l], conv_w, state_gdn[:, l].astype(F32), nseq=bs, seq_len=ls,
                       row_block0=n_ctx // ls, period=GRID_W)

        m_intra, w_st, w_ro, lam_t = s5_w[0][l], s5_w[1][l], s5_w[2][l], s5_w[3][l]
        h0_c = jnp.zeros((S5_G, bp, 4 * S5_P), F32)
        sre, sim = state_s5_re[:, l].astype(F32), state_s5_im[:, l].astype(F32)
        h0_s = jnp.concatenate([sre[:, 0], sim[:, 0], sre[:, 1], sim[:, 1]], axis=-1).transpose(1, 0, 2)
        y_c, fin = _s5(pm, m_intra, w_st, w_ro, lam_t, h0_c, nseq=bp, n_chunks=lp // S5_T, row_block0=0,
                       latent=False)
        y_s, _ = _s5(pm, m_intra, w_st, w_ro, lam_t, h0_s, nseq=bs, n_chunks=GRID_W,
                     row_block0=n_ctx // S5_XROWS, latent=True)

        x_c, x_s = _out(x_c, x_s, ada[l], pm, og_c, og_s, y_c, y_s, od_c, od_s,
                        gla_norm[l][None].astype(F32), gdn_norm[l][None].astype(F32), s5_d[l][None].astype(F32),
                        glu_w_bf[l], s5_glu_b[l][None].astype(F32), w_out_bf[l], norm_post[l][None].astype(F32),
                        ctx_tiles, tiles_per_latent)
        gla_states.append(sg)
        gdn_states.append(sd)
        fin = fin.transpose(1, 0, 2).reshape(bp, S5_G, 2, 2, S5_P)
        s5_states.append(fin.transpose(0, 2, 3, 1, 4))

    dt = x_prompt.dtype
    s5_all = jnp.stack(s5_states, axis=1)
    y_prompt = x_c.reshape(bp, lp, D_MODEL).astype(dt)
    y_sample = x_s.reshape(bs, ls, D_MODEL).astype(x_sample.dtype)
    return (y_prompt, y_sample, jnp.stack(gla_states, axis=1).astype(dt),
            s5_all[:, :, :, 0].astype(dt), s5_all[:, :, :, 1].astype(dt),
            jnp.stack(gdn_states, axis=1).astype(dt))
```

```python
import functools
import math

import jax
import jax.numpy as jnp
from jax import lax
from jax.experimental import pallas as pl
from jax.experimental.pallas import tpu as pltpu

F32 = jnp.float32
BF16 = jnp.bfloat16
HI = lax.Precision.HIGHEST

D_MODEL = 1024
DEPTH = 4
GRID_W = 64
CHUNK = 64
EPS = 1e-6
GLA_H, GLA_DK, GLA_DV, GLA_LR, GLA_TAU = 4, 64, 128, 16, 16.0
GLA_QK, GLA_W = GLA_H * GLA_DK, GLA_H * GLA_DV
S5_GH, S5_W, S5_P = 16, 512, 64
S5_G = S5_W // S5_GH
S5_T = 16
GDN_H, GDN_DK, GDN_DV = 4, 128, 128
GDN_W = GDN_H * GDN_DV
MIX_W = GLA_W + S5_W + GDN_W

LANE = 128
TOKEN_TILE = 256
VMEM_LIMIT = 48 * 1024 * 1024

COL_GQ, COL_GK, COL_GV, COL_GGATE, COL_SU, COL_SGATE = 0, 2, 4, 8, 12, 16
COL_DQ, COL_DK, COL_DV, COL_DGATE = 20, 24, 28, 32
MAIN_W = 36 * LANE
SM_GLR, SM_DA, SM_DB = 0, 32, 40


def _dot(a, b, precision=None):
    return lax.dot_general(a, b, (((1,), (0,)), ((), ())), precision=precision,
                           preferred_element_type=F32)


def _dot_nt(a, b, precision=None):
    return lax.dot_general(a, b, (((1,), (1,)), ((), ())), precision=precision,
                           preferred_element_type=F32)


def _dot_tn(a, b, precision=None):
    return lax.dot_general(a, b, (((0,), (0,)), ((), ())), precision=precision,
                           preferred_element_type=F32)


def _split_bf16(x):
    hi = x.astype(BF16)
    return hi, (x - hi.astype(F32)).astype(BF16)


def _dot3(a, b):
    a_hi, a_lo = _split_bf16(a)
    b_hi, b_lo = _split_bf16(b)
    return _dot(a_hi, b_hi) + (_dot(a_hi, b_lo) + _dot(a_lo, b_hi))


def _silu(x):
    return x * jax.nn.sigmoid(x)


def _softplus(x):
    return jnp.maximum(x, 0.0) + jnp.log1p(jnp.exp(-jnp.abs(x)))


def _params(sem):
    return pltpu.CompilerParams(dimension_semantics=sem, vmem_limit_bytes=VMEM_LIMIT)


def _adaln_kernel(cond_ref, w_ref, b_ref, o_ref):
    c = cond_ref[...]
    o_ref[0] = _dot(_silu(c), w_ref[0], precision=HI) + b_ref[0]


def _adaln(cond, w_ada, b_ada):
    rows = cond.shape[0]
    nj = 3 * D_MODEL // 1024
    return pl.pallas_call(
        _adaln_kernel,
        out_shape=jax.ShapeDtypeStruct((DEPTH, rows, 3 * D_MODEL), F32),
        grid=(DEPTH, nj),
        in_specs=[pl.BlockSpec((rows, D_MODEL), lambda l, j: (0, 0)),
                  pl.BlockSpec((1, D_MODEL, 1024), lambda l, j: (l, 0, j)),
                  pl.BlockSpec((1, 1, 1024), lambda l, j: (l, 0, j))],
        out_specs=pl.BlockSpec((1, rows, 1024), lambda l, j: (l, 0, j)),
        compiler_params=_params(("arbitrary", "arbitrary")),
        name="adaln",
    )(cond, w_ada, b_ada.reshape(DEPTH, 1, 3 * D_MODEL))


def _wprep_kernel(w_ref, m_ref, s_ref):
    x = w_ref[0]
    o = _W_IN_OFFS
    main = jnp.concatenate([x[:, o['gq'][0]:o['gv'][1]], x[:, o['ggate'][0]:o['dqkv'][1]],
                            x[:, o['dgate'][0]:o['dgate'][1]]], axis=1)
    small = jnp.concatenate([x[:, o['glr'][0]:o['glr'][1]], x[:, o['da'][0]:o['db'][1]],
                             jnp.zeros((x.shape[0], LANE - 2 * GLA_LR - 4 * GDN_H), x.dtype)], axis=1)
    m_ref[0] = main.astype(BF16)
    s_ref[0] = small.astype(BF16)


def _w_in_offsets():
    offs, pos = {}, 0
    for name, width in (('gq', GLA_QK), ('gk', GLA_QK), ('gv', GLA_W), ('glr', 2 * GLA_LR), ('ggate', GLA_W),
                        ('su', S5_W), ('sgate', S5_W), ('dqkv', 3 * GDN_W), ('da', 2 * GDN_H),
                        ('db', 2 * GDN_H), ('dgate', GDN_W)):
        offs[name] = (pos, pos + width)
        pos += width
    return offs


_W_IN_OFFS = _w_in_offsets()


def _wprep(w_in):
    depth, d_model, in_dim = w_in.shape
    rows = 256
    return pl.pallas_call(
        _wprep_kernel,
        out_shape=(jax.ShapeDtypeStruct((depth, d_model, MAIN_W), BF16),
                   jax.ShapeDtypeStruct((depth, d_model, LANE), BF16)),
        grid=(depth, d_model // rows),
        in_specs=[pl.BlockSpec((1, rows, in_dim), lambda l, i: (l, i, 0))],
        out_specs=(pl.BlockSpec((1, rows, MAIN_W), lambda l, i: (l, i, 0)),
                   pl.BlockSpec((1, rows, LANE), lambda l, i: (l, i, 0))),
        compiler_params=_params(("arbitrary", "arbitrary")),
        name="wprep",
    )(w_in)


def _inproj_kernel(xc_ref, xs_ref, ada_ref, g_ref, wm_ref, ws_ref, om_ref, os_ref, *, ctx_tiles):
    x = jnp.where(pl.program_id(0) < ctx_tiles, xc_ref[...], xs_ref[...])
    nrm = x * lax.rsqrt(jnp.mean(x * x, axis=-1, keepdims=True) + EPS) * g_ref[...]
    ada = ada_ref[0]
    shift = ada[:, 0:D_MODEL]
    scale = ada[:, D_MODEL:2 * D_MODEL]
    h = (nrm * (1.0 + scale) + shift).astype(BF16)
    om_ref[...] = _dot(h, wm_ref[...])
    os_ref[...] = _dot(h, ws_ref[...])


def _ada_row(i, ctx_tiles, tiles_per_latent):
    return jnp.where(i < ctx_tiles, 0, 1 + (i - ctx_tiles) // tiles_per_latent)


def _ctx_tile(i, ctx_tiles):
    return jnp.minimum(i, ctx_tiles - 1)


def _lat_tile(i, ctx_tiles):
    return jnp.maximum(i - ctx_tiles, 0)


def _inproj(x_c, x_s, ada, norm_pre, w_main, w_small, layer, ctx_tiles, tiles_per_latent):
    nt = x_c.shape[0] + x_s.shape[0]
    row = functools.partial(_ada_row, ctx_tiles=ctx_tiles, tiles_per_latent=tiles_per_latent)
    return pl.pallas_call(
        functools.partial(_inproj_kernel, ctx_tiles=ctx_tiles),
        out_shape=(jax.ShapeDtypeStruct((nt, MAIN_W), F32), jax.ShapeDtypeStruct((nt, LANE), F32)),
        grid=(nt // TOKEN_TILE,),
        in_specs=[pl.BlockSpec((TOKEN_TILE, D_MODEL), lambda i: (_ctx_tile(i, ctx_tiles), 0)),
                  pl.BlockSpec((TOKEN_TILE, D_MODEL), lambda i: (_lat_tile(i, ctx_tiles), 0)),
                  pl.BlockSpec((None, 1, 1, 3 * D_MODEL), lambda i: (layer, row(i), 0, 0)),
                  pl.BlockSpec((None, 1, D_MODEL), lambda i: (layer, 0, 0)),
                  pl.BlockSpec((None, D_MODEL, MAIN_W), lambda i: (layer, 0, 0)),
                  pl.BlockSpec((None, D_MODEL, LANE), lambda i: (layer, 0, 0))],
        out_specs=(pl.BlockSpec((TOKEN_TILE, MAIN_W), lambda i: (i, 0)),
                   pl.BlockSpec((TOKEN_TILE, LANE), lambda i: (i, 0))),
        compiler_params=_params(("arbitrary",)),
        name="inproj",
    )(x_c, x_s, ada, norm_pre, w_main, w_small)


def _tri_mask(n, reverse, strict=False):
    r = lax.broadcasted_iota(jnp.int32, (n, n), 0)
    c = lax.broadcasted_iota(jnp.int32, (n, n), 1)
    if reverse:
        return (r < c) if strict else (r <= c)
    return (r > c) if strict else (r >= c)


def _eye_mask(n):
    return lax.broadcasted_iota(jnp.int32, (n, n), 0) == lax.broadcasted_iota(jnp.int32, (n, n), 1)


GLA_CPI = 4


def _state_refs(rest, has_state, emit_state, aliased):
    rest = list(rest)
    s0_ref = rest.pop(0) if has_state else None
    if aliased:
        rest.pop(0)
    o_ref = rest.pop(0)
    sf_ref = rest.pop(0) if emit_state else None
    return s0_ref, o_ref, sf_ref, rest


def _gla_kernel(q_ref, k_ref, v_ref, sm_ref, gw_ref, gb_ref, *rest, n_chunks, has_state, emit_state, aliased):
    s0_ref, o_ref, sf_ref, (g_s, ob_s, st_ref) = _state_refs(rest, has_state, emit_state, aliased)
    for d in range(2):
        if has_state:
            s0 = jnp.concatenate([s0_ref[0, d, 0], s0_ref[0, d, 1]], axis=0)
            st_ref[d] = s0.T
        else:
            st_ref[d] = jnp.zeros((GLA_DV, LANE), F32)
        z = _dot3(sm_ref[...], gw_ref[d]) + gb_ref[d]
        g_s[d] = -_softplus(-z) * (1.0 / GLA_TAU)

    lane = lax.broadcasted_iota(jnp.int32, (CHUNK, LANE), 1)
    head_mask = [lane < GLA_DK, lane >= GLA_DK]
    scale = GLA_DK ** -0.5
    causal = [_tri_mask(CHUNK, False), _tri_mask(CHUNK, True)]
    causal_bf = [m.astype(BF16) for m in causal]

    def chunk_step(it, carry):
        jd = [(j, d) for j in range(GLA_CPI) for d in range(2)]
        jdh = [(j, d, h) for (j, d) in jd for h in range(2)]
        cc = {(j, d): (it * GLA_CPI + j) if d == 0 else n_chunks - 1 - (it * GLA_CPI + j) for (j, d) in jd}
        rows = {u: pl.ds(pl.multiple_of(cc[u] * CHUNK, CHUNK), CHUNK) for u in jd}
        q = {u: q_ref[rows[u], :] for u in jd}
        k = {u: k_ref[rows[u], :] for u in jd}
        v = {u: v_ref[rows[u], :] for u in jd}
        g = {(j, d): g_s[d, rows[j, d], :] for (j, d) in jd}
        g_hi = {u: g[u].astype(BF16) for u in jd}
        g_lo = {u: (g[u] - g_hi[u].astype(F32)).astype(BF16) for u in jd}
        b = {u: _dot(causal_bf[u[1]], g_hi[u]) + _dot(causal_bf[u[1]], g_lo[u]) for u in jd}
        b_last = {(j, d): b[j, d][CHUNK - 1:CHUNK, :] if d == 0 else b[j, d][0:1, :] for (j, d) in jd}
        qe = {u: q[u] * jnp.exp(b[u]) * scale for u in jd}
        ke = {u: (k[u] * jnp.exp(-b[u])).astype(BF16) for u in jd}
        kd = {u: k[u] * jnp.exp(b_last[u] - b[u]) for u in jd}
        qh = {(j, d, h): jnp.where(head_mask[h], qe[j, d], 0.0).astype(BF16) for (j, d, h) in jdh}
        vh = {(j, d, h): v[j, d][:, h * GLA_DV:(h + 1) * GLA_DV].astype(BF16) for (j, d, h) in jdh}
        sc = {(j, d, h): jnp.where(causal[d], _dot_nt(qh[j, d, h], ke[j, d]), 0.0).astype(BF16)
              for (j, d, h) in jdh}
        upd = {(j, d, h): _dot_tn(vh[j, d, h], jnp.where(head_mask[h], kd[j, d], 0.0).astype(BF16))
               for (j, d, h) in jdh}
        o_intra = {u: _dot(sc[u], vh[u]) for u in jdh}
        for j in range(GLA_CPI):
            st_bf = {d: st_ref[d].astype(BF16) for d in range(2)}
            for d in range(2):
                for h in range(2):
                    oh = o_intra[j, d, h] + _dot_nt(qh[j, d, h], st_bf[d])
                    if d == 0:
                        o_ref[rows[j, d], h * GLA_DV:(h + 1) * GLA_DV] = oh
                    else:
                        ob_s[rows[j, d], h * GLA_DV:(h + 1) * GLA_DV] = oh
                st_ref[d] = st_ref[d] * jnp.exp(b_last[j, d]) + upd[j, d, 0] + upd[j, d, 1]
        return carry

    lax.fori_loop(0, n_chunks // GLA_CPI, chunk_step, 0)
    o_ref[...] += ob_s[...]
    if emit_state:
        for d in range(2):
            s = st_ref[d].T
            for h in range(2):
                sf_ref[0, d, h] = s[h * GLA_DK:(h + 1) * GLA_DK, :]


def _state_plumbing(layer, s0, sf_acc, nseq, heads, hps, dk, dv, n_lead):
    blk = (1, None, 2, hps, dk, dv)
    imap = lambda b, p, *_: (b, layer, 0, p, 0, 0)
    in_specs, operands, out_specs, out_shapes, aliases = [], [], [], [], {}
    if s0 is not None:
        in_specs.append(pl.BlockSpec(blk, imap))
        operands.append(s0)
    emit = sf_acc is not False
    if emit:
        if sf_acc is not None:
            aliases[n_lead + len(operands)] = 1
            in_specs.append(pl.BlockSpec(memory_space=pl.ANY))
            operands.append(sf_acc)
        out_specs.append(pl.BlockSpec(blk, imap))
        out_shapes.append(jax.ShapeDtypeStruct((nseq, DEPTH, 2, heads, dk, dv), F32))
    flags = dict(has_state=s0 is not None, emit_state=emit, aliased=emit and sf_acc is not None)
    return in_specs, operands, out_specs, out_shapes, aliases, flags


def _gla(pm, ps, gw_pad, gb, layer, s0, sf_acc, *, nseq, seq_len, row_block0):
    st_in, st_ops, st_out, st_shapes, aliases, flags = _state_plumbing(
        layer, s0, sf_acc, nseq, GLA_H, 2, GLA_DK, GLA_DV, n_lead=6)
    kern = functools.partial(_gla_kernel, n_chunks=seq_len // CHUNK, **flags)
    return pl.pallas_call(
        kern,
        out_shape=[jax.ShapeDtypeStruct((nseq * seq_len, GLA_W), F32)] + st_shapes,
        grid=(nseq, 2),
        in_specs=[pl.BlockSpec((seq_len, LANE), lambda b, p: (row_block0 + b, COL_GQ + p)),
                  pl.BlockSpec((seq_len, LANE), lambda b, p: (row_block0 + b, COL_GK + p)),
                  pl.BlockSpec((seq_len, 2 * LANE), lambda b, p: (row_block0 + b, COL_GV // 2 + p)),
                  pl.BlockSpec((seq_len, LANE), lambda b, p: (row_block0 + b, 0)),
                  pl.BlockSpec((None, 2, LANE, LANE), lambda b, p: (layer, 0, 0, p)),
                  pl.BlockSpec((None, 2, 1, LANE), lambda b, p: (layer, 0, 0, p))] + st_in,
        out_specs=[pl.BlockSpec((seq_len, 2 * LANE), lambda b, p: (b, p))] + st_out,
        scratch_shapes=[pltpu.VMEM((2, seq_len, LANE), F32), pltpu.VMEM((seq_len, 2 * LANE), F32),
                        pltpu.VMEM((2, GLA_DV, LANE), F32)],
        input_output_aliases=aliases,
        compiler_params=_params(("arbitrary", "arbitrary")),
        name="gla",
    )(pm, pm, pm, ps, gw_pad, gb, *st_ops)


GDN_HPS = 4
GDN_CPI = 2
assert GDN_CPI * CHUNK == LANE


def _block_mask(n, s, reverse):
    r = lax.broadcasted_iota(jnp.int32, (n, n), 0)
    c = lax.broadcasted_iota(jnp.int32, (n, n), 1)
    if reverse:
        r, c = c, r
    sh = s.bit_length() - 1
    same_pair = (r >> (sh + 1)) == (c >> (sh + 1))
    return same_pair & (((r >> sh) & 1) == 1) & (((c >> sh) & 1) == 0)


def _gdn_kernel(ab_ref, q_ref, k_ref, v_ref, sm_ref, cwq_ref, cwk_ref, cwv_ref, *rest,
                layer, n_chunks, period, has_state, emit_state, aliased):
    s0_ref, o_ref, sf_ref, scratch = _state_refs(rest, has_state, emit_state, aliased)
    q_s, k_s, v_s, g_s, b_s, u_s, wq_s, a_s, kd_s, dl_s, ob_s, st_ref = scratch
    hp = pl.program_id(1)
    seq_len = n_chunks * CHUNK
    row = lax.broadcasted_iota(jnp.int32, (seq_len, LANE), 0)
    first = (row % period) == 0
    last = (row % period) == period - 1

    def conv_silu(x, w):
        xp = jnp.where(first, 0.0, pltpu.roll(x, 1, 0))
        xn = jnp.where(last, 0.0, pltpu.roll(x, seq_len - 1, 0))
        return _silu(xp * w[0:1, :] + x * w[1:2, :] + xn * w[2:3, :])

    def l2norm(x):
        return x * lax.rsqrt(jnp.sum(x * x, axis=-1, keepdims=True) + EPS)

    lane1 = lax.broadcasted_iota(jnp.int32, (1, LANE), 1)
    a_log = jnp.zeros((1, LANE), F32)
    dt_bias = jnp.zeros((1, LANE), F32)
    for d in range(2):
        for hh in range(GDN_H):
            a_log = jnp.where(lane1 == SM_DA + d * GDN_H + hh, ab_ref[layer, d, hh], a_log)
            dt_bias = jnp.where(lane1 == SM_DA + d * GDN_H + hh, ab_ref[layer, d, GDN_H + hh], dt_bias)
    sm = sm_ref[...]
    g_all = -jnp.exp(a_log) * _softplus(sm + dt_bias)
    b_all = jax.nn.sigmoid(sm)
    lane = lax.broadcasted_iota(jnp.int32, (seq_len, LANE), 1)
    in_chunk = row % CHUNK

    def chunk_cumsum(x, reverse):
        sh = 1
        while sh < CHUNK:
            if reverse:
                x = x + jnp.where(in_chunk < CHUNK - sh, pltpu.roll(x, seq_len - sh, 0), 0.0)
            else:
                x = x + jnp.where(in_chunk >= sh, pltpu.roll(x, sh, 0), 0.0)
            sh *= 2
        return x

    gc_all = [chunk_cumsum(g_all, False), chunk_cumsum(g_all, True)]

    def lane_bcast(x, j):
        col = jnp.sum(jnp.where(lane == j, x, 0.0), axis=1, keepdims=True)
        return jnp.broadcast_to(col, (seq_len, LANE))

    for h in range(GDN_HPS):
        hd = hp * GDN_HPS + h
        lanes = slice(h * LANE, (h + 1) * LANE)
        q_s[h] = l2norm(conv_silu(q_ref[:, lanes], cwq_ref[:, lanes])) * GDN_DK ** -0.5
        k_s[h] = l2norm(conv_silu(k_ref[:, lanes], cwk_ref[:, lanes]))
        v_s[h] = conv_silu(v_ref[:, lanes], cwv_ref[:, lanes])
        for d in range(2):
            g_s[h, d] = lane_bcast(gc_all[d], SM_DA + d * GDN_H + hd)
            b_s[h, d] = lane_bcast(b_all, SM_DB + d * GDN_H + hd)
            if has_state:
                st_ref[h, d] = s0_ref[0, d, h]
            else:
                st_ref[h, d] = jnp.zeros((GDN_DK, GDN_DV), F32)

    causal = [_tri_mask(CHUNK, False), _tri_mask(CHUNK, True)]
    strict_f = [_tri_mask(CHUNK, d == 1, strict=True).astype(F32) for d in range(2)]
    level_masks = [[_block_mask(CHUNK, 1 << j, d == 1).astype(F32) for j in range(CHUNK.bit_length() - 1)]
                   for d in range(2)]

    eye_f = _eye_mask(CHUNK).astype(F32)

    def phase_a(it, carry):
        cs = [it * GDN_CPI + j for j in range(GDN_CPI)]
        rows = [pl.ds(pl.multiple_of(c * CHUNK, CHUNK), CHUNK) for c in cs]
        rows2 = [pl.ds(pl.multiple_of(c * 2 * CHUNK, 2 * CHUNK), CHUNK) for c in cs]
        rows2b = [pl.ds(pl.multiple_of(c * 2 * CHUNK + CHUNK, CHUNK), CHUNK) for c in cs]
        pairs = [(j, h) for j in range(GDN_CPI) for h in range(GDN_HPS)]
        units = [(j, h, d) for (j, h) in pairs for d in range(2)]
        q = {(j, h): q_s[h, rows[j], :] for (j, h) in pairs}
        k = {(j, h): k_s[h, rows[j], :] for (j, h) in pairs}
        v = {(j, h): v_s[h, rows[j], :] for (j, h) in pairs}
        k_bf = {p: k[p].astype(BF16) for p in pairs}
        beta = {(j, h, d): b_s[h, d, rows[j], :] for (j, h, d) in units}
        gc = {(j, h, d): g_s[h, d, rows[j], :] for (j, h, d) in units}
        gc_t = {}
        for h in range(GDN_HPS):
            for d in range(2):
                blk_t = jnp.concatenate([gc[j, h, d] for j in range(GDN_CPI)], axis=0).T
                for j in range(GDN_CPI):
                    gc_t[j, h, d] = blk_t[0:CHUNK, j * CHUNK:(j + 1) * CHUNK]
        qk = {p: _dot_nt(q[p].astype(BF16), k_bf[p]) for p in pairs}
        kk = {p: _dot_nt(k_bf[p], k_bf[p]) for p in pairs}
        decay, m, t = {}, {}, {}
        for u in units:
            j, h, d = u
            diff = gc[u][:, :CHUNK] - gc_t[u]
            decay[u] = jnp.where(causal[d], jnp.exp(jnp.where(causal[d], diff, 0.0)), 0.0)
            m[u] = kk[j, h] * beta[u][:, :CHUNK] * decay[u] * strict_f[d]
            t[u] = eye_f - m[u] * level_masks[d][0]
        for lvl in range(1, len(level_masks[0])):
            t_bf = {u: t[u].astype(BF16) for u in units}
            p1 = {u: _dot(t_bf[u], (m[u] * level_masks[u[2]][lvl]).astype(BF16)).astype(BF16) for u in units}
            t = {u: t[u] - _dot(p1[u], t_bf[u]) for u in units}
        egc = {u: jnp.exp(gc[u]) for u in units}
        rhs = {(j, h, d): jnp.concatenate([v[j, h] * beta[j, h, d], k[j, h] * (beta[j, h, d] * egc[j, h, d])],
                                          axis=1).astype(BF16) for (j, h, d) in units}
        uw = {u: _dot(t[u].astype(BF16), rhs[u]) for u in units}
        for u in units:
            j, h, d = u
            gc_last = gc[u][CHUNK - 1:CHUNK, :] if d == 0 else gc[u][0:1, :]
            u_s[h, d, rows[j], :] = uw[u][:, 0:GDN_DV]
            wq_s[h, d, rows2[j], :] = uw[u][:, GDN_DV:GDN_DV + GDN_DK].astype(BF16)
            wq_s[h, d, rows2b[j], :] = (q[j, h] * egc[u]).astype(BF16)
            a_s[h, d, rows[j], :] = (qk[j, h] * decay[u]).astype(BF16)
            kd_s[h, d, rows[j], :] = (k[j, h] * jnp.exp(gc_last - gc[u])).astype(BF16)
            dl_s[h, d, cs[j]] = jnp.broadcast_to(jnp.exp(gc_last), (8, LANE))
        return carry

    lax.fori_loop(0, n_chunks // GDN_CPI, phase_a, 0)

    def phase_b(i, carry):
        chains = [(h, d) for h in range(GDN_HPS) for d in range(2)]
        cc = {0: i, 1: n_chunks - 1 - i}
        rows = {d: pl.ds(pl.multiple_of(cc[d] * CHUNK, CHUNK), CHUNK) for d in range(2)}
        rows2 = {d: pl.ds(pl.multiple_of(cc[d] * 2 * CHUNK, 2 * CHUNK), 2 * CHUNK) for d in range(2)}
        s = {hd: st_ref[hd[0], hd[1]] for hd in chains}
        ws = {(h, d): _dot(wq_s[h, d, rows2[d], :], s[h, d].astype(BF16)) for (h, d) in chains}
        v_new = {(h, d): (u_s[h, d, rows[d], :] - ws[h, d][0:CHUNK, :]).astype(BF16) for (h, d) in chains}
        o = {(h, d): ws[h, d][CHUNK:2 * CHUNK, :] + _dot(a_s[h, d, rows[d], :], v_new[h, d]) for (h, d) in chains}
        upd = {(h, d): _dot_tn(kd_s[h, d, rows[d], :], v_new[h, d]) for (h, d) in chains}
        for (h, d) in chains:
            lanes = slice(h * LANE, (h + 1) * LANE)
            if d == 0:
                o_ref[rows[d], lanes] = o[h, d]
            else:
                ob_s[rows[d], lanes] = o[h, d]
            st_ref[h, d] = dl_s[h, d, cc[d]][0:1, :] * s[h, d] + upd[h, d]
        return carry

    lax.fori_loop(0, n_chunks, phase_b, 0)
    o_ref[...] += ob_s[...]
    if emit_state:
        for h in range(GDN_HPS):
            for d in range(2):
                sf_ref[0, d, h] = st_ref[h, d]


def _gdn(pm, ps, ab, conv_w, layer, s0, sf_acc, *, nseq, seq_len, row_block0, period):
    hps = GDN_HPS
    st_in, st_ops, st_out, st_shapes, aliases, flags = _state_plumbing(
        layer, s0, sf_acc, nseq, GDN_H, hps, GDN_DK, GDN_DV, n_lead=8)
    n_chunks = seq_len // CHUNK
    kern = functools.partial(_gdn_kernel, layer=layer, n_chunks=n_chunks, period=period, **flags)
    wide = hps * LANE
    npair = GDN_H // hps
    grid_spec = pltpu.PrefetchScalarGridSpec(
        num_scalar_prefetch=1,
        grid=(nseq, npair),
        in_specs=[pl.BlockSpec((seq_len, wide), lambda b, p, ab: (row_block0 + b, COL_DQ // hps + p),
                               pipeline_mode=pl.Buffered(1)),
                  pl.BlockSpec((seq_len, wide), lambda b, p, ab: (row_block0 + b, COL_DK // hps + p),
                               pipeline_mode=pl.Buffered(1)),
                  pl.BlockSpec((seq_len, wide), lambda b, p, ab: (row_block0 + b, COL_DV // hps + p),
                               pipeline_mode=pl.Buffered(1)),
                  pl.BlockSpec((seq_len, LANE), lambda b, p, ab: (row_block0 + b, 0)),
                  pl.BlockSpec((None, 3, wide), lambda b, p, ab: (layer, 0, p)),
                  pl.BlockSpec((None, 3, wide), lambda b, p, ab: (layer, 0, npair + p)),
                  pl.BlockSpec((None, 3, wide), lambda b, p, ab: (layer, 0, 2 * npair + p))] + st_in,
        out_specs=[pl.BlockSpec((seq_len, wide), lambda b, p, ab: (b, p))] + st_out,
        scratch_shapes=[pltpu.VMEM((hps, seq_len, LANE), F32),
                        pltpu.VMEM((hps, seq_len, LANE), F32),
                        pltpu.VMEM((hps, seq_len, LANE), F32),
                        pltpu.VMEM((hps, 2, seq_len, LANE), F32),
                        pltpu.VMEM((hps, 2, seq_len, LANE), F32),
                        pltpu.VMEM((hps, 2, seq_len, GDN_DV), F32),
                        pltpu.VMEM((hps, 2, 2 * seq_len, GDN_DK), BF16),
                        pltpu.VMEM((hps, 2, seq_len, CHUNK), BF16),
                        pltpu.VMEM((hps, 2, seq_len, GDN_DK), BF16),
                        pltpu.VMEM((hps, 2, n_chunks, 8, LANE), F32),
                        pltpu.VMEM((seq_len, wide), F32),
                        pltpu.VMEM((hps, 2, GDN_DK, GDN_DV), F32)],
    )
    return pl.pallas_call(
        kern,
        out_shape=[jax.ShapeDtypeStruct((nseq * seq_len, GDN_W), F32)] + st_shapes,
        grid_spec=grid_spec,
        input_output_aliases=aliases,
        compiler_params=_params(("arbitrary", "arbitrary")),
        name="gdn",
    )(ab, pm, pm, pm, ps, conv_w, conv_w, conv_w, *st_ops)


S5_GPB = LANE // S5_GH
S5_XROWS = 4096


def _lane_block_transpose(arrs, lane_grp):
    arrs = list(arrs)
    k = S5_GPB // 2
    while k:
        high = (lane_grp & k) != 0
        for i in range(S5_GPB):
            if not i & k:
                lo, hi = arrs[i], arrs[i + k]
                arrs[i] = jnp.where(high, pltpu.roll(hi, k * S5_GH, 1), lo)
                arrs[i + k] = jnp.where(high, hi, pltpu.roll(lo, LANE - k * S5_GH, 1))
        k //= 2
    return arrs


def _s5_kernel(*refs, n_chunks, nseq, latent):
    n_x = len(refs) - 10
    x_refs = refs[:n_x]
    mi_ref, wst_ref, wout_ref, lam_ref, h0_ref, y_ref, fin_ref, u_s, e_s, y_s = refs[n_x:]
    seq_len = n_chunks * S5_T
    seq_per_x = S5_XROWS // seq_len
    lane_grp = lax.broadcasted_iota(jnp.int32, (n_chunks, LANE), 1) >> 4

    def slab_rows(base, t):
        if latent:
            return pl.ds(pl.multiple_of(base + t * GRID_W, GRID_W), n_chunks)
        return pl.ds(base + t, n_chunks, stride=S5_T)

    def relayout_in(x_ref, b0):
        def body(bl, carry):
            base = bl * seq_len
            b = b0 + bl
            for lt in range(2):
                slabs = [x_ref[slab_rows(base, S5_GPB * lt + tt), :] for tt in range(S5_GPB)]
                for gi, tile in enumerate(_lane_block_transpose(slabs, lane_grp)):
                    u_s[gi, lt, pl.ds(b, n_chunks, stride=nseq), :] = tile
            return carry
        lax.fori_loop(0, seq_per_x, body, 0)

    for i, x_ref in enumerate(x_refs):
        relayout_in(x_ref, i * seq_per_x)

    half = 2 * S5_P
    u_bf = [jnp.concatenate([u_s[gi, 0], u_s[gi, 1]], axis=1).astype(BF16) for gi in range(S5_GPB)]
    for gi in range(S5_GPB):
        e = _dot(u_bf[gi], wst_ref[gi])
        e_s[gi, 0] = e[:, 0:half]
        e_s[gi, 1] = e[:, half:2 * half]
    for gi in range(S5_GPB):
        y = _dot(u_bf[gi], mi_ref[gi])
        y_s[gi, 0] = y[:, 0:LANE]
        y_s[gi, 1] = y[:, LANE:2 * LANE]

    lam = [lam_ref[gi] for gi in range(S5_GPB)]

    def step(c, carry):
        rf = pl.ds(pl.multiple_of(c * nseq, nseq), nseq)
        rb = pl.ds(pl.multiple_of((n_chunks - 1 - c) * nseq, nseq), nseq)
        new = []
        for gi in range(S5_GPB):
            xf, xb = carry[2 * gi], carry[2 * gi + 1]
            u_s[gi, 0, rf, :] = xf
            u_s[gi, 1, rb, :] = xb
            l = lam[gi]
            new.append(xf * l[0:1, :] + pltpu.roll(xf, S5_P, 1) * l[1:2, :] + e_s[gi, 0, rf, :])
            new.append(xb * l[2:3, :] + pltpu.roll(xb, S5_P, 1) * l[3:4, :] + e_s[gi, 1, rb, :])
        return tuple(new)

    init = []
    for gi in range(S5_GPB):
        h0 = h0_ref[gi]
        init += [h0[:, 0:half], h0[:, half:2 * half]]
    fin = lax.fori_loop(0, n_chunks, step, tuple(init))
    for gi in range(S5_GPB):
        fin_ref[gi, :, 0:half] = fin[2 * gi]
        fin_ref[gi, :, half:2 * half] = fin[2 * gi + 1]
        x_in = jnp.concatenate([u_s[gi, 0], u_s[gi, 1]], axis=1).astype(BF16)
        y = _dot(x_in, wout_ref[gi])
        y_s[gi, 0] += y[:, 0:LANE]
        y_s[gi, 1] += y[:, LANE:2 * LANE]

    def relayout_out(b, carry):
        base = b * seq_len
        for lt in range(2):
            tiles = [y_s[gi, lt, pl.ds(b, n_chunks, stride=nseq), :] for gi in range(S5_GPB)]
            for tt, slab in enumerate(_lane_block_transpose(tiles, lane_grp)):
                y_ref[slab_rows(base, S5_GPB * lt + tt), :] = slab
        return carry

    lax.fori_loop(0, nseq, relayout_out, 0)


def _s5(pm, m_intra, w_st, w_out, lam_t, h0, layer, h0_layer, *, nseq, n_chunks, row_block0, latent):
    rows = n_chunks * nseq
    n_tok = rows * S5_T
    n_x = n_tok // S5_XROWS
    kern = functools.partial(_s5_kernel, n_chunks=n_chunks, nseq=nseq, latent=latent)
    wspec = pl.BlockSpec((None, S5_GPB, 2 * LANE, 2 * LANE), lambda j: (layer, j, 0, 0))
    x_specs = [pl.BlockSpec((S5_XROWS, LANE), functools.partial(lambda j, i: (row_block0 + i, COL_SU + j), i=i))
               for i in range(n_x)]
    plane = pltpu.VMEM((S5_GPB, 2, rows, LANE), F32)
    return pl.pallas_call(
        kern,
        out_shape=(jax.ShapeDtypeStruct((n_tok, S5_W), F32),
                   jax.ShapeDtypeStruct((S5_G, nseq, 2 * LANE), F32)),
        grid=(S5_G // S5_GPB,),
        in_specs=x_specs + [wspec, wspec, wspec,
                            pl.BlockSpec((None, S5_GPB, 4, LANE), lambda j: (layer, j, 0, 0)),
                            pl.BlockSpec((None, S5_GPB, nseq, 2 * LANE), lambda j: (h0_layer, j, 0, 0))],
        out_specs=(pl.BlockSpec((n_tok, LANE), lambda j: (0, j)),
                   pl.BlockSpec((S5_GPB, nseq, 2 * LANE), lambda j: (j, 0, 0))),
        scratch_shapes=[plane, plane, plane],
        compiler_params=_params(("arbitrary",)),
        name="s5",
    )(*([pm] * n_x), m_intra, w_st, w_out, lam_t, h0)


def _cmul(ar, ai, br, bi):
    return ar * br - ai * bi, ar * bi + ai * br


def _s5_weights(lam_re, lam_im, log_dt, b_re, b_im, c_re, c_im):
    t = S5_T
    dt = jnp.exp(log_dt)[..., None]
    tau = jnp.arange(t + 1, dtype=F32)[:, None, None, None]
    mag = jnp.exp(tau * (lam_re * dt)[None])
    ang = tau * (lam_im * dt)[None]
    pw_re, pw_im = mag * jnp.cos(ang), mag * jnp.sin(ang)
    nr, ni = pw_re[1] - 1.0, pw_im[1]
    den = lam_re * lam_re + lam_im * lam_im
    fr, fi = (nr * lam_re + ni * lam_im) / den, (ni * lam_re - nr * lam_im) / den
    bb_re, bb_im = _cmul(fr[..., None], fi[..., None], b_re[None], b_im[None])

    bt_re, bt_im = jnp.swapaxes(bb_re, -1, -2), jnp.swapaxes(bb_im, -1, -2)
    ct_re, ct_im = jnp.swapaxes(c_re, -1, -2), jnp.swapaxes(c_im, -1, -2)
    pwt_re, pwt_im = jnp.moveaxis(pw_re, 0, -1), jnp.moveaxis(pw_im, 0, -1)
    cl_re, cl_im = _cmul(pwt_re[..., None], pwt_im[..., None], ct_re[:, :, :, None, :], ct_im[:, :, :, None, :])

    def lanes(x):
        return x.reshape(S5_G, S5_P, t * S5_GH)

    def k_rows(d, taus):
        return (jnp.einsum('gip,gpx->gix', bt_re[d], lanes(cl_re[d][:, :, taus]), precision=HI)
                - jnp.einsum('gip,gpx->gix', bt_im[d], lanes(cl_im[d][:, :, taus]), precision=HI))

    width = t * S5_GH
    pad = (t - 1) * S5_GH
    kf = jnp.pad(k_rows(0, slice(0, t)), ((0, 0), (0, 0), (pad, 0)))
    kb = jnp.pad(k_rows(1, slice(t - 1, None, -1)), ((0, 0), (0, 0), (0, pad)))
    m_intra = jnp.stack([kf[:, :, pad - s * S5_GH:pad - s * S5_GH + width]
                         + kb[:, :, (t - 1 - s) * S5_GH:(t - 1 - s) * S5_GH + width] for s in range(t)], axis=1)
    m_intra = m_intra.reshape(S5_G, width, width)

    def st(d, taus):
        return _cmul(jnp.moveaxis(pw_re[taus, d], 0, 1)[:, :, None, :], jnp.moveaxis(pw_im[taus, d], 0, 1)[:, :, None, :],
                     bt_re[d][:, None], bt_im[d][:, None])
    w_st = jnp.concatenate(st(0, slice(t - 1, None, -1)) + st(1, slice(0, t)), axis=-1)
    w_st = w_st.reshape(S5_G, width, 4 * S5_P)

    w_out = jnp.concatenate([lanes(cl_re[0][:, :, 1:t + 1]), -lanes(cl_im[0][:, :, 1:t + 1]),
                             lanes(cl_re[1][:, :, t:0:-1]), -lanes(cl_im[1][:, :, t:0:-1])], axis=1)


    lt_re, lt_im = pw_re[t], pw_im[t]
    lam_t = jnp.stack([jnp.concatenate([lt_re[0], lt_re[0]], -1), jnp.concatenate([-lt_im[0], lt_im[0]], -1),
                       jnp.concatenate([lt_re[1], lt_re[1]], -1), jnp.concatenate([-lt_im[1], lt_im[1]], -1)],
                      axis=1)
    return m_intra.astype(BF16), w_st.astype(BF16), w_out.astype(BF16), lam_t


def _gelu_tanh(x):
    return 0.5 * x * (1.0 + jnp.tanh(math.sqrt(2.0 / math.pi) * (x + 0.044715 * (x * x * x))))


def _out_kernel(xc_ref, xs_ref, ada_ref, ogc_ref, ogs_ref, gg_ref, ysc_ref, yss_ref, su_ref, sg_ref,
                odc_ref, ods_ref, dg_ref, gn_ref, dn_ref, sd_ref, gw_ref, gb_ref, wo_ref, np_ref,
                oc_ref, os_ref, *, ctx_tiles):
    is_ctx = pl.program_id(0) < ctx_tiles

    def pick(c_ref, s_ref):
        return jnp.where(is_ctx, c_ref[...], s_ref[...])

    def head_norm(o, g):
        parts = []
        for h in range(o.shape[1] // LANE):
            oh = o[:, h * LANE:(h + 1) * LANE]
            parts.append(oh * lax.rsqrt(jnp.mean(oh * oh, axis=-1, keepdims=True) + EPS) * g)
        return jnp.concatenate(parts, axis=1)

    o_gla = head_norm(pick(ogc_ref, ogs_ref), gn_ref[...]) * _silu(gg_ref[...])
    y = _gelu_tanh(pick(ysc_ref, yss_ref) + sd_ref[...] * su_ref[...])
    y = y * jax.nn.sigmoid(_dot(y.astype(BF16), gw_ref[...]) + gb_ref[...])
    o_s5 = y * _silu(sg_ref[...])
    o_gdn = head_norm(pick(odc_ref, ods_ref), dn_ref[...]) * _silu(dg_ref[...])
    out = (_dot(o_gla.astype(BF16), wo_ref[0:GLA_W, :])
           + _dot(o_s5.astype(BF16), wo_ref[GLA_W:GLA_W + S5_W, :])
           + _dot(o_gdn.astype(BF16), wo_ref[GLA_W + S5_W:MIX_W, :]))
    r = out * lax.rsqrt(jnp.mean(out * out, axis=-1, keepdims=True) + EPS) * np_ref[...]
    gate = ada_ref[0][:, 2 * D_MODEL:3 * D_MODEL]
    x_new = pick(xc_ref, xs_ref) + gate * r

    @pl.when(is_ctx)
    def _():
        oc_ref[...] = x_new

    @pl.when(jnp.logical_not(is_ctx))
    def _():
        os_ref[...] = x_new


def _out(x_c, x_s, ada, pm, og_c, og_s, y_c, y_s, od_c, od_s, gla_norm, gdn_norm, s5_d, glu_w, glu_b, w_out,
         norm_post, layer, ctx_tiles, tiles_per_latent):
    nt = x_c.shape[0] + x_s.shape[0]
    row = functools.partial(_ada_row, ctx_tiles=ctx_tiles, tiles_per_latent=tiles_per_latent)
    wide = 4 * LANE

    def tok(col_block):
        return pl.BlockSpec((TOKEN_TILE, wide), lambda i: (i, col_block))

    def ctx(width):
        return pl.BlockSpec((TOKEN_TILE, width), lambda i: (_ctx_tile(i, ctx_tiles), 0))

    def lat(width):
        return pl.BlockSpec((TOKEN_TILE, width), lambda i: (_lat_tile(i, ctx_tiles), 0))

    def full(shape):
        return pl.BlockSpec((None,) + shape, lambda i: (layer,) + (0,) * len(shape))

    return pl.pallas_call(
        functools.partial(_out_kernel, ctx_tiles=ctx_tiles),
        out_shape=(jax.ShapeDtypeStruct(x_c.shape, F32), jax.ShapeDtypeStruct(x_s.shape, F32)),
        grid=(nt // TOKEN_TILE,),
        in_specs=[ctx(D_MODEL), lat(D_MODEL),
                  pl.BlockSpec((None, 1, 1, 3 * D_MODEL), lambda i: (layer, row(i), 0, 0)),
                  ctx(wide), lat(wide), tok(COL_GGATE // 4),
                  ctx(wide), lat(wide), tok(COL_SU // 4), tok(COL_SGATE // 4),
                  ctx(wide), lat(wide), tok(COL_DGATE // 4),
                  full((1, LANE)), full((1, LANE)), full((1, S5_W)), full((S5_W, S5_W)), full((1, S5_W)),
                  full((MIX_W, D_MODEL)), full((1, D_MODEL))],
        out_specs=(ctx(D_MODEL), lat(D_MODEL)),
        compiler_params=_params(("arbitrary",)),
        name="outproj",
    )(x_c, x_s, ada, og_c, og_s, pm, y_c, y_s, pm, pm, od_c, od_s, pm, gla_norm, gdn_norm, s5_d, glu_w, glu_b,
      w_out, norm_post)


def kernel(x_prompt, x_sample, c, state_gla, state_s5_re, state_s5_im, state_gdn, c_ctx, norm_pre, norm_post, w_ada, b_ada, w_in, gla_gate_w, gla_gate_b, gla_norm, s5_lam_re, s5_lam_im, s5_log_dt, s5_b_re, s5_b_im, s5_c_re, s5_c_im, s5_d, s5_glu_w, s5_glu_b, gdn_conv, gdn_a_log, gdn_dt_bias, gdn_norm, w_out):
    bp, lp, _ = x_prompt.shape
    bs, ls, _ = x_sample.shape
    n_ctx = bp * lp
    assert lp % TOKEN_TILE == 0 and ls % TOKEN_TILE == 0 and n_ctx % ls == 0
    assert ls // GRID_W == S5_T and lp % S5_T == 0 and n_ctx % S5_XROWS == 0 and (bs * ls) % S5_XROWS == 0
    ctx_tiles = n_ctx // TOKEN_TILE
    tiles_per_latent = ls // TOKEN_TILE

    cond = jnp.concatenate([c_ctx[None].astype(F32), c.astype(F32)], axis=0)
    rows = -(-cond.shape[0] // 8) * 8
    cond = jnp.pad(cond, ((0, rows - cond.shape[0]), (0, 0)))
    ada = _adaln(cond, w_ada.astype(F32), b_ada.astype(F32)).reshape(DEPTH, rows, 1, 3 * D_MODEL)

    w_main, w_small = _wprep(w_in.astype(F32))
    w_out_bf = w_out.astype(BF16)
    glu_w_bf = s5_glu_w.astype(BF16)
    gw_pad = jnp.zeros((DEPTH, 2, LANE, GLA_QK), F32)
    for d in range(2):
        gw_pad = gw_pad.at[:, d, SM_GLR + d * GLA_LR:SM_GLR + (d + 1) * GLA_LR, :].set(gla_gate_w[:, d].astype(F32))
    gb = gla_gate_b.astype(F32).reshape(DEPTH, 2, 1, GLA_QK)
    gdn_ab = jnp.concatenate([gdn_a_log, gdn_dt_bias], axis=-1).astype(F32)
    conv_w = gdn_conv.astype(F32)
    m_intra, w_st, w_ro, lam_t = jax.vmap(_s5_weights)(*(p.astype(F32) for p in (
        s5_lam_re, s5_lam_im, s5_log_dt, s5_b_re, s5_b_im, s5_c_re, s5_c_im)))
    h0_c = jnp.zeros((1, S5_G, bp, 4 * S5_P), F32)
    sre, sim = state_s5_re.astype(F32), state_s5_im.astype(F32)
    h0_s = jnp.concatenate([sre[:, :, 0], sim[:, :, 0], sre[:, :, 1], sim[:, :, 1]], axis=-1).transpose(1, 2, 0, 3)
    vec = lambda p: p.astype(F32).reshape(DEPTH, 1, -1)
    norm_pre_v, norm_post_v, gla_norm_v, gdn_norm_v = vec(norm_pre), vec(norm_post), vec(gla_norm), vec(gdn_norm)
    s5_d_v, glu_b_v = vec(s5_d), vec(s5_glu_b)
    st_gla, st_gdn = state_gla.astype(F32), state_gdn.astype(F32)

    x_c = x_prompt.reshape(n_ctx, D_MODEL).astype(F32)
    x_s = x_sample.reshape(bs * ls, D_MODEL).astype(F32)
    new_gla, new_gdn, s5_states = None, None, []
    lat0 = n_ctx // ls
    for l in range(DEPTH):
        pm, ps = _inproj(x_c, x_s, ada, norm_pre_v, w_main, w_small, l, ctx_tiles, tiles_per_latent)
        og_c, new_gla = _gla(pm, ps, gw_pad, gb, l, None, new_gla, nseq=bp, seq_len=lp, row_block0=0)
        og_s, = _gla(pm, ps, gw_pad, gb, l, st_gla, False, nseq=bs, seq_len=ls, row_block0=lat0)
        od_c, new_gdn = _gdn(pm, ps, gdn_ab, conv_w, l, None, new_gdn, nseq=bp, seq_len=lp, row_block0=0, period=lp)
        od_s, = _gdn(pm, ps, gdn_ab, conv_w, l, st_gdn, False, nseq=bs, seq_len=ls, row_block0=lat0,
                     period=GRID_W)
        y_c, fin = _s5(pm, m_intra, w_st, w_ro, lam_t, h0_c, l, 0, nseq=bp, n_chunks=lp // S5_T, row_block0=0,
                       latent=False)
        y_s, _ = _s5(pm, m_intra, w_st, w_ro, lam_t, h0_s, l, l, nseq=bs, n_chunks=GRID_W,
                     row_block0=n_ctx // S5_XROWS, latent=True)
        x_c, x_s = _out(x_c, x_s, ada, pm, og_c, og_s, y_c, y_s, od_c, od_s, gla_norm_v, gdn_norm_v, s5_d_v,
                        glu_w_bf, glu_b_v, w_out_bf, norm_post_v, l, ctx_tiles, tiles_per_latent)
        fin = fin.transpose(1, 0, 2).reshape(bp, S5_G, 2, 2, S5_P)
        s5_states.append(fin.transpose(0, 2, 3, 1, 4))

    dt = x_prompt.dtype
    s5_all = jnp.stack(s5_states, axis=1)
    y_prompt = x_c.reshape(bp, lp, D_MODEL).astype(dt)
    y_sample = x_s.reshape(bs, ls, D_MODEL).astype(x_sample.dtype)
    return (y_prompt, y_sample, new_gla.astype(dt), s5_all[:, :, :, 0].astype(dt), s5_all[:, :, :, 1].astype(dt),
            new_gdn.astype(dt))
```

```python
import functools
import math

import jax
import jax.numpy as jnp
from jax import lax
from jax.experimental import pallas as pl
from jax.experimental.pallas import tpu as pltpu

F32 = jnp.float32
BF16 = jnp.bfloat16
HI = lax.Precision.HIGHEST

D_MODEL = 1024
DEPTH = 4
GRID_W = 64
CHUNK = 64
EPS = 1e-6
GLA_H, GLA_DK, GLA_DV, GLA_LR, GLA_TAU = 4, 64, 128, 16, 16.0
GLA_QK, GLA_W = GLA_H * GLA_DK, GLA_H * GLA_DV
S5_GH, S5_W, S5_P = 16, 512, 64
S5_G = S5_W // S5_GH
S5_T = 16
GDN_H, GDN_DK, GDN_DV = 4, 128, 128
GDN_W = GDN_H * GDN_DV
MIX_W = GLA_W + S5_W + GDN_W

LANE = 128
TOKEN_TILE = 256
VMEM_LIMIT = 48 * 1024 * 1024

COL_GQ, COL_GK, COL_GV, COL_GGATE, COL_SGATE = 0, 2, 4, 8, 12
COL_DQ, COL_DK, COL_DV, COL_DGATE = 16, 20, 24, 28
MAIN_W = 32 * LANE
COL_SU, COL_SM = 0, 4
SIDE_W = 5 * LANE
SM_GLR, SM_DA, SM_DB = 0, 32, 40


def _dot(a, b, precision=None):
    return lax.dot_general(a, b, (((1,), (0,)), ((), ())), precision=precision,
                           preferred_element_type=F32)


def _dot_nt(a, b, precision=None):
    return lax.dot_general(a, b, (((1,), (1,)), ((), ())), precision=precision,
                           preferred_element_type=F32)


def _dot_tn(a, b, precision=None):
    return lax.dot_general(a, b, (((0,), (0,)), ((), ())), precision=precision,
                           preferred_element_type=F32)


def _split_bf16(x):
    hi = x.astype(BF16)
    return hi, (x - hi.astype(F32)).astype(BF16)


def _dot3(a, b):
    a_hi, a_lo = _split_bf16(a)
    b_hi, b_lo = _split_bf16(b)
    return _dot(a_hi, b_hi) + (_dot(a_hi, b_lo) + _dot(a_lo, b_hi))


def _silu(x):
    return x * jax.nn.sigmoid(x)


def _softplus(x):
    return jnp.maximum(x, 0.0) + jnp.log1p(jnp.exp(-jnp.abs(x)))


def _params(sem):
    return pltpu.CompilerParams(dimension_semantics=sem, vmem_limit_bytes=VMEM_LIMIT)


def _adaln_kernel(cond_ref, w_ref, b_ref, o_ref):
    c = cond_ref[...]
    o_ref[0] = _dot(_silu(c), w_ref[0], precision=HI) + b_ref[0]


def _adaln(cond, w_ada, b_ada):
    rows = cond.shape[0]
    nj = 3 * D_MODEL // 1024
    return pl.pallas_call(
        _adaln_kernel,
        out_shape=jax.ShapeDtypeStruct((DEPTH, rows, 3 * D_MODEL), F32),
        grid=(DEPTH, nj),
        in_specs=[pl.BlockSpec((rows, D_MODEL), lambda l, j: (0, 0)),
                  pl.BlockSpec((1, D_MODEL, 1024), lambda l, j: (l, 0, j)),
                  pl.BlockSpec((1, 1, 1024), lambda l, j: (l, 0, j))],
        out_specs=pl.BlockSpec((1, rows, 1024), lambda l, j: (l, 0, j)),
        compiler_params=_params(("arbitrary", "arbitrary")),
        name="adaln",
    )(cond, w_ada, b_ada.reshape(DEPTH, 1, 3 * D_MODEL))


def _wprep_kernel(w_ref, m_ref, s_ref):
    x = w_ref[0]
    o = _W_IN_OFFS
    main = jnp.concatenate([x[:, o['gq'][0]:o['gv'][1]], x[:, o['ggate'][0]:o['ggate'][1]],
                            x[:, o['sgate'][0]:o['dqkv'][1]], x[:, o['dgate'][0]:o['dgate'][1]]], axis=1)
    side = jnp.concatenate([x[:, o['su'][0]:o['su'][1]], x[:, o['glr'][0]:o['glr'][1]], x[:, o['da'][0]:o['db'][1]],
                            jnp.zeros((x.shape[0], LANE - 2 * GLA_LR - 4 * GDN_H), x.dtype)], axis=1)
    m_ref[0] = main.astype(BF16)
    s_ref[0] = side.astype(BF16)


def _w_in_offsets():
    offs, pos = {}, 0
    for name, width in (('gq', GLA_QK), ('gk', GLA_QK), ('gv', GLA_W), ('glr', 2 * GLA_LR), ('ggate', GLA_W),
                        ('su', S5_W), ('sgate', S5_W), ('dqkv', 3 * GDN_W), ('da', 2 * GDN_H),
                        ('db', 2 * GDN_H), ('dgate', GDN_W)):
        offs[name] = (pos, pos + width)
        pos += width
    return offs


_W_IN_OFFS = _w_in_offsets()


def _wprep(w_in):
    depth, d_model, in_dim = w_in.shape
    rows = 256
    return pl.pallas_call(
        _wprep_kernel,
        out_shape=(jax.ShapeDtypeStruct((depth, d_model, MAIN_W), BF16),
                   jax.ShapeDtypeStruct((depth, d_model, SIDE_W), BF16)),
        grid=(depth, d_model // rows),
        in_specs=[pl.BlockSpec((1, rows, in_dim), lambda l, i: (l, i, 0))],
        out_specs=(pl.BlockSpec((1, rows, MAIN_W), lambda l, i: (l, i, 0)),
                   pl.BlockSpec((1, rows, SIDE_W), lambda l, i: (l, i, 0))),
        compiler_params=_params(("arbitrary", "arbitrary")),
        name="wprep",
    )(w_in)


def _inproj_kernel(xc_ref, xs_ref, ada_ref, g_ref, wm_ref, ws_ref, om_ref, os_ref, *, ctx_tiles):
    x = jnp.where(pl.program_id(0) < ctx_tiles, xc_ref[...], xs_ref[...])
    nrm = x * lax.rsqrt(jnp.mean(x * x, axis=-1, keepdims=True) + EPS) * g_ref[...]
    ada = ada_ref[0]
    shift = ada[:, 0:D_MODEL]
    scale = ada[:, D_MODEL:2 * D_MODEL]
    h = (nrm * (1.0 + scale) + shift).astype(BF16)
    om_ref[...] = _dot(h, wm_ref[...]).astype(BF16)
    os_ref[...] = _dot(h, ws_ref[...])


def _ada_row(i, ctx_tiles, tiles_per_latent):
    return jnp.where(i < ctx_tiles, 0, 1 + (i - ctx_tiles) // tiles_per_latent)


def _ctx_tile(i, ctx_tiles):
    return jnp.minimum(i, ctx_tiles - 1)


def _lat_tile(i, ctx_tiles):
    return jnp.maximum(i - ctx_tiles, 0)


def _inproj(x_c, x_s, ada, norm_pre, w_main, w_small, layer, ctx_tiles, tiles_per_latent):
    nt = x_c.shape[0] + x_s.shape[0]
    row = functools.partial(_ada_row, ctx_tiles=ctx_tiles, tiles_per_latent=tiles_per_latent)
    return pl.pallas_call(
        functools.partial(_inproj_kernel, ctx_tiles=ctx_tiles),
        out_shape=(jax.ShapeDtypeStruct((nt, MAIN_W), BF16), jax.ShapeDtypeStruct((nt, SIDE_W), F32)),
        grid=(nt // TOKEN_TILE,),
        in_specs=[pl.BlockSpec((TOKEN_TILE, D_MODEL), lambda i: (_ctx_tile(i, ctx_tiles), 0)),
                  pl.BlockSpec((TOKEN_TILE, D_MODEL), lambda i: (_lat_tile(i, ctx_tiles), 0)),
                  pl.BlockSpec((None, 1, 1, 3 * D_MODEL), lambda i: (layer, row(i), 0, 0)),
                  pl.BlockSpec((None, 1, D_MODEL), lambda i: (layer, 0, 0)),
                  pl.BlockSpec((None, D_MODEL, MAIN_W), lambda i: (layer, 0, 0)),
                  pl.BlockSpec((None, D_MODEL, SIDE_W), lambda i: (layer, 0, 0))],
        out_specs=(pl.BlockSpec((TOKEN_TILE, MAIN_W), lambda i: (i, 0)),
                   pl.BlockSpec((TOKEN_TILE, SIDE_W), lambda i: (i, 0))),
        compiler_params=_params(("arbitrary",)),
        name="inproj",
    )(x_c, x_s, ada, norm_pre, w_main, w_small)


def _tri_mask(n, reverse, strict=False):
    r = lax.broadcasted_iota(jnp.int32, (n, n), 0)
    c = lax.broadcasted_iota(jnp.int32, (n, n), 1)
    if reverse:
        return (r < c) if strict else (r <= c)
    return (r > c) if strict else (r >= c)


def _eye_mask(n):
    return lax.broadcasted_iota(jnp.int32, (n, n), 0) == lax.broadcasted_iota(jnp.int32, (n, n), 1)


GLA_CPI = 4


def _state_refs(rest, has_state, emit_state, aliased):
    rest = list(rest)
    s0_ref = rest.pop(0) if has_state else None
    if aliased:
        rest.pop(0)
    o_ref = rest.pop(0)
    sf_ref = rest.pop(0) if emit_state else None
    return s0_ref, o_ref, sf_ref, rest


def _gla_kernel(q_ref, k_ref, v_ref, sm_ref, gw_ref, gb_ref, *rest, n_chunks, has_state, emit_state, aliased):
    s0_ref, o_ref, sf_ref, (g_s, ob_s, st_ref) = _state_refs(rest, has_state, emit_state, aliased)
    for d in range(2):
        if has_state:
            s0 = jnp.concatenate([s0_ref[0, d, 0], s0_ref[0, d, 1]], axis=0)
            st_ref[d] = s0.T
        else:
            st_ref[d] = jnp.zeros((GLA_DV, LANE), F32)
        z = _dot3(sm_ref[...], gw_ref[d]) + gb_ref[d]
        g_s[d] = -_softplus(-z) * (1.0 / GLA_TAU)

    lane = lax.broadcasted_iota(jnp.int32, (CHUNK, LANE), 1)
    head_mask = [lane < GLA_DK, lane >= GLA_DK]
    scale = GLA_DK ** -0.5
    causal = [_tri_mask(CHUNK, False), _tri_mask(CHUNK, True)]
    causal_bf = [m.astype(BF16) for m in causal]

    def chunk_step(it, carry):
        jd = [(j, d) for j in range(GLA_CPI) for d in range(2)]
        jdh = [(j, d, h) for (j, d) in jd for h in range(2)]
        cc = {(j, d): (it * GLA_CPI + j) if d == 0 else n_chunks - 1 - (it * GLA_CPI + j) for (j, d) in jd}
        rows = {u: pl.ds(pl.multiple_of(cc[u] * CHUNK, CHUNK), CHUNK) for u in jd}
        q = {u: q_ref[rows[u], :].astype(F32) for u in jd}
        k = {u: k_ref[rows[u], :].astype(F32) for u in jd}
        v = {u: v_ref[rows[u], :] for u in jd}
        g = {(j, d): g_s[d, rows[j, d], :] for (j, d) in jd}
        g_hi = {u: g[u].astype(BF16) for u in jd}
        g_lo = {u: (g[u] - g_hi[u].astype(F32)).astype(BF16) for u in jd}
        b = {u: _dot(causal_bf[u[1]], g_hi[u]) + _dot(causal_bf[u[1]], g_lo[u]) for u in jd}
        b_last = {(j, d): b[j, d][CHUNK - 1:CHUNK, :] if d == 0 else b[j, d][0:1, :] for (j, d) in jd}
        qe = {u: q[u] * jnp.exp(b[u]) * scale for u in jd}
        ke = {u: (k[u] * jnp.exp(-b[u])).astype(BF16) for u in jd}
        kd = {u: k[u] * jnp.exp(b_last[u] - b[u]) for u in jd}
        qh = {(j, d, h): jnp.where(head_mask[h], qe[j, d], 0.0).astype(BF16) for (j, d, h) in jdh}
        vh = {(j, d, h): v[j, d][:, h * GLA_DV:(h + 1) * GLA_DV].astype(BF16) for (j, d, h) in jdh}
        sc = {(j, d, h): jnp.where(causal[d], _dot_nt(qh[j, d, h], ke[j, d]), 0.0).astype(BF16)
              for (j, d, h) in jdh}
        upd = {(j, d, h): _dot_tn(vh[j, d, h], jnp.where(head_mask[h], kd[j, d], 0.0).astype(BF16))
               for (j, d, h) in jdh}
        o_intra = {u: _dot(sc[u], vh[u]) for u in jdh}
        for j in range(GLA_CPI):
            st_bf = {d: st_ref[d].astype(BF16) for d in range(2)}
            for d in range(2):
                for h in range(2):
                    oh = o_intra[j, d, h] + _dot_nt(qh[j, d, h], st_bf[d])
                    if d == 0:
                        o_ref[rows[j, d], h * GLA_DV:(h + 1) * GLA_DV] = oh
                    else:
                        ob_s[rows[j, d], h * GLA_DV:(h + 1) * GLA_DV] = oh
                st_ref[d] = st_ref[d] * jnp.exp(b_last[j, d]) + upd[j, d, 0] + upd[j, d, 1]
        return carry

    lax.fori_loop(0, n_chunks // GLA_CPI, chunk_step, 0)
    o_ref[...] += ob_s[...]
    if emit_state:
        for d in range(2):
            s = st_ref[d].T
            for h in range(2):
                sf_ref[0, d, h] = s[h * GLA_DK:(h + 1) * GLA_DK, :]


def _state_plumbing(layer, s0, sf_acc, nseq, heads, hps, dk, dv, n_lead):
    blk = (1, None, 2, hps, dk, dv)
    imap = lambda b, p, *_: (b, layer, 0, p, 0, 0)
    in_specs, operands, out_specs, out_shapes, aliases = [], [], [], [], {}
    if s0 is not None:
        in_specs.append(pl.BlockSpec(blk, imap))
        operands.append(s0)
    emit = sf_acc is not False
    if emit:
        if sf_acc is not None:
            aliases[n_lead + len(operands)] = 1
            in_specs.append(pl.BlockSpec(memory_space=pl.ANY))
            operands.append(sf_acc)
        out_specs.append(pl.BlockSpec(blk, imap))
        out_shapes.append(jax.ShapeDtypeStruct((nseq, DEPTH, 2, heads, dk, dv), F32))
    flags = dict(has_state=s0 is not None, emit_state=emit, aliased=emit and sf_acc is not None)
    return in_specs, operands, out_specs, out_shapes, aliases, flags


def _gla(pm, ps, gw_pad, gb, layer, s0, sf_acc, *, nseq, seq_len, row_block0):
    st_in, st_ops, st_out, st_shapes, aliases, flags = _state_plumbing(
        layer, s0, sf_acc, nseq, GLA_H, 2, GLA_DK, GLA_DV, n_lead=6)
    kern = functools.partial(_gla_kernel, n_chunks=seq_len // CHUNK, **flags)
    return pl.pallas_call(
        kern,
        out_shape=[jax.ShapeDtypeStruct((nseq * seq_len, GLA_W), F32)] + st_shapes,
        grid=(nseq, 2),
        in_specs=[pl.BlockSpec((seq_len, LANE), lambda b, p: (row_block0 + b, COL_GQ + p)),
                  pl.BlockSpec((seq_len, LANE), lambda b, p: (row_block0 + b, COL_GK + p)),
                  pl.BlockSpec((seq_len, 2 * LANE), lambda b, p: (row_block0 + b, COL_GV // 2 + p)),
                  pl.BlockSpec((seq_len, LANE), lambda b, p: (row_block0 + b, COL_SM)),
                  pl.BlockSpec((None, 2, LANE, LANE), lambda b, p: (layer, 0, 0, p)),
                  pl.BlockSpec((None, 2, 1, LANE), lambda b, p: (layer, 0, 0, p))] + st_in,
        out_specs=[pl.BlockSpec((seq_len, 2 * LANE), lambda b, p: (b, p))] + st_out,
        scratch_shapes=[pltpu.VMEM((2, seq_len, LANE), F32), pltpu.VMEM((seq_len, 2 * LANE), F32),
                        pltpu.VMEM((2, GLA_DV, LANE), F32)],
        input_output_aliases=aliases,
        compiler_params=_params(("arbitrary", "arbitrary")),
        name="gla",
    )(pm, pm, pm, ps, gw_pad, gb, *st_ops)


GDN_HPS = 4
GDN_CPI = 4
assert GDN_CPI % 2 == 0 and 2 * CHUNK == LANE


def _block_mask(n, s, reverse):
    r = lax.broadcasted_iota(jnp.int32, (n, n), 0)
    c = lax.broadcasted_iota(jnp.int32, (n, n), 1)
    if reverse:
        r, c = c, r
    sh = s.bit_length() - 1
    same_pair = (r >> (sh + 1)) == (c >> (sh + 1))
    return same_pair & (((r >> sh) & 1) == 1) & (((c >> sh) & 1) == 0)


def _gdn_kernel(ab_ref, q_ref, k_ref, v_ref, sm_ref, cwq_ref, cwk_ref, cwv_ref, *rest,
                layer, n_chunks, period, has_state, emit_state, aliased):
    s0_ref, o_ref, sf_ref, scratch = _state_refs(rest, has_state, emit_state, aliased)
    q_s, k_s, v_s, g_s, b_s, u_s, wq_s, a_s, kd_s, dl_s, ob_s, st_ref = scratch
    hp = pl.program_id(1)
    seq_len = n_chunks * CHUNK
    row = lax.broadcasted_iota(jnp.int32, (seq_len, LANE), 0)
    first = (row % period) == 0
    last = (row % period) == period - 1

    def conv_silu(x, w):
        xp = jnp.where(first, 0.0, pltpu.roll(x, 1, 0))
        xn = jnp.where(last, 0.0, pltpu.roll(x, seq_len - 1, 0))
        return _silu(xp * w[0:1, :] + x * w[1:2, :] + xn * w[2:3, :])

    def l2norm(x):
        return x * lax.rsqrt(jnp.sum(x * x, axis=-1, keepdims=True) + EPS)

    lane1 = lax.broadcasted_iota(jnp.int32, (1, LANE), 1)
    a_log = jnp.zeros((1, LANE), F32)
    dt_bias = jnp.zeros((1, LANE), F32)
    for d in range(2):
        for hh in range(GDN_H):
            a_log = jnp.where(lane1 == SM_DA + d * GDN_H + hh, ab_ref[layer, d, hh], a_log)
            dt_bias = jnp.where(lane1 == SM_DA + d * GDN_H + hh, ab_ref[layer, d, GDN_H + hh], dt_bias)
    sm = sm_ref[...]
    g_all = -jnp.exp(a_log) * _softplus(sm + dt_bias)
    b_all = jax.nn.sigmoid(sm)
    lane = lax.broadcasted_iota(jnp.int32, (seq_len, LANE), 1)
    in_chunk = row % CHUNK

    def chunk_cumsum(x, reverse):
        sh = 1
        while sh < CHUNK:
            if reverse:
                x = x + jnp.where(in_chunk < CHUNK - sh, pltpu.roll(x, seq_len - sh, 0), 0.0)
            else:
                x = x + jnp.where(in_chunk >= sh, pltpu.roll(x, sh, 0), 0.0)
            sh *= 2
        return x

    gc_all = [chunk_cumsum(g_all, False), chunk_cumsum(g_all, True)]

    def lane_bcast(x, j):
        col = jnp.sum(jnp.where(lane == j, x, 0.0), axis=1, keepdims=True)
        return jnp.broadcast_to(col, (seq_len, LANE))

    for h in range(GDN_HPS):
        hd = hp * GDN_HPS + h
        lanes = slice(h * LANE, (h + 1) * LANE)
        q_s[h] = l2norm(conv_silu(q_ref[:, lanes].astype(F32), cwq_ref[:, lanes])) * GDN_DK ** -0.5
        k_s[h] = l2norm(conv_silu(k_ref[:, lanes].astype(F32), cwk_ref[:, lanes]))
        v_s[h] = conv_silu(v_ref[:, lanes].astype(F32), cwv_ref[:, lanes])
        for d in range(2):
            g_s[h, d] = lane_bcast(gc_all[d], SM_DA + d * GDN_H + hd)
            b_s[h, d] = lane_bcast(b_all, SM_DB + d * GDN_H + hd)
            if has_state:
                st_ref[h, d] = s0_ref[0, d, h]
            else:
                st_ref[h, d] = jnp.zeros((GDN_DK, GDN_DV), F32)

    causal = [_tri_mask(CHUNK, False), _tri_mask(CHUNK, True)]
    strict_f = [_tri_mask(CHUNK, d == 1, strict=True).astype(F32) for d in range(2)]
    level_masks = [[_block_mask(CHUNK, 1 << j, d == 1).astype(F32) for j in range(CHUNK.bit_length() - 1)]
                   for d in range(2)]

    eye_f = _eye_mask(CHUNK).astype(F32)

    def phase_a(it, carry):
        cs = [it * GDN_CPI + j for j in range(GDN_CPI)]
        rows = [pl.ds(pl.multiple_of(c * CHUNK, CHUNK), CHUNK) for c in cs]
        rows2 = [pl.ds(pl.multiple_of(c * 2 * CHUNK, 2 * CHUNK), CHUNK) for c in cs]
        rows2b = [pl.ds(pl.multiple_of(c * 2 * CHUNK + CHUNK, CHUNK), CHUNK) for c in cs]
        pairs = [(j, h) for j in range(GDN_CPI) for h in range(GDN_HPS)]
        units = [(j, h, d) for (j, h) in pairs for d in range(2)]
        q = {(j, h): q_s[h, rows[j], :] for (j, h) in pairs}
        k = {(j, h): k_s[h, rows[j], :] for (j, h) in pairs}
        v = {(j, h): v_s[h, rows[j], :] for (j, h) in pairs}
        k_bf = {p: k[p].astype(BF16) for p in pairs}
        beta = {(j, h, d): b_s[h, d, rows[j], :] for (j, h, d) in units}
        gc = {(j, h, d): g_s[h, d, rows[j], :] for (j, h, d) in units}
        gc_t = {}
        for h in range(GDN_HPS):
            for d in range(2):
                for j0 in range(0, GDN_CPI, 2):
                    blk_t = jnp.concatenate([gc[j0, h, d], gc[j0 + 1, h, d]], axis=0).T
                    for j in range(2):
                        gc_t[j0 + j, h, d] = blk_t[0:CHUNK, j * CHUNK:(j + 1) * CHUNK]
        qk = {p: _dot_nt(q[p].astype(BF16), k_bf[p]) for p in pairs}
        kk = {p: _dot_nt(k_bf[p], k_bf[p]) for p in pairs}
        decay, m, t = {}, {}, {}
        for u in units:
            j, h, d = u
            diff = gc[u][:, :CHUNK] - gc_t[u]
            decay[u] = jnp.where(causal[d], jnp.exp(jnp.where(causal[d], diff, 0.0)), 0.0)
            m[u] = kk[j, h] * beta[u][:, :CHUNK] * decay[u] * strict_f[d]
            t[u] = eye_f - m[u] * level_masks[d][0]
        for lvl in range(1, len(level_masks[0])):
            t_bf = {u: t[u].astype(BF16) for u in units}
            p1 = {u: _dot(t_bf[u], (m[u] * level_masks[u[2]][lvl]).astype(BF16)).astype(BF16) for u in units}
            t = {u: t[u] - _dot(p1[u], t_bf[u]) for u in units}
        egc = {u: jnp.exp(gc[u]) for u in units}
        rhs = {(j, h, d): jnp.concatenate([v[j, h] * beta[j, h, d], k[j, h] * (beta[j, h, d] * egc[j, h, d])],
                                          axis=1).astype(BF16) for (j, h, d) in units}
        uw = {u: _dot(t[u].astype(BF16), rhs[u]) for u in units}
        for u in units:
            j, h, d = u
            gc_last = gc[u][CHUNK - 1:CHUNK, :] if d == 0 else gc[u][0:1, :]
            u_s[h, d, rows[j], :] = uw[u][:, 0:GDN_DV]
            wq_s[h, d, rows2[j], :] = uw[u][:, GDN_DV:GDN_DV + GDN_DK].astype(BF16)
            wq_s[h, d, rows2b[j], :] = (q[j, h] * egc[u]).astype(BF16)
            a_s[h, d, rows[j], :] = (qk[j, h] * decay[u]).astype(BF16)
            kd_s[h, d, rows[j], :] = (k[j, h] * jnp.exp(gc_last - gc[u])).astype(BF16)
            dl_s[h, d, cs[j]] = jnp.broadcast_to(jnp.exp(gc_last), (8, LANE))
        return carry

    lax.fori_loop(0, n_chunks // GDN_CPI, phase_a, 0)

    def phase_b(i, carry):
        chains = [(h, d) for h in range(GDN_HPS) for d in range(2)]
        cc = {0: i, 1: n_chunks - 1 - i}
        rows = {d: pl.ds(pl.multiple_of(cc[d] * CHUNK, CHUNK), CHUNK) for d in range(2)}
        rows2 = {d: pl.ds(pl.multiple_of(cc[d] * 2 * CHUNK, 2 * CHUNK), 2 * CHUNK) for d in range(2)}
        s = {hd: st_ref[hd[0], hd[1]] for hd in chains}
        ws = {(h, d): _dot(wq_s[h, d, rows2[d], :], s[h, d].astype(BF16)) for (h, d) in chains}
        v_new = {(h, d): (u_s[h, d, rows[d], :] - ws[h, d][0:CHUNK, :]).astype(BF16) for (h, d) in chains}
        o = {(h, d): ws[h, d][CHUNK:2 * CHUNK, :] + _dot(a_s[h, d, rows[d], :], v_new[h, d]) for (h, d) in chains}
        upd = {(h, d): _dot_tn(kd_s[h, d, rows[d], :], v_new[h, d]) for (h, d) in chains}
        for (h, d) in chains:
            lanes = slice(h * LANE, (h + 1) * LANE)
            if d == 0:
                o_ref[rows[d], lanes] = o[h, d]
            else:
                ob_s[rows[d], lanes] = o[h, d]
            st_ref[h, d] = dl_s[h, d, cc[d]][0:1, :] * s[h, d] + upd[h, d]
        return carry

    lax.fori_loop(0, n_chunks, phase_b, 0)
    o_ref[...] += ob_s[...]
    if emit_state:
        for h in range(GDN_HPS):
            for d in range(2):
                sf_ref[0, d, h] = st_ref[h, d]


def _gdn(pm, ps, ab, conv_w, layer, s0, sf_acc, *, nseq, seq_len, row_block0, period):
    hps = GDN_HPS
    st_in, st_ops, st_out, st_shapes, aliases, flags = _state_plumbing(
        layer, s0, sf_acc, nseq, GDN_H, hps, GDN_DK, GDN_DV, n_lead=8)
    n_chunks = seq_len // CHUNK
    kern = functools.partial(_gdn_kernel, layer=layer, n_chunks=n_chunks, period=period, **flags)
    wide = hps * LANE
    npair = GDN_H // hps
    grid_spec = pltpu.PrefetchScalarGridSpec(
        num_scalar_prefetch=1,
        grid=(nseq, npair),
        in_specs=[pl.BlockSpec((seq_len, wide), lambda b, p, ab: (row_block0 + b, COL_DQ // hps + p),
                               pipeline_mode=pl.Buffered(1)),
                  pl.BlockSpec((seq_len, wide), lambda b, p, ab: (row_block0 + b, COL_DK // hps + p),
                               pipeline_mode=pl.Buffered(1)),
                  pl.BlockSpec((seq_len, wide), lambda b, p, ab: (row_block0 + b, COL_DV // hps + p),
                               pipeline_mode=pl.Buffered(1)),
                  pl.BlockSpec((seq_len, LANE), lambda b, p, ab: (row_block0 + b, COL_SM)),
                  pl.BlockSpec((None, 3, wide), lambda b, p, ab: (layer, 0, p)),
                  pl.BlockSpec((None, 3, wide), lambda b, p, ab: (layer, 0, npair + p)),
                  pl.BlockSpec((None, 3, wide), lambda b, p, ab: (layer, 0, 2 * npair + p))] + st_in,
        out_specs=[pl.BlockSpec((seq_len, wide), lambda b, p, ab: (b, p))] + st_out,
        scratch_shapes=[pltpu.VMEM((hps, seq_len, LANE), F32),
                        pltpu.VMEM((hps, seq_len, LANE), F32),
                        pltpu.VMEM((hps, seq_len, LANE), F32),
                        pltpu.VMEM((hps, 2, seq_len, LANE), F32),
                        pltpu.VMEM((hps, 2, seq_len, LANE), F32),
                        pltpu.VMEM((hps, 2, seq_len, GDN_DV), F32),
                        pltpu.VMEM((hps, 2, 2 * seq_len, GDN_DK), BF16),
                        pltpu.VMEM((hps, 2, seq_len, CHUNK), BF16),
                        pltpu.VMEM((hps, 2, seq_len, GDN_DK), BF16),
                        pltpu.VMEM((hps, 2, n_chunks, 8, LANE), F32),
                        pltpu.VMEM((seq_len, wide), F32),
                        pltpu.VMEM((hps, 2, GDN_DK, GDN_DV), F32)],
    )
    return pl.pallas_call(
        kern,
        out_shape=[jax.ShapeDtypeStruct((nseq * seq_len, GDN_W), F32)] + st_shapes,
        grid_spec=grid_spec,
        input_output_aliases=aliases,
        compiler_params=_params(("arbitrary", "arbitrary")),
        name="gdn",
    )(ab, pm, pm, pm, ps, conv_w, conv_w, conv_w, *st_ops)


S5_GPB = LANE // S5_GH
S5_XROWS = 4096


def _lane_block_transpose(arrs, lane_grp):
    arrs = list(arrs)
    k = S5_GPB // 2
    while k:
        high = (lane_grp & k) != 0
        for i in range(S5_GPB):
            if not i & k:
                lo, hi = arrs[i], arrs[i + k]
                arrs[i] = jnp.where(high, pltpu.roll(hi, k * S5_GH, 1), lo)
                arrs[i + k] = jnp.where(high, hi, pltpu.roll(lo, LANE - k * S5_GH, 1))
        k //= 2
    return arrs


def _s5_kernel(*refs, n_chunks, nseq, latent):
    n_x = len(refs) - 10
    x_refs = refs[:n_x]
    mi_ref, wst_ref, wout_ref, lam_ref, h0_ref, y_ref, fin_ref, u_s, e_s, y_s = refs[n_x:]
    seq_len = n_chunks * S5_T
    seq_per_x = S5_XROWS // seq_len
    lane_grp = lax.broadcasted_iota(jnp.int32, (n_chunks, LANE), 1) >> 4

    def slab_rows(base, t):
        if latent:
            return pl.ds(pl.multiple_of(base + t * GRID_W, GRID_W), n_chunks)
        return pl.ds(base + t, n_chunks, stride=S5_T)

    def relayout_in(x_ref, b0):
        def body(bl, carry):
            base = bl * seq_len
            b = b0 + bl
            for lt in range(2):
                slabs = [x_ref[slab_rows(base, S5_GPB * lt + tt), :] for tt in range(S5_GPB)]
                for gi, tile in enumerate(_lane_block_transpose(slabs, lane_grp)):
                    u_s[gi, lt, pl.ds(b, n_chunks, stride=nseq), :] = tile
            return carry
        lax.fori_loop(0, seq_per_x, body, 0)

    for i, x_ref in enumerate(x_refs):
        relayout_in(x_ref, i * seq_per_x)

    half = 2 * S5_P
    u_bf = [jnp.concatenate([u_s[gi, 0], u_s[gi, 1]], axis=1).astype(BF16) for gi in range(S5_GPB)]
    for gi in range(S5_GPB):
        e = _dot(u_bf[gi], wst_ref[gi])
        e_s[gi, 0] = e[:, 0:half]
        e_s[gi, 1] = e[:, half:2 * half]
    for gi in range(S5_GPB):
        y = _dot(u_bf[gi], mi_ref[gi])
        y_s[gi, 0] = y[:, 0:LANE]
        y_s[gi, 1] = y[:, LANE:2 * LANE]

    lam = [lam_ref[gi] for gi in range(S5_GPB)]

    def step(c, carry):
        rf = pl.ds(pl.multiple_of(c * nseq, nseq), nseq)
        rb = pl.ds(pl.multiple_of((n_chunks - 1 - c) * nseq, nseq), nseq)
        new = []
        for gi in range(S5_GPB):
            xf, xb = carry[2 * gi], carry[2 * gi + 1]
            u_s[gi, 0, rf, :] = xf
            u_s[gi, 1, rb, :] = xb
            l = lam[gi]
            new.append(xf * l[0:1, :] + pltpu.roll(xf, S5_P, 1) * l[1:2, :] + e_s[gi, 0, rf, :])
            new.append(xb * l[2:3, :] + pltpu.roll(xb, S5_P, 1) * l[3:4, :] + e_s[gi, 1, rb, :])
        return tuple(new)

    init = []
    for gi in range(S5_GPB):
        h0 = h0_ref[gi]
        init += [h0[:, 0:half], h0[:, half:2 * half]]
    fin = lax.fori_loop(0, n_chunks, step, tuple(init))
    for gi in range(S5_GPB):
        fin_ref[gi, :, 0:half] = fin[2 * gi]
        fin_ref[gi, :, half:2 * half] = fin[2 * gi + 1]
        x_in = jnp.concatenate([u_s[gi, 0], u_s[gi, 1]], axis=1).astype(BF16)
        y = _dot(x_in, wout_ref[gi])
        y_s[gi, 0] += y[:, 0:LANE]
        y_s[gi, 1] += y[:, LANE:2 * LANE]

    def relayout_out(b, carry):
        base = b * seq_len
        for lt in range(2):
            tiles = [y_s[gi, lt, pl.ds(b, n_chunks, stride=nseq), :] for gi in range(S5_GPB)]
            for tt, slab in enumerate(_lane_block_transpose(tiles, lane_grp)):
                y_ref[slab_rows(base, S5_GPB * lt + tt), :] = slab
        return carry

    lax.fori_loop(0, nseq, relayout_out, 0)


def _s5(ps, m_intra, w_st, w_out, lam_t, h0, layer, h0_layer, *, nseq, n_chunks, row_block0, latent):
    rows = n_chunks * nseq
    n_tok = rows * S5_T
    n_x = n_tok // S5_XROWS
    kern = functools.partial(_s5_kernel, n_chunks=n_chunks, nseq=nseq, latent=latent)
    wspec = pl.BlockSpec((None, S5_GPB, 2 * LANE, 2 * LANE), lambda j: (layer, j, 0, 0))
    x_specs = [pl.BlockSpec((S5_XROWS, LANE), functools.partial(lambda j, i: (row_block0 + i, COL_SU + j), i=i))
               for i in range(n_x)]
    plane = pltpu.VMEM((S5_GPB, 2, rows, LANE), F32)
    return pl.pallas_call(
        kern,
        out_shape=(jax.ShapeDtypeStruct((n_tok, S5_W), F32),
                   jax.ShapeDtypeStruct((S5_G, nseq, 2 * LANE), F32)),
        grid=(S5_G // S5_GPB,),
        in_specs=x_specs + [wspec, wspec, wspec,
                            pl.BlockSpec((None, S5_GPB, 4, LANE), lambda j: (layer, j, 0, 0)),
                            pl.BlockSpec((None, S5_GPB, nseq, 2 * LANE), lambda j: (h0_layer, j, 0, 0))],
        out_specs=(pl.BlockSpec((n_tok, LANE), lambda j: (0, j)),
                   pl.BlockSpec((S5_GPB, nseq, 2 * LANE), lambda j: (j, 0, 0))),
        scratch_shapes=[plane, plane, plane],
        compiler_params=_params(("arbitrary",)),
        name="s5",
    )(*([ps] * n_x), m_intra, w_st, w_out, lam_t, h0)


def _cmul(ar, ai, br, bi):
    return ar * br - ai * bi, ar * bi + ai * br


def _s5_weights(lam_re, lam_im, log_dt, b_re, b_im, c_re, c_im):
    t = S5_T
    dt = jnp.exp(log_dt)[..., None]
    tau = jnp.arange(t + 1, dtype=F32)[:, None, None, None]
    mag = jnp.exp(tau * (lam_re * dt)[None])
    ang = tau * (lam_im * dt)[None]
    pw_re, pw_im = mag * jnp.cos(ang), mag * jnp.sin(ang)
    nr, ni = pw_re[1] - 1.0, pw_im[1]
    den = lam_re * lam_re + lam_im * lam_im
    fr, fi = (nr * lam_re + ni * lam_im) / den, (ni * lam_re - nr * lam_im) / den
    bb_re, bb_im = _cmul(fr[..., None], fi[..., None], b_re[None], b_im[None])

    bt_re, bt_im = jnp.swapaxes(bb_re, -1, -2), jnp.swapaxes(bb_im, -1, -2)
    ct_re, ct_im = jnp.swapaxes(c_re, -1, -2), jnp.swapaxes(c_im, -1, -2)
    pwt_re, pwt_im = jnp.moveaxis(pw_re, 0, -1), jnp.moveaxis(pw_im, 0, -1)
    cl_re, cl_im = _cmul(pwt_re[..., None], pwt_im[..., None], ct_re[:, :, :, None, :], ct_im[:, :, :, None, :])

    def lanes(x):
        return x.reshape(S5_G, S5_P, t * S5_GH)

    def k_rows(d, taus):
        return (jnp.einsum('gip,gpx->gix', bt_re[d], lanes(cl_re[d][:, :, taus]), precision=HI)
                - jnp.einsum('gip,gpx->gix', bt_im[d], lanes(cl_im[d][:, :, taus]), precision=HI))

    width = t * S5_GH
    pad = (t - 1) * S5_GH
    kf = jnp.pad(k_rows(0, slice(0, t)), ((0, 0), (0, 0), (pad, 0)))
    kb = jnp.pad(k_rows(1, slice(t - 1, None, -1)), ((0, 0), (0, 0), (0, pad)))
    m_intra = jnp.stack([kf[:, :, pad - s * S5_GH:pad - s * S5_GH + width]
                         + kb[:, :, (t - 1 - s) * S5_GH:(t - 1 - s) * S5_GH + width] for s in range(t)], axis=1)
    m_intra = m_intra.reshape(S5_G, width, width)

    def st(d, taus):
        return _cmul(jnp.moveaxis(pw_re[taus, d], 0, 1)[:, :, None, :], jnp.moveaxis(pw_im[taus, d], 0, 1)[:, :, None, :],
                     bt_re[d][:, None], bt_im[d][:, None])
    w_st = jnp.concatenate(st(0, slice(t - 1, None, -1)) + st(1, slice(0, t)), axis=-1)
    w_st = w_st.reshape(S5_G, width, 4 * S5_P)

    w_out = jnp.concatenate([lanes(cl_re[0][:, :, 1:t + 1]), -lanes(cl_im[0][:, :, 1:t + 1]),
                             lanes(cl_re[1][:, :, t:0:-1]), -lanes(cl_im[1][:, :, t:0:-1])], axis=1)


    lt_re, lt_im = pw_re[t], pw_im[t]
    lam_t = jnp.stack([jnp.concatenate([lt_re[0], lt_re[0]], -1), jnp.concatenate([-lt_im[0], lt_im[0]], -1),
                       jnp.concatenate([lt_re[1], lt_re[1]], -1), jnp.concatenate([-lt_im[1], lt_im[1]], -1)],
                      axis=1)
    return m_intra.astype(BF16), w_st.astype(BF16), w_out.astype(BF16), lam_t


def _gelu_tanh(x):
    return 0.5 * x * (1.0 + jnp.tanh(math.sqrt(2.0 / math.pi) * (x + 0.044715 * (x * x * x))))


def _out_kernel(xc_ref, xs_ref, ada_ref, ogc_ref, ogs_ref, gg_ref, ysc_ref, yss_ref, su_ref, sg_ref,
                odc_ref, ods_ref, dg_ref, gn_ref, dn_ref, sd_ref, gw_ref, gb_ref, wo_ref, np_ref,
                oc_ref, os_ref, *, ctx_tiles):
    is_ctx = pl.program_id(0) < ctx_tiles

    def pick(c_ref, s_ref):
        return jnp.where(is_ctx, c_ref[...], s_ref[...])

    def head_norm(o, g):
        parts = []
        for h in range(o.shape[1] // LANE):
            oh = o[:, h * LANE:(h + 1) * LANE]
            parts.append(oh * lax.rsqrt(jnp.mean(oh * oh, axis=-1, keepdims=True) + EPS) * g)
        return jnp.concatenate(parts, axis=1)

    o_gla = head_norm(pick(ogc_ref, ogs_ref), gn_ref[...]) * _silu(gg_ref[...].astype(F32))
    y = _gelu_tanh(pick(ysc_ref, yss_ref) + sd_ref[...] * su_ref[...])
    y = y * jax.nn.sigmoid(_dot(y.astype(BF16), gw_ref[...]) + gb_ref[...])
    o_s5 = y * _silu(sg_ref[...].astype(F32))
    o_gdn = head_norm(pick(odc_ref, ods_ref), dn_ref[...]) * _silu(dg_ref[...].astype(F32))
    out = (_dot(o_gla.astype(BF16), wo_ref[0:GLA_W, :])
           + _dot(o_s5.astype(BF16), wo_ref[GLA_W:GLA_W + S5_W, :])
           + _dot(o_gdn.astype(BF16), wo_ref[GLA_W + S5_W:MIX_W, :]))
    r = out * lax.rsqrt(jnp.mean(out * out, axis=-1, keepdims=True) + EPS) * np_ref[...]
    gate = ada_ref[0][:, 2 * D_MODEL:3 * D_MODEL]
    x_new = pick(xc_ref, xs_ref) + gate * r

    @pl.when(is_ctx)
    def _():
        oc_ref[...] = x_new

    @pl.when(jnp.logical_not(is_ctx))
    def _():
        os_ref[...] = x_new


def _out(x_c, x_s, ada, pm, ps, og_c, og_s, y_c, y_s, od_c, od_s, gla_norm, gdn_norm, s5_d, glu_w, glu_b, w_out,
         norm_post, layer, ctx_tiles, tiles_per_latent):
    nt = x_c.shape[0] + x_s.shape[0]
    row = functools.partial(_ada_row, ctx_tiles=ctx_tiles, tiles_per_latent=tiles_per_latent)
    wide = 4 * LANE

    def tok(col_block):
        return pl.BlockSpec((TOKEN_TILE, wide), lambda i: (i, col_block))

    def ctx(width):
        return pl.BlockSpec((TOKEN_TILE, width), lambda i: (_ctx_tile(i, ctx_tiles), 0))

    def lat(width):
        return pl.BlockSpec((TOKEN_TILE, width), lambda i: (_lat_tile(i, ctx_tiles), 0))

    def full(shape):
        return pl.BlockSpec((None,) + shape, lambda i: (layer,) + (0,) * len(shape))

    return pl.pallas_call(
        functools.partial(_out_kernel, ctx_tiles=ctx_tiles),
        out_shape=(jax.ShapeDtypeStruct(x_c.shape, F32), jax.ShapeDtypeStruct(x_s.shape, F32)),
        grid=(nt // TOKEN_TILE,),
        in_specs=[ctx(D_MODEL), lat(D_MODEL),
                  pl.BlockSpec((None, 1, 1, 3 * D_MODEL), lambda i: (layer, row(i), 0, 0)),
                  ctx(wide), lat(wide), tok(COL_GGATE // 4),
                  ctx(wide), lat(wide), tok(COL_SU // 4), tok(COL_SGATE // 4),
                  ctx(wide), lat(wide), tok(COL_DGATE // 4),
                  full((1, LANE)), full((1, LANE)), full((1, S5_W)), full((S5_W, S5_W)), full((1, S5_W)),
                  full((MIX_W, D_MODEL)), full((1, D_MODEL))],
        out_specs=(ctx(D_MODEL), lat(D_MODEL)),
        compiler_params=_params(("arbitrary",)),
        name="outproj",
    )(x_c, x_s, ada, og_c, og_s, pm, y_c, y_s, ps, pm, od_c, od_s, pm, gla_norm, gdn_norm, s5_d, glu_w, glu_b,
      w_out, norm_post)


def kernel(x_prompt, x_sample, c, state_gla, state_s5_re, state_s5_im, state_gdn, c_ctx, norm_pre, norm_post, w_ada, b_ada, w_in, gla_gate_w, gla_gate_b, gla_norm, s5_lam_re, s5_lam_im, s5_log_dt, s5_b_re, s5_b_im, s5_c_re, s5_c_im, s5_d, s5_glu_w, s5_glu_b, gdn_conv, gdn_a_log, gdn_dt_bias, gdn_norm, w_out):
    bp, lp, _ = x_prompt.shape
    bs, ls, _ = x_sample.shape
    n_ctx = bp * lp
    assert lp % TOKEN_TILE == 0 and ls % TOKEN_TILE == 0 and n_ctx % ls == 0
    assert ls // GRID_W == S5_T and lp % S5_T == 0 and n_ctx % S5_XROWS == 0 and (bs * ls) % S5_XROWS == 0
    ctx_tiles = n_ctx // TOKEN_TILE
    tiles_per_latent = ls // TOKEN_TILE

    cond = jnp.concatenate([c_ctx[None].astype(F32), c.astype(F32)], axis=0)
    rows = -(-cond.shape[0] // 8) * 8
    cond = jnp.pad(cond, ((0, rows - cond.shape[0]), (0, 0)))
    ada = _adaln(cond, w_ada.astype(F32), b_ada.astype(F32)).reshape(DEPTH, rows, 1, 3 * D_MODEL)

    w_main, w_small = _wprep(w_in.astype(F32))
    w_out_bf = w_out.astype(BF16)
    glu_w_bf = s5_glu_w.astype(BF16)
    gw_pad = jnp.zeros((DEPTH, 2, LANE, GLA_QK), F32)
    for d in range(2):
        gw_pad = gw_pad.at[:, d, SM_GLR + d * GLA_LR:SM_GLR + (d + 1) * GLA_LR, :].set(gla_gate_w[:, d].astype(F32))
    gb = gla_gate_b.astype(F32).reshape(DEPTH, 2, 1, GLA_QK)
    gdn_ab = jnp.concatenate([gdn_a_log, gdn_dt_bias], axis=-1).astype(F32)
    conv_w = gdn_conv.astype(F32)
    m_intra, w_st, w_ro, lam_t = jax.vmap(_s5_weights)(*(p.astype(F32) for p in (
        s5_lam_re, s5_lam_im, s5_log_dt, s5_b_re, s5_b_im, s5_c_re, s5_c_im)))
    h0_c = jnp.zeros((1, S5_G, bp, 4 * S5_P), F32)
    sre, sim = state_s5_re.astype(F32), state_s5_im.astype(F32)
    h0_s = jnp.concatenate([sre[:, :, 0], sim[:, :, 0], sre[:, :, 1], sim[:, :, 1]], axis=-1).transpose(1, 2, 0, 3)
    vec = lambda p: p.astype(F32).reshape(DEPTH, 1, -1)
    norm_pre_v, norm_post_v, gla_norm_v, gdn_norm_v = vec(norm_pre), vec(norm_post), vec(gla_norm), vec(gdn_norm)
    s5_d_v, glu_b_v = vec(s5_d), vec(s5_glu_b)
    st_gla, st_gdn = state_gla.astype(F32), state_gdn.astype(F32)

    x_c = x_prompt.reshape(n_ctx, D_MODEL).astype(F32)
    x_s = x_sample.reshape(bs * ls, D_MODEL).astype(F32)
    new_gla, new_gdn, s5_states = None, None, []
    lat0 = n_ctx // ls
    for l in range(DEPTH):
        pm, ps = _inproj(x_c, x_s, ada, norm_pre_v, w_main, w_small, l, ctx_tiles, tiles_per_latent)
        og_c, new_gla = _gla(pm, ps, gw_pad, gb, l, None, new_gla, nseq=bp, seq_len=lp, row_block0=0)
        og_s, = _gla(pm, ps, gw_pad, gb, l, st_gla, False, nseq=bs, seq_len=ls, row_block0=lat0)
        od_c, new_gdn = _gdn(pm, ps, gdn_ab, conv_w, l, None, new_gdn, nseq=bp, seq_len=lp, row_block0=0, period=lp)
        od_s, = _gdn(pm, ps, gdn_ab, conv_w, l, st_gdn, False, nseq=bs, seq_len=ls, row_block0=lat0,
                     period=GRID_W)
        y_c, fin = _s5(ps, m_intra, w_st, w_ro, lam_t, h0_c, l, 0, nseq=bp, n_chunks=lp // S5_T, row_block0=0,
                       latent=False)
        y_s, _ = _s5(ps, m_intra, w_st, w_ro, lam_t, h0_s, l, l, nseq=bs, n_chunks=GRID_W,
                     row_block0=n_ctx // S5_XROWS, latent=True)
        x_c, x_s = _out(x_c, x_s, ada, pm, ps, og_c, og_s, y_c, y_s, od_c, od_s, gla_norm_v, gdn_norm_v, s5_d_v,
                        glu_w_bf, glu_b_v, w_out_bf, norm_post_v, l, ctx_tiles, tiles_per_latent)
        fin = fin.transpose(1, 0, 2).reshape(bp, S5_G, 2, 2, S5_P)
        s5_states.append(fin.transpose(0, 2, 3, 1, 4))

    dt = x_prompt.dtype
    s5_all = jnp.stack(s5_states, axis=1)
    y_prompt = x_c.reshape(bp, lp, D_MODEL).astype(dt)
    y_sample = x_s.reshape(bs, ls, D_MODEL).astype(x_sample.dtype)
    return (y_prompt, y_sample, new_gla.astype(dt), s5_all[:, :, :, 0].astype(dt), s5_all[:, :, :, 1].astype(dt),
            new_gdn.astype(dt))
```

```python
import functools
import math

import jax
import jax.numpy as jnp
from jax import lax
from jax.experimental import pallas as pl
from jax.experimental.pallas import tpu as pltpu

F32 = jnp.float32
BF16 = jnp.bfloat16
HI = lax.Precision.HIGHEST

D_MODEL = 1024
DEPTH = 4
GRID_W = 64
CHUNK = 64
EPS = 1e-6
GLA_H, GLA_DK, GLA_DV, GLA_LR, GLA_TAU = 4, 64, 128, 16, 16.0
GLA_QK, GLA_W = GLA_H * GLA_DK, GLA_H * GLA_DV
S5_GH, S5_W, S5_P = 16, 512, 64
S5_G = S5_W // S5_GH
S5_T = 16
GDN_H, GDN_DK, GDN_DV = 4, 128, 128
GDN_W = GDN_H * GDN_DV
MIX_W = GLA_W + S5_W + GDN_W

LANE = 128
TOKEN_TILE = 256
VMEM_LIMIT = 48 * 1024 * 1024

COL_GQ, COL_GK, COL_GV, COL_GGATE, COL_SGATE = 0, 2, 4, 8, 12
COL_DQ, COL_DK, COL_DV, COL_DGATE = 16, 20, 24, 28
MAIN_W = 32 * LANE
COL_SU, COL_SM = 0, 4
SIDE_W = 5 * LANE
SM_GLR, SM_DA, SM_DB = 0, 32, 40


def _dot(a, b, precision=None):
    return lax.dot_general(a, b, (((1,), (0,)), ((), ())), precision=precision,
                           preferred_element_type=F32)


def _dot_nt(a, b, precision=None):
    return lax.dot_general(a, b, (((1,), (1,)), ((), ())), precision=precision,
                           preferred_element_type=F32)


def _dot_tn(a, b, precision=None):
    return lax.dot_general(a, b, (((0,), (0,)), ((), ())), precision=precision,
                           preferred_element_type=F32)


def _split_bf16(x):
    hi = x.astype(BF16)
    return hi, (x - hi.astype(F32)).astype(BF16)


def _dot3(a, b):
    a_hi, a_lo = _split_bf16(a)
    b_hi, b_lo = _split_bf16(b)
    return _dot(a_hi, b_hi) + (_dot(a_hi, b_lo) + _dot(a_lo, b_hi))


def _silu(x):
    return x * jax.nn.sigmoid(x)


def _softplus(x):
    return jnp.maximum(x, 0.0) + jnp.log1p(jnp.exp(-jnp.abs(x)))


def _params(sem):
    return pltpu.CompilerParams(dimension_semantics=sem, vmem_limit_bytes=VMEM_LIMIT)


def _adaln_kernel(cond_ref, w_ref, b_ref, o_ref):
    c = cond_ref[...]
    o_ref[0] = _dot(_silu(c), w_ref[0], precision=HI) + b_ref[0]


def _adaln(cond, w_ada, b_ada):
    rows = cond.shape[0]
    nj = 3 * D_MODEL // 1024
    return pl.pallas_call(
        _adaln_kernel,
        out_shape=jax.ShapeDtypeStruct((DEPTH, rows, 3 * D_MODEL), F32),
        grid=(DEPTH, nj),
        in_specs=[pl.BlockSpec((rows, D_MODEL), lambda l, j: (0, 0)),
                  pl.BlockSpec((1, D_MODEL, 1024), lambda l, j: (l, 0, j)),
                  pl.BlockSpec((1, 1, 1024), lambda l, j: (l, 0, j))],
        out_specs=pl.BlockSpec((1, rows, 1024), lambda l, j: (l, 0, j)),
        compiler_params=_params(("arbitrary", "arbitrary")),
        name="adaln",
    )(cond, w_ada, b_ada.reshape(DEPTH, 1, 3 * D_MODEL))


def _wprep_kernel(w_ref, m_ref, s_ref):
    x = w_ref[0]
    o = _W_IN_OFFS
    main = jnp.concatenate([x[:, o['gq'][0]:o['gv'][1]], x[:, o['ggate'][0]:o['ggate'][1]],
                            x[:, o['sgate'][0]:o['dqkv'][1]], x[:, o['dgate'][0]:o['dgate'][1]]], axis=1)
    side = jnp.concatenate([x[:, o['su'][0]:o['su'][1]], x[:, o['glr'][0]:o['glr'][1]], x[:, o['da'][0]:o['db'][1]],
                            jnp.zeros((x.shape[0], LANE - 2 * GLA_LR - 4 * GDN_H), x.dtype)], axis=1)
    m_ref[0] = main.astype(BF16)
    s_ref[0] = side.astype(BF16)


def _w_in_offsets():
    offs, pos = {}, 0
    for name, width in (('gq', GLA_QK), ('gk', GLA_QK), ('gv', GLA_W), ('glr', 2 * GLA_LR), ('ggate', GLA_W),
                        ('su', S5_W), ('sgate', S5_W), ('dqkv', 3 * GDN_W), ('da', 2 * GDN_H),
                        ('db', 2 * GDN_H), ('dgate', GDN_W)):
        offs[name] = (pos, pos + width)
        pos += width
    return offs


_W_IN_OFFS = _w_in_offsets()


def _wprep(w_in):
    depth, d_model, in_dim = w_in.shape
    rows = 256
    return pl.pallas_call(
        _wprep_kernel,
        out_shape=(jax.ShapeDtypeStruct((depth, d_model, MAIN_W), BF16),
                   jax.ShapeDtypeStruct((depth, d_model, SIDE_W), BF16)),
        grid=(depth, d_model // rows),
        in_specs=[pl.BlockSpec((1, rows, in_dim), lambda l, i: (l, i, 0))],
        out_specs=(pl.BlockSpec((1, rows, MAIN_W), lambda l, i: (l, i, 0)),
                   pl.BlockSpec((1, rows, SIDE_W), lambda l, i: (l, i, 0))),
        compiler_params=_params(("arbitrary", "arbitrary")),
        name="wprep",
    )(w_in)


def _inproj_kernel(xc_ref, xs_ref, ada_ref, g_ref, wm_ref, ws_ref, om_ref, os_ref, *, ctx_tiles):
    x = jnp.where(pl.program_id(0) < ctx_tiles, xc_ref[...], xs_ref[...])
    nrm = x * lax.rsqrt(jnp.mean(x * x, axis=-1, keepdims=True) + EPS) * g_ref[...]
    ada = ada_ref[0]
    shift = ada[:, 0:D_MODEL]
    scale = ada[:, D_MODEL:2 * D_MODEL]
    h = (nrm * (1.0 + scale) + shift).astype(BF16)
    om_ref[...] = _dot(h, wm_ref[...]).astype(BF16)
    os_ref[...] = _dot(h, ws_ref[...])


def _ada_row(i, ctx_tiles, tiles_per_latent):
    return jnp.where(i < ctx_tiles, 0, 1 + (i - ctx_tiles) // tiles_per_latent)


def _ctx_tile(i, ctx_tiles):
    return jnp.minimum(i, ctx_tiles - 1)


def _lat_tile(i, ctx_tiles):
    return jnp.maximum(i - ctx_tiles, 0)


def _inproj(x_c, x_s, ada, norm_pre, w_main, w_small, layer, ctx_tiles, tiles_per_latent):
    nt = x_c.shape[0] + x_s.shape[0]
    row = functools.partial(_ada_row, ctx_tiles=ctx_tiles, tiles_per_latent=tiles_per_latent)
    return pl.pallas_call(
        functools.partial(_inproj_kernel, ctx_tiles=ctx_tiles),
        out_shape=(jax.ShapeDtypeStruct((nt, MAIN_W), BF16), jax.ShapeDtypeStruct((nt, SIDE_W), F32)),
        grid=(nt // TOKEN_TILE,),
        in_specs=[pl.BlockSpec((TOKEN_TILE, D_MODEL), lambda i: (_ctx_tile(i, ctx_tiles), 0)),
                  pl.BlockSpec((TOKEN_TILE, D_MODEL), lambda i: (_lat_tile(i, ctx_tiles), 0)),
                  pl.BlockSpec((None, 1, 1, 3 * D_MODEL), lambda i: (layer, row(i), 0, 0)),
                  pl.BlockSpec((None, 1, D_MODEL), lambda i: (layer, 0, 0)),
                  pl.BlockSpec((None, D_MODEL, MAIN_W), lambda i: (layer, 0, 0)),
                  pl.BlockSpec((None, D_MODEL, SIDE_W), lambda i: (layer, 0, 0))],
        out_specs=(pl.BlockSpec((TOKEN_TILE, MAIN_W), lambda i: (i, 0)),
                   pl.BlockSpec((TOKEN_TILE, SIDE_W), lambda i: (i, 0))),
        compiler_params=_params(("arbitrary",)),
        name="inproj",
    )(x_c, x_s, ada, norm_pre, w_main, w_small)


def _tri_mask(n, reverse, strict=False):
    r = lax.broadcasted_iota(jnp.int32, (n, n), 0)
    c = lax.broadcasted_iota(jnp.int32, (n, n), 1)
    if reverse:
        return (r < c) if strict else (r <= c)
    return (r > c) if strict else (r >= c)


def _eye_mask(n):
    return lax.broadcasted_iota(jnp.int32, (n, n), 0) == lax.broadcasted_iota(jnp.int32, (n, n), 1)


GLA_CPI = 4


def _state_refs(rest, has_state, emit_state, aliased):
    rest = list(rest)
    s0_ref = rest.pop(0) if has_state else None
    if aliased:
        rest.pop(0)
    o_ref = rest.pop(0)
    sf_ref = rest.pop(0) if emit_state else None
    return s0_ref, o_ref, sf_ref, rest


def _gla_kernel(q_ref, k_ref, v_ref, sm_ref, gw_ref, gb_ref, *rest, n_chunks, has_state, emit_state, aliased):
    s0_ref, o_ref, sf_ref, (g_s, ob_s, st_ref) = _state_refs(rest, has_state, emit_state, aliased)
    for d in range(2):
        if has_state:
            s0 = jnp.concatenate([s0_ref[0, d, 0], s0_ref[0, d, 1]], axis=0)
            st_ref[d] = s0.T
        else:
            st_ref[d] = jnp.zeros((GLA_DV, LANE), F32)
        z = _dot3(sm_ref[...], gw_ref[d]) + gb_ref[d]
        g_s[d] = -_softplus(-z) * (1.0 / GLA_TAU)

    lane = lax.broadcasted_iota(jnp.int32, (CHUNK, LANE), 1)
    head_mask = [lane < GLA_DK, lane >= GLA_DK]
    scale = GLA_DK ** -0.5
    causal = [_tri_mask(CHUNK, False), _tri_mask(CHUNK, True)]
    causal_bf = [m.astype(BF16) for m in causal]

    def chunk_step(it, carry):
        jd = [(j, d) for j in range(GLA_CPI) for d in range(2)]
        jdh = [(j, d, h) for (j, d) in jd for h in range(2)]
        cc = {(j, d): (it * GLA_CPI + j) if d == 0 else n_chunks - 1 - (it * GLA_CPI + j) for (j, d) in jd}
        rows = {u: pl.ds(pl.multiple_of(cc[u] * CHUNK, CHUNK), CHUNK) for u in jd}
        q = {u: q_ref[rows[u], :].astype(F32) for u in jd}
        k = {u: k_ref[rows[u], :].astype(F32) for u in jd}
        v = {u: v_ref[rows[u], :] for u in jd}
        g = {(j, d): g_s[d, rows[j, d], :] for (j, d) in jd}
        g_hi = {u: g[u].astype(BF16) for u in jd}
        g_lo = {u: (g[u] - g_hi[u].astype(F32)).astype(BF16) for u in jd}
        b = {u: _dot(causal_bf[u[1]], g_hi[u]) + _dot(causal_bf[u[1]], g_lo[u]) for u in jd}
        b_last = {(j, d): b[j, d][CHUNK - 1:CHUNK, :] if d == 0 else b[j, d][0:1, :] for (j, d) in jd}
        qe = {u: q[u] * jnp.exp(b[u]) * scale for u in jd}
        ke = {u: (k[u] * jnp.exp(-b[u])).astype(BF16) for u in jd}
        kd = {u: k[u] * jnp.exp(b_last[u] - b[u]) for u in jd}
        qh = {(j, d, h): jnp.where(head_mask[h], qe[j, d], 0.0).astype(BF16) for (j, d, h) in jdh}
        vh = {(j, d, h): v[j, d][:, h * GLA_DV:(h + 1) * GLA_DV].astype(BF16) for (j, d, h) in jdh}
        sc = {(j, d, h): jnp.where(causal[d], _dot_nt(qh[j, d, h], ke[j, d]), 0.0).astype(BF16)
              for (j, d, h) in jdh}
        upd = {(j, d, h): _dot_tn(vh[j, d, h], jnp.where(head_mask[h], kd[j, d], 0.0).astype(BF16))
               for (j, d, h) in jdh}
        o_intra = {u: _dot(sc[u], vh[u]) for u in jdh}
        for j in range(GLA_CPI):
            st_bf = {d: st_ref[d].astype(BF16) for d in range(2)}
            for d in range(2):
                for h in range(2):
                    oh = o_intra[j, d, h] + _dot_nt(qh[j, d, h], st_bf[d])
                    if d == 0:
                        o_ref[rows[j, d], h * GLA_DV:(h + 1) * GLA_DV] = oh
                    else:
                        ob_s[rows[j, d], h * GLA_DV:(h + 1) * GLA_DV] = oh
                st_ref[d] = st_ref[d] * jnp.exp(b_last[j, d]) + upd[j, d, 0] + upd[j, d, 1]
        return carry

    lax.fori_loop(0, n_chunks // GLA_CPI, chunk_step, 0)
    o_ref[...] += ob_s[...]
    if emit_state:
        for d in range(2):
            s = st_ref[d].T
            for h in range(2):
                sf_ref[0, d, h] = s[h * GLA_DK:(h + 1) * GLA_DK, :]


def _state_plumbing(layer, s0, sf_acc, nseq, heads, hps, dk, dv, n_lead):
    blk = (1, None, 2, hps, dk, dv)
    imap = lambda b, p, *_: (b, layer, 0, p, 0, 0)
    in_specs, operands, out_specs, out_shapes, aliases = [], [], [], [], {}
    if s0 is not None:
        in_specs.append(pl.BlockSpec(blk, imap))
        operands.append(s0)
    emit = sf_acc is not False
    if emit:
        if sf_acc is not None:
            aliases[n_lead + len(operands)] = 1
            in_specs.append(pl.BlockSpec(memory_space=pl.ANY))
            operands.append(sf_acc)
        out_specs.append(pl.BlockSpec(blk, imap))
        out_shapes.append(jax.ShapeDtypeStruct((nseq, DEPTH, 2, heads, dk, dv), F32))
    flags = dict(has_state=s0 is not None, emit_state=emit, aliased=emit and sf_acc is not None)
    return in_specs, operands, out_specs, out_shapes, aliases, flags


def _gla(pm, ps, gw_pad, gb, layer, s0, sf_acc, *, nseq, seq_len, row_block0):
    st_in, st_ops, st_out, st_shapes, aliases, flags = _state_plumbing(
        layer, s0, sf_acc, nseq, GLA_H, 2, GLA_DK, GLA_DV, n_lead=6)
    kern = functools.partial(_gla_kernel, n_chunks=seq_len // CHUNK, **flags)
    return pl.pallas_call(
        kern,
        out_shape=[jax.ShapeDtypeStruct((nseq * seq_len, GLA_W), F32)] + st_shapes,
        grid=(nseq, 2),
        in_specs=[pl.BlockSpec((seq_len, LANE), lambda b, p: (row_block0 + b, COL_GQ + p)),
                  pl.BlockSpec((seq_len, LANE), lambda b, p: (row_block0 + b, COL_GK + p)),
                  pl.BlockSpec((seq_len, 2 * LANE), lambda b, p: (row_block0 + b, COL_GV // 2 + p)),
                  pl.BlockSpec((seq_len, LANE), lambda b, p: (row_block0 + b, COL_SM)),
                  pl.BlockSpec((None, 2, LANE, LANE), lambda b, p: (layer, 0, 0, p)),
                  pl.BlockSpec((None, 2, 1, LANE), lambda b, p: (layer, 0, 0, p))] + st_in,
        out_specs=[pl.BlockSpec((seq_len, 2 * LANE), lambda b, p: (b, p))] + st_out,
        scratch_shapes=[pltpu.VMEM((2, seq_len, LANE), F32), pltpu.VMEM((seq_len, 2 * LANE), F32),
                        pltpu.VMEM((2, GLA_DV, LANE), F32)],
        input_output_aliases=aliases,
        compiler_params=_params(("arbitrary", "arbitrary")),
        name="gla",
    )(pm, pm, pm, ps, gw_pad, gb, *st_ops)


GDN_HPS = 4
GDN_CPI = 4
assert GDN_CPI % 2 == 0 and 2 * CHUNK == LANE


def _block_mask(n, s, reverse):
    r = lax.broadcasted_iota(jnp.int32, (n, n), 0)
    c = lax.broadcasted_iota(jnp.int32, (n, n), 1)
    if reverse:
        r, c = c, r
    sh = s.bit_length() - 1
    same_pair = (r >> (sh + 1)) == (c >> (sh + 1))
    return same_pair & (((r >> sh) & 1) == 1) & (((c >> sh) & 1) == 0)


def _gdn_kernel(ab_ref, q_ref, k_ref, v_ref, sm_ref, cwq_ref, cwk_ref, cwv_ref, *rest,
                layer, n_chunks, period, has_state, emit_state, aliased):
    s0_ref, o_ref, sf_ref, scratch = _state_refs(rest, has_state, emit_state, aliased)
    q_s, k_s, v_s, g_s, b_s, u_s, wq_s, a_s, kd_s, dl_s, ob_s, st_ref = scratch
    hp = pl.program_id(1)
    seq_len = n_chunks * CHUNK
    row = lax.broadcasted_iota(jnp.int32, (seq_len, LANE), 0)
    first = (row % period) == 0
    last = (row % period) == period - 1

    def conv_silu(x, w):
        xp = jnp.where(first, 0.0, pltpu.roll(x, 1, 0))
        xn = jnp.where(last, 0.0, pltpu.roll(x, seq_len - 1, 0))
        return _silu(xp * w[0:1, :] + x * w[1:2, :] + xn * w[2:3, :])

    def l2norm(x):
        return x * lax.rsqrt(jnp.sum(x * x, axis=-1, keepdims=True) + EPS)

    lane1 = lax.broadcasted_iota(jnp.int32, (1, LANE), 1)
    a_log = jnp.zeros((1, LANE), F32)
    dt_bias = jnp.zeros((1, LANE), F32)
    for d in range(2):
        for hh in range(GDN_H):
            a_log = jnp.where(lane1 == SM_DA + d * GDN_H + hh, ab_ref[layer, d, hh], a_log)
            dt_bias = jnp.where(lane1 == SM_DA + d * GDN_H + hh, ab_ref[layer, d, GDN_H + hh], dt_bias)
    sm = sm_ref[...]
    g_all = -jnp.exp(a_log) * _softplus(sm + dt_bias)
    b_all = jax.nn.sigmoid(sm)
    lane = lax.broadcasted_iota(jnp.int32, (seq_len, LANE), 1)
    in_chunk = row % CHUNK

    def chunk_cumsum(x, reverse):
        sh = 1
        while sh < CHUNK:
            if reverse:
                x = x + jnp.where(in_chunk < CHUNK - sh, pltpu.roll(x, seq_len - sh, 0), 0.0)
            else:
                x = x + jnp.where(in_chunk >= sh, pltpu.roll(x, sh, 0), 0.0)
            sh *= 2
        return x

    gc_all = [chunk_cumsum(g_all, False), chunk_cumsum(g_all, True)]

    def lane_bcast(x, j):
        col = jnp.sum(jnp.where(lane == j, x, 0.0), axis=1, keepdims=True)
        return jnp.broadcast_to(col, (seq_len, LANE))

    for h in range(GDN_HPS):
        hd = hp * GDN_HPS + h
        lanes = slice(h * LANE, (h + 1) * LANE)
        q_s[h] = l2norm(conv_silu(q_ref[:, lanes].astype(F32), cwq_ref[:, lanes])) * GDN_DK ** -0.5
        k_s[h] = l2norm(conv_silu(k_ref[:, lanes].astype(F32), cwk_ref[:, lanes]))
        v_s[h] = conv_silu(v_ref[:, lanes].astype(F32), cwv_ref[:, lanes])
        for d in range(2):
            g_s[h, d] = lane_bcast(gc_all[d], SM_DA + d * GDN_H + hd)
            b_s[h, d] = lane_bcast(b_all, SM_DB + d * GDN_H + hd)
            if has_state:
                st_ref[h, d] = s0_ref[0, d, h]
            else:
                st_ref[h, d] = jnp.zeros((GDN_DK, GDN_DV), F32)

    causal = [_tri_mask(CHUNK, False), _tri_mask(CHUNK, True)]
    strict_f = [_tri_mask(CHUNK, d == 1, strict=True).astype(F32) for d in range(2)]
    level_masks = [[_block_mask(CHUNK, 1 << j, d == 1).astype(F32) for j in range(CHUNK.bit_length() - 1)]
                   for d in range(2)]

    eye_f = _eye_mask(CHUNK).astype(F32)

    def phase_a(it, carry):
        cs = [it * GDN_CPI + j for j in range(GDN_CPI)]
        rows = [pl.ds(pl.multiple_of(c * CHUNK, CHUNK), CHUNK) for c in cs]
        rows2 = [pl.ds(pl.multiple_of(c * 2 * CHUNK, 2 * CHUNK), CHUNK) for c in cs]
        rows2b = [pl.ds(pl.multiple_of(c * 2 * CHUNK + CHUNK, CHUNK), CHUNK) for c in cs]
        pairs = [(j, h) for j in range(GDN_CPI) for h in range(GDN_HPS)]
        units = [(j, h, d) for (j, h) in pairs for d in range(2)]
        q = {(j, h): q_s[h, rows[j], :] for (j, h) in pairs}
        k = {(j, h): k_s[h, rows[j], :] for (j, h) in pairs}
        v = {(j, h): v_s[h, rows[j], :] for (j, h) in pairs}
        k_bf = {p: k[p].astype(BF16) for p in pairs}
        beta = {(j, h, d): b_s[h, d, rows[j], :] for (j, h, d) in units}
        gc = {(j, h, d): g_s[h, d, rows[j], :] for (j, h, d) in units}
        gc_t = {}
        for h in range(GDN_HPS):
            for d in range(2):
                for j0 in range(0, GDN_CPI, 2):
                    blk_t = jnp.concatenate([gc[j0, h, d], gc[j0 + 1, h, d]], axis=0).T
                    for j in range(2):
                        gc_t[j0 + j, h, d] = blk_t[0:CHUNK, j * CHUNK:(j + 1) * CHUNK]
        qk = {p: _dot_nt(q[p].astype(BF16), k_bf[p]) for p in pairs}
        kk = {p: _dot_nt(k_bf[p], k_bf[p]) for p in pairs}
        decay, m, t = {}, {}, {}
        for u in units:
            j, h, d = u
            diff = gc[u][:, :CHUNK] - gc_t[u]
            decay[u] = jnp.where(causal[d], jnp.exp(jnp.where(causal[d], diff, 0.0)), 0.0)
            m[u] = kk[j, h] * beta[u][:, :CHUNK] * decay[u] * strict_f[d]
            t[u] = eye_f - m[u] * level_masks[d][0]
        for lvl in range(1, len(level_masks[0])):
            t_bf = {u: t[u].astype(BF16) for u in units}
            p1 = {u: _dot(t_bf[u], (m[u] * level_masks[u[2]][lvl]).astype(BF16)).astype(BF16) for u in units}
            t = {u: t[u] - _dot(p1[u], t_bf[u]) for u in units}
        egc = {u: jnp.exp(gc[u]) for u in units}
        rhs = {(j, h, d): jnp.concatenate([v[j, h] * beta[j, h, d], k[j, h] * (beta[j, h, d] * egc[j, h, d])],
                                          axis=1).astype(BF16) for (j, h, d) in units}
        uw = {u: _dot(t[u].astype(BF16), rhs[u]) for u in units}
        for u in units:
            j, h, d = u
            gc_last = gc[u][CHUNK - 1:CHUNK, :] if d == 0 else gc[u][0:1, :]
            u_s[h, d, rows[j], :] = uw[u][:, 0:GDN_DV]
            wq_s[h, d, rows2[j], :] = uw[u][:, GDN_DV:GDN_DV + GDN_DK].astype(BF16)
            wq_s[h, d, rows2b[j], :] = (q[j, h] * egc[u]).astype(BF16)
            a_s[h, d, rows[j], :] = (qk[j, h] * decay[u]).astype(BF16)
            kd_s[h, d, rows[j], :] = (k[j, h] * jnp.exp(gc_last - gc[u])).astype(BF16)
            dl_s[h, d, cs[j]] = jnp.broadcast_to(jnp.exp(gc_last), (8, LANE))
        return carry

    lax.fori_loop(0, n_chunks // GDN_CPI, phase_a, 0)

    def phase_b(i, carry):
        chains = [(h, d) for h in range(GDN_HPS) for d in range(2)]
        cc = {0: i, 1: n_chunks - 1 - i}
        rows = {d: pl.ds(pl.multiple_of(cc[d] * CHUNK, CHUNK), CHUNK) for d in range(2)}
        rows2 = {d: pl.ds(pl.multiple_of(cc[d] * 2 * CHUNK, 2 * CHUNK), 2 * CHUNK) for d in range(2)}
        s = {hd: st_ref[hd[0], hd[1]] for hd in chains}
        ws = {(h, d): _dot(wq_s[h, d, rows2[d], :], s[h, d].astype(BF16)) for (h, d) in chains}
        v_new = {(h, d): (u_s[h, d, rows[d], :] - ws[h, d][0:CHUNK, :]).astype(BF16) for (h, d) in chains}
        o = {(h, d): ws[h, d][CHUNK:2 * CHUNK, :] + _dot(a_s[h, d, rows[d], :], v_new[h, d]) for (h, d) in chains}
        upd = {(h, d): _dot_tn(kd_s[h, d, rows[d], :], v_new[h, d]) for (h, d) in chains}
        for (h, d) in chains:
            lanes = slice(h * LANE, (h + 1) * LANE)
            if d == 0:
                o_ref[rows[d], lanes] = o[h, d]
            else:
                ob_s[rows[d], lanes] = o[h, d]
            st_ref[h, d] = dl_s[h, d, cc[d]][0:1, :] * s[h, d] + upd[h, d]
        return carry

    lax.fori_loop(0, n_chunks, phase_b, 0)
    o_ref[...] += ob_s[...]
    if emit_state:
        for h in range(GDN_HPS):
            for d in range(2):
                sf_ref[0, d, h] = st_ref[h, d]


def _gdn(pm, ps, ab, conv_w, layer, s0, sf_acc, *, nseq, seq_len, row_block0, period):
    hps = GDN_HPS
    st_in, st_ops, st_out, st_shapes, aliases, flags = _state_plumbing(
        layer, s0, sf_acc, nseq, GDN_H, hps, GDN_DK, GDN_DV, n_lead=8)
    n_chunks = seq_len // CHUNK
    kern = functools.partial(_gdn_kernel, layer=layer, n_chunks=n_chunks, period=period, **flags)
    wide = hps * LANE
    npair = GDN_H // hps
    grid_spec = pltpu.PrefetchScalarGridSpec(
        num_scalar_prefetch=1,
        grid=(nseq, npair),
        in_specs=[pl.BlockSpec((seq_len, wide), lambda b, p, ab: (row_block0 + b, COL_DQ // hps + p)),
                  pl.BlockSpec((seq_len, wide), lambda b, p, ab: (row_block0 + b, COL_DK // hps + p)),
                  pl.BlockSpec((seq_len, wide), lambda b, p, ab: (row_block0 + b, COL_DV // hps + p)),
                  pl.BlockSpec((seq_len, LANE), lambda b, p, ab: (row_block0 + b, COL_SM)),
                  pl.BlockSpec((None, 3, wide), lambda b, p, ab: (layer, 0, p)),
                  pl.BlockSpec((None, 3, wide), lambda b, p, ab: (layer, 0, npair + p)),
                  pl.BlockSpec((None, 3, wide), lambda b, p, ab: (layer, 0, 2 * npair + p))] + st_in,
        out_specs=[pl.BlockSpec((seq_len, wide), lambda b, p, ab: (b, p))] + st_out,
        scratch_shapes=[pltpu.VMEM((hps, seq_len, LANE), F32),
                        pltpu.VMEM((hps, seq_len, LANE), F32),
                        pltpu.VMEM((hps, seq_len, LANE), F32),
                        pltpu.VMEM((hps, 2, seq_len, LANE), F32),
                        pltpu.VMEM((hps, 2, seq_len, LANE), F32),
                        pltpu.VMEM((hps, 2, seq_len, GDN_DV), F32),
                        pltpu.VMEM((hps, 2, 2 * seq_len, GDN_DK), BF16),
                        pltpu.VMEM((hps, 2, seq_len, CHUNK), BF16),
                        pltpu.VMEM((hps, 2, seq_len, GDN_DK), BF16),
                        pltpu.VMEM((hps, 2, n_chunks, 8, LANE), F32),
                        pltpu.VMEM((seq_len, wide), F32),
                        pltpu.VMEM((hps, 2, GDN_DK, GDN_DV), F32)],
    )
    return pl.pallas_call(
        kern,
        out_shape=[jax.ShapeDtypeStruct((nseq * seq_len, GDN_W), F32)] + st_shapes,
        grid_spec=grid_spec,
        input_output_aliases=aliases,
        compiler_params=_params(("arbitrary", "arbitrary")),
        name="gdn",
    )(ab, pm, pm, pm, ps, conv_w, conv_w, conv_w, *st_ops)


S5_GPB = LANE // S5_GH
S5_XROWS = 4096


def _lane_block_transpose(arrs, lane_grp):
    arrs = list(arrs)
    k = S5_GPB // 2
    while k:
        high = (lane_grp & k) != 0
        for i in range(S5_GPB):
            if not i & k:
                lo, hi = arrs[i], arrs[i + k]
                arrs[i] = jnp.where(high, pltpu.roll(hi, k * S5_GH, 1), lo)
                arrs[i + k] = jnp.where(high, hi, pltpu.roll(lo, LANE - k * S5_GH, 1))
        k //= 2
    return arrs


def _s5_kernel(*refs, n_chunks, nseq, latent):
    n_x = len(refs) - 11
    x_refs = refs[:n_x]
    mi_ref, wst_ref, wout_ref, lam_ref, h0_ref, y_ref, fin_ref, u_s, e_s, es_s, y_s = refs[n_x:]
    seq_len = n_chunks * S5_T
    seq_per_x = S5_XROWS // seq_len
    lane_grp = lax.broadcasted_iota(jnp.int32, (n_chunks, LANE), 1) >> 4

    def slab_rows(base, t):
        if latent:
            return pl.ds(pl.multiple_of(base + t * GRID_W, GRID_W), n_chunks)
        return pl.ds(base + t, n_chunks, stride=S5_T)

    def relayout_in(x_ref, b0):
        def body(bl, carry):
            base = bl * seq_len
            b = b0 + bl
            for lt in range(2):
                slabs = [x_ref[slab_rows(base, S5_GPB * lt + tt), :] for tt in range(S5_GPB)]
                for gi, tile in enumerate(_lane_block_transpose(slabs, lane_grp)):
                    u_s[gi, lt, pl.ds(b, n_chunks, stride=nseq), :] = tile
            return carry
        lax.fori_loop(0, seq_per_x, body, 0)

    for i, x_ref in enumerate(x_refs):
        relayout_in(x_ref, i * seq_per_x)

    half = 2 * S5_P

    def swap(x):
        return pltpu.roll(x, S5_P, 1)

    u_bf = [jnp.concatenate([u_s[gi, 0], u_s[gi, 1]], axis=1).astype(BF16) for gi in range(S5_GPB)]
    for gi in range(S5_GPB):
        e = _dot(u_bf[gi], wst_ref[gi])
        for d in range(2):
            e_s[gi, d] = e[:, d * half:(d + 1) * half]
            es_s[gi, d] = swap(e[:, d * half:(d + 1) * half])
    for gi in range(S5_GPB):
        y = _dot(u_bf[gi], mi_ref[gi])
        y_s[gi, 0] = y[:, 0:LANE]
        y_s[gi, 1] = y[:, LANE:2 * LANE]

    lam = [lam_ref[gi] for gi in range(S5_GPB)]

    def step(c, carry):
        rf = pl.ds(pl.multiple_of(c * nseq, nseq), nseq)
        rb = pl.ds(pl.multiple_of((n_chunks - 1 - c) * nseq, nseq), nseq)
        new = []
        for gi in range(S5_GPB):
            l = lam[gi]
            for d, rows in ((0, rf), (1, rb)):
                x, xs = carry[4 * gi + 2 * d], carry[4 * gi + 2 * d + 1]
                u_s[gi, d, rows, :] = x
                l1, l2 = l[2 * d:2 * d + 1, :], l[2 * d + 1:2 * d + 2, :]
                new.append(x * l1 + xs * l2 + e_s[gi, d, rows, :])
                new.append(xs * l1 - x * l2 + es_s[gi, d, rows, :])
        return tuple(new)

    init = []
    for gi in range(S5_GPB):
        h0 = h0_ref[gi]
        for d in range(2):
            x0 = h0[:, d * half:(d + 1) * half]
            init += [x0, swap(x0)]
    fin = lax.fori_loop(0, n_chunks, step, tuple(init))
    for gi in range(S5_GPB):
        fin_ref[gi, :, 0:half] = fin[4 * gi]
        fin_ref[gi, :, half:2 * half] = fin[4 * gi + 2]
        x_in = jnp.concatenate([u_s[gi, 0], u_s[gi, 1]], axis=1).astype(BF16)
        y = _dot(x_in, wout_ref[gi])
        y_s[gi, 0] += y[:, 0:LANE]
        y_s[gi, 1] += y[:, LANE:2 * LANE]

    def relayout_out(b, carry):
        base = b * seq_len
        for lt in range(2):
            tiles = [y_s[gi, lt, pl.ds(b, n_chunks, stride=nseq), :] for gi in range(S5_GPB)]
            for tt, slab in enumerate(_lane_block_transpose(tiles, lane_grp)):
                y_ref[slab_rows(base, S5_GPB * lt + tt), :] = slab
        return carry

    lax.fori_loop(0, nseq, relayout_out, 0)


def _s5(ps, m_intra, w_st, w_out, lam_t, h0, layer, h0_layer, *, nseq, n_chunks, row_block0, latent):
    rows = n_chunks * nseq
    n_tok = rows * S5_T
    n_x = n_tok // S5_XROWS
    kern = functools.partial(_s5_kernel, n_chunks=n_chunks, nseq=nseq, latent=latent)
    wspec = pl.BlockSpec((None, S5_GPB, 2 * LANE, 2 * LANE), lambda j: (layer, j, 0, 0))
    x_specs = [pl.BlockSpec((S5_XROWS, LANE), functools.partial(lambda j, i: (row_block0 + i, COL_SU + j), i=i))
               for i in range(n_x)]
    plane = pltpu.VMEM((S5_GPB, 2, rows, LANE), F32)
    return pl.pallas_call(
        kern,
        out_shape=(jax.ShapeDtypeStruct((n_tok, S5_W), F32),
                   jax.ShapeDtypeStruct((S5_G, nseq, 2 * LANE), F32)),
        grid=(S5_G // S5_GPB,),
        in_specs=x_specs + [wspec, wspec, wspec,
                            pl.BlockSpec((None, S5_GPB, 4, LANE), lambda j: (layer, j, 0, 0)),
                            pl.BlockSpec((None, S5_GPB, nseq, 2 * LANE), lambda j: (h0_layer, j, 0, 0))],
        out_specs=(pl.BlockSpec((n_tok, LANE), lambda j: (0, j)),
                   pl.BlockSpec((S5_GPB, nseq, 2 * LANE), lambda j: (j, 0, 0))),
        scratch_shapes=[plane, plane, plane, plane],
        compiler_params=_params(("arbitrary",)),
        name="s5",
    )(*([ps] * n_x), m_intra, w_st, w_out, lam_t, h0)


def _cmul(ar, ai, br, bi):
    return ar * br - ai * bi, ar * bi + ai * br


def _s5_weights(lam_re, lam_im, log_dt, b_re, b_im, c_re, c_im):
    t = S5_T
    dt = jnp.exp(log_dt)[..., None]
    tau = jnp.arange(t + 1, dtype=F32)[:, None, None, None]
    mag = jnp.exp(tau * (lam_re * dt)[None])
    ang = tau * (lam_im * dt)[None]
    pw_re, pw_im = mag * jnp.cos(ang), mag * jnp.sin(ang)
    nr, ni = pw_re[1] - 1.0, pw_im[1]
    den = lam_re * lam_re + lam_im * lam_im
    fr, fi = (nr * lam_re + ni * lam_im) / den, (ni * lam_re - nr * lam_im) / den
    bb_re, bb_im = _cmul(fr[..., None], fi[..., None], b_re[None], b_im[None])

    bt_re, bt_im = jnp.swapaxes(bb_re, -1, -2), jnp.swapaxes(bb_im, -1, -2)
    ct_re, ct_im = jnp.swapaxes(c_re, -1, -2), jnp.swapaxes(c_im, -1, -2)
    pwt_re, pwt_im = jnp.moveaxis(pw_re, 0, -1), jnp.moveaxis(pw_im, 0, -1)
    cl_re, cl_im = _cmul(pwt_re[..., None], pwt_im[..., None], ct_re[:, :, :, None, :], ct_im[:, :, :, None, :])

    def lanes(x):
        return x.reshape(S5_G, S5_P, t * S5_GH)

    def k_rows(d, taus):
        return (jnp.einsum('gip,gpx->gix', bt_re[d], lanes(cl_re[d][:, :, taus]), precision=HI)
                - jnp.einsum('gip,gpx->gix', bt_im[d], lanes(cl_im[d][:, :, taus]), precision=HI))

    width = t * S5_GH
    pad = (t - 1) * S5_GH
    kf = jnp.pad(k_rows(0, slice(0, t)), ((0, 0), (0, 0), (pad, 0)))
    kb = jnp.pad(k_rows(1, slice(t - 1, None, -1)), ((0, 0), (0, 0), (0, pad)))
    m_intra = jnp.stack([kf[:, :, pad - s * S5_GH:pad - s * S5_GH + width]
                         + kb[:, :, (t - 1 - s) * S5_GH:(t - 1 - s) * S5_GH + width] for s in range(t)], axis=1)
    m_intra = m_intra.reshape(S5_G, width, width)

    def st(d, taus):
        return _cmul(jnp.moveaxis(pw_re[taus, d], 0, 1)[:, :, None, :], jnp.moveaxis(pw_im[taus, d], 0, 1)[:, :, None, :],
                     bt_re[d][:, None], bt_im[d][:, None])
    w_st = jnp.concatenate(st(0, slice(t - 1, None, -1)) + st(1, slice(0, t)), axis=-1)
    w_st = w_st.reshape(S5_G, width, 4 * S5_P)

    w_out = jnp.concatenate([lanes(cl_re[0][:, :, 1:t + 1]), -lanes(cl_im[0][:, :, 1:t + 1]),
                             lanes(cl_re[1][:, :, t:0:-1]), -lanes(cl_im[1][:, :, t:0:-1])], axis=1)


    lt_re, lt_im = pw_re[t], pw_im[t]
    lam_t = jnp.stack([jnp.concatenate([lt_re[0], lt_re[0]], -1), jnp.concatenate([-lt_im[0], lt_im[0]], -1),
                       jnp.concatenate([lt_re[1], lt_re[1]], -1), jnp.concatenate([-lt_im[1], lt_im[1]], -1)],
                      axis=1)
    return m_intra.astype(BF16), w_st.astype(BF16), w_out.astype(BF16), lam_t


def _gelu_tanh(x):
    return 0.5 * x * (1.0 + jnp.tanh(math.sqrt(2.0 / math.pi) * (x + 0.044715 * (x * x * x))))


def _out_kernel(xc_ref, xs_ref, ada_ref, ogc_ref, ogs_ref, gg_ref, ysc_ref, yss_ref, su_ref, sg_ref,
                odc_ref, ods_ref, dg_ref, gn_ref, dn_ref, sd_ref, gw_ref, gb_ref, wo_ref, np_ref,
                oc_ref, os_ref, *, ctx_tiles):
    is_ctx = pl.program_id(0) < ctx_tiles

    def pick(c_ref, s_ref):
        return jnp.where(is_ctx, c_ref[...], s_ref[...])

    def head_norm(o, g):
        parts = []
        for h in range(o.shape[1] // LANE):
            oh = o[:, h * LANE:(h + 1) * LANE]
            parts.append(oh * lax.rsqrt(jnp.mean(oh * oh, axis=-1, keepdims=True) + EPS) * g)
        return jnp.concatenate(parts, axis=1)

    o_gla = head_norm(pick(ogc_ref, ogs_ref), gn_ref[...]) * _silu(gg_ref[...].astype(F32))
    y = _gelu_tanh(pick(ysc_ref, yss_ref) + sd_ref[...] * su_ref[...])
    y = y * jax.nn.sigmoid(_dot(y.astype(BF16), gw_ref[...]) + gb_ref[...])
    o_s5 = y * _silu(sg_ref[...].astype(F32))
    o_gdn = head_norm(pick(odc_ref, ods_ref), dn_ref[...]) * _silu(dg_ref[...].astype(F32))
    out = (_dot(o_gla.astype(BF16), wo_ref[0:GLA_W, :])
           + _dot(o_s5.astype(BF16), wo_ref[GLA_W:GLA_W + S5_W, :])
           + _dot(o_gdn.astype(BF16), wo_ref[GLA_W + S5_W:MIX_W, :]))
    r = out * lax.rsqrt(jnp.mean(out * out, axis=-1, keepdims=True) + EPS) * np_ref[...]
    gate = ada_ref[0][:, 2 * D_MODEL:3 * D_MODEL]
    x_new = pick(xc_ref, xs_ref) + gate * r

    @pl.when(is_ctx)
    def _():
        oc_ref[...] = x_new

    @pl.when(jnp.logical_not(is_ctx))
    def _():
        os_ref[...] = x_new


def _out(x_c, x_s, ada, pm, ps, og_c, og_s, y_c, y_s, od_c, od_s, gla_norm, gdn_norm, s5_d, glu_w, glu_b, w_out,
         norm_post, layer, ctx_tiles, tiles_per_latent):
    nt = x_c.shape[0] + x_s.shape[0]
    row = functools.partial(_ada_row, ctx_tiles=ctx_tiles, tiles_per_latent=tiles_per_latent)
    wide = 4 * LANE

    def tok(col_block):
        return pl.BlockSpec((TOKEN_TILE, wide), lambda i: (i, col_block))

    def ctx(width):
        return pl.BlockSpec((TOKEN_TILE, width), lambda i: (_ctx_tile(i, ctx_tiles), 0))

    def lat(width):
        return pl.BlockSpec((TOKEN_TILE, width), lambda i: (_lat_tile(i, ctx_tiles), 0))

    def full(shape):
        return pl.BlockSpec((None,) + shape, lambda i: (layer,) + (0,) * len(shape))

    return pl.pallas_call(
        functools.partial(_out_kernel, ctx_tiles=ctx_tiles),
        out_shape=(jax.ShapeDtypeStruct(x_c.shape, F32), jax.ShapeDtypeStruct(x_s.shape, F32)),
        grid=(nt // TOKEN_TILE,),
        in_specs=[ctx(D_MODEL), lat(D_MODEL),
                  pl.BlockSpec((None, 1, 1, 3 * D_MODEL), lambda i: (layer, row(i), 0, 0)),
                  ctx(wide), lat(wide), tok(COL_GGATE // 4),
                  ctx(wide), lat(wide), tok(COL_SU // 4), tok(COL_SGATE // 4),
                  ctx(wide), lat(wide), tok(COL_DGATE // 4),
                  full((1, LANE)), full((1, LANE)), full((1, S5_W)), full((S5_W, S5_W)), full((1, S5_W)),
                  full((MIX_W, D_MODEL)), full((1, D_MODEL))],
        out_specs=(ctx(D_MODEL), lat(D_MODEL)),
        compiler_params=_params(("arbitrary",)),
        name="outproj",
    )(x_c, x_s, ada, og_c, og_s, pm, y_c, y_s, ps, pm, od_c, od_s, pm, gla_norm, gdn_norm, s5_d, glu_w, glu_b,
      w_out, norm_post)


def kernel(x_prompt, x_sample, c, state_gla, state_s5_re, state_s5_im, state_gdn, c_ctx, norm_pre, norm_post, w_ada, b_ada, w_in, gla_gate_w, gla_gate_b, gla_norm, s5_lam_re, s5_lam_im, s5_log_dt, s5_b_re, s5_b_im, s5_c_re, s5_c_im, s5_d, s5_glu_w, s5_glu_b, gdn_conv, gdn_a_log, gdn_dt_bias, gdn_norm, w_out):
    bp, lp, _ = x_prompt.shape
    bs, ls, _ = x_sample.shape
    n_ctx = bp * lp
    assert lp % TOKEN_TILE == 0 and ls % TOKEN_TILE == 0 and n_ctx % ls == 0
    assert ls // GRID_W == S5_T and lp % S5_T == 0 and n_ctx % S5_XROWS == 0 and (bs * ls) % S5_XROWS == 0
    ctx_tiles = n_ctx // TOKEN_TILE
    tiles_per_latent = ls // TOKEN_TILE

    cond = jnp.concatenate([c_ctx[None].astype(F32), c.astype(F32)], axis=0)
    rows = -(-cond.shape[0] // 8) * 8
    cond = jnp.pad(cond, ((0, rows - cond.shape[0]), (0, 0)))
    ada = _adaln(cond, w_ada.astype(F32), b_ada.astype(F32)).reshape(DEPTH, rows, 1, 3 * D_MODEL)

    w_main, w_small = _wprep(w_in.astype(F32))
    w_out_bf = w_out.astype(BF16)
    glu_w_bf = s5_glu_w.astype(BF16)
    gw_pad = jnp.zeros((DEPTH, 2, LANE, GLA_QK), F32)
    for d in range(2):
        gw_pad = gw_pad.at[:, d, SM_GLR + d * GLA_LR:SM_GLR + (d + 1) * GLA_LR, :].set(gla_gate_w[:, d].astype(F32))
    gb = gla_gate_b.astype(F32).reshape(DEPTH, 2, 1, GLA_QK)
    gdn_ab = jnp.concatenate([gdn_a_log, gdn_dt_bias], axis=-1).astype(F32)
    conv_w = gdn_conv.astype(F32)
    m_intra, w_st, w_ro, lam_t = jax.vmap(_s5_weights)(*(p.astype(F32) for p in (
        s5_lam_re, s5_lam_im, s5_log_dt, s5_b_re, s5_b_im, s5_c_re, s5_c_im)))
    h0_c = jnp.zeros((1, S5_G, bp, 4 * S5_P), F32)
    sre, sim = state_s5_re.astype(F32), state_s5_im.astype(F32)
    h0_s = jnp.concatenate([sre[:, :, 0], sim[:, :, 0], sre[:, :, 1], sim[:, :, 1]], axis=-1).transpose(1, 2, 0, 3)
    vec = lambda p: p.astype(F32).reshape(DEPTH, 1, -1)
    norm_pre_v, norm_post_v, gla_norm_v, gdn_norm_v = vec(norm_pre), vec(norm_post), vec(gla_norm), vec(gdn_norm)
    s5_d_v, glu_b_v = vec(s5_d), vec(s5_glu_b)
    st_gla, st_gdn = state_gla.astype(F32), state_gdn.astype(F32)

    x_c = x_prompt.reshape(n_ctx, D_MODEL).astype(F32)
    x_s = x_sample.reshape(bs * ls, D_MODEL).astype(F32)
    new_gla, new_gdn, s5_states = None, None, []
    lat0 = n_ctx // ls
    for l in range(DEPTH):
        pm, ps = _inproj(x_c, x_s, ada, norm_pre_v, w_main, w_small, l, ctx_tiles, tiles_per_latent)
        og_c, new_gla = _gla(pm, ps, gw_pad, gb, l, None, new_gla, nseq=bp, seq_len=lp, row_block0=0)
        og_s, = _gla(pm, ps, gw_pad, gb, l, st_gla, False, nseq=bs, seq_len=ls, row_block0=lat0)
        od_c, new_gdn = _gdn(pm, ps, gdn_ab, conv_w, l, None, new_gdn, nseq=bp, seq_len=lp, row_block0=0, period=lp)
        od_s, = _gdn(pm, ps, gdn_ab, conv_w, l, st_gdn, False, nseq=bs, seq_len=ls, row_block0=lat0,
                     period=GRID_W)
        y_c, fin = _s5(ps, m_intra, w_st, w_ro, lam_t, h0_c, l, 0, nseq=bp, n_chunks=lp // S5_T, row_block0=0,
                       latent=False)
        y_s, _ = _s5(ps, m_intra, w_st, w_ro, lam_t, h0_s, l, l, nseq=bs, n_chunks=GRID_W,
                     row_block0=n_ctx // S5_XROWS, latent=True)
        x_c, x_s = _out(x_c, x_s, ada, pm, ps, og_c, og_s, y_c, y_s, od_c, od_s, gla_norm_v, gdn_norm_v, s5_d_v,
                        glu_w_bf, glu_b_v, w_out_bf, norm_post_v, l, ctx_tiles, tiles_per_latent)
        fin = fin.transpose(1, 0, 2).reshape(bp, S5_G, 2, 2, S5_P)
        s5_states.append(fin.transpose(0, 2, 3, 1, 4))

    dt = x_prompt.dtype
    s5_all = jnp.stack(s5_states, axis=1)
    y_prompt = x_c.reshape(bp, lp, D_MODEL).astype(dt)
    y_sample = x_s.reshape(bs, ls, D_MODEL).astype(x_sample.dtype)
    return (y_prompt, y_sample, new_gla.astype(dt), s5_all[:, :, :, 0].astype(dt), s5_all[:, :, :, 1].astype(dt),
            new_gdn.astype(dt))
```

```python
import functools
import math

import jax
import jax.numpy as jnp
from jax import lax
from jax.experimental import pallas as pl
from jax.experimental.pallas import tpu as pltpu

F32 = jnp.float32
BF16 = jnp.bfloat16
HI = lax.Precision.HIGHEST

D_MODEL = 1024
DEPTH = 4
GRID_W = 64
CHUNK = 64
EPS = 1e-6
GLA_H, GLA_DK, GLA_DV, GLA_LR, GLA_TAU = 4, 64, 128, 16, 16.0
GLA_QK, GLA_W = GLA_H * GLA_DK, GLA_H * GLA_DV
S5_GH, S5_W, S5_P = 16, 512, 64
S5_G = S5_W // S5_GH
S5_T = 16
GDN_H, GDN_DK, GDN_DV = 4, 128, 128
GDN_W = GDN_H * GDN_DV
MIX_W = GLA_W + S5_W + GDN_W

LANE = 128
TOKEN_TILE = 256
VMEM_LIMIT = 48 * 1024 * 1024

COL_GQ, COL_GK, COL_GV, COL_GGATE, COL_SGATE = 0, 2, 4, 8, 12
COL_DQ, COL_DK, COL_DV, COL_DGATE = 16, 20, 24, 28
MAIN_W = 32 * LANE
COL_SU, COL_SM = 0, 4
SIDE_W = 5 * LANE
SM_GLR, SM_DA, SM_DB = 0, 32, 40


def _dot(a, b, precision=None):
    return lax.dot_general(a, b, (((1,), (0,)), ((), ())), precision=precision,
                           preferred_element_type=F32)


def _dot_nt(a, b, precision=None):
    return lax.dot_general(a, b, (((1,), (1,)), ((), ())), precision=precision,
                           preferred_element_type=F32)


def _dot_tn(a, b, precision=None):
    return lax.dot_general(a, b, (((0,), (0,)), ((), ())), precision=precision,
                           preferred_element_type=F32)


def _split_bf16(x):
    hi = x.astype(BF16)
    return hi, (x - hi.astype(F32)).astype(BF16)


def _dot3(a, b):
    a_hi, a_lo = _split_bf16(a)
    b_hi, b_lo = _split_bf16(b)
    return _dot(a_hi, b_hi) + (_dot(a_hi, b_lo) + _dot(a_lo, b_hi))


def _silu(x):
    return x * jax.nn.sigmoid(x)


def _softplus(x):
    return jnp.maximum(x, 0.0) + jnp.log1p(jnp.exp(-jnp.abs(x)))


def _params(sem):
    return pltpu.CompilerParams(dimension_semantics=sem, vmem_limit_bytes=VMEM_LIMIT)


def _adaln_kernel(cond_ref, w_ref, b_ref, o_ref):
    c = cond_ref[...]
    o_ref[0] = _dot(_silu(c), w_ref[0], precision=HI) + b_ref[0]


def _adaln(cond, w_ada, b_ada):
    rows = cond.shape[0]
    nj = 3 * D_MODEL // 1024
    return pl.pallas_call(
        _adaln_kernel,
        out_shape=jax.ShapeDtypeStruct((DEPTH, rows, 3 * D_MODEL), F32),
        grid=(DEPTH, nj),
        in_specs=[pl.BlockSpec((rows, D_MODEL), lambda l, j: (0, 0)),
                  pl.BlockSpec((1, D_MODEL, 1024), lambda l, j: (l, 0, j)),
                  pl.BlockSpec((1, 1, 1024), lambda l, j: (l, 0, j))],
        out_specs=pl.BlockSpec((1, rows, 1024), lambda l, j: (l, 0, j)),
        compiler_params=_params(("arbitrary", "arbitrary")),
        name="adaln",
    )(cond, w_ada, b_ada.reshape(DEPTH, 1, 3 * D_MODEL))


def _wprep_kernel(w_ref, m_ref, s_ref):
    x = w_ref[0]
    o = _W_IN_OFFS
    main = jnp.concatenate([x[:, o['gq'][0]:o['gv'][1]], x[:, o['ggate'][0]:o['ggate'][1]],
                            x[:, o['sgate'][0]:o['dqkv'][1]], x[:, o['dgate'][0]:o['dgate'][1]]], axis=1)
    side = jnp.concatenate([x[:, o['su'][0]:o['su'][1]], x[:, o['glr'][0]:o['glr'][1]], x[:, o['da'][0]:o['db'][1]],
                            jnp.zeros((x.shape[0], LANE - 2 * GLA_LR - 4 * GDN_H), x.dtype)], axis=1)
    m_ref[0] = main.astype(BF16)
    s_ref[0] = side.astype(BF16)


def _w_in_offsets():
    offs, pos = {}, 0
    for name, width in (('gq', GLA_QK), ('gk', GLA_QK), ('gv', GLA_W), ('glr', 2 * GLA_LR), ('ggate', GLA_W),
                        ('su', S5_W), ('sgate', S5_W), ('dqkv', 3 * GDN_W), ('da', 2 * GDN_H),
                        ('db', 2 * GDN_H), ('dgate', GDN_W)):
        offs[name] = (pos, pos + width)
        pos += width
    return offs


_W_IN_OFFS = _w_in_offsets()


def _wprep(w_in):
    depth, d_model, in_dim = w_in.shape
    rows = 256
    return pl.pallas_call(
        _wprep_kernel,
        out_shape=(jax.ShapeDtypeStruct((depth, d_model, MAIN_W), BF16),
                   jax.ShapeDtypeStruct((depth, d_model, SIDE_W), BF16)),
        grid=(depth, d_model // rows),
        in_specs=[pl.BlockSpec((1, rows, in_dim), lambda l, i: (l, i, 0))],
        out_specs=(pl.BlockSpec((1, rows, MAIN_W), lambda l, i: (l, i, 0)),
                   pl.BlockSpec((1, rows, SIDE_W), lambda l, i: (l, i, 0))),
        compiler_params=_params(("arbitrary", "arbitrary")),
        name="wprep",
    )(w_in)


def _inproj_kernel(xc_ref, xs_ref, ada_ref, g_ref, wm_ref, ws_ref, om_ref, os_ref, *, ctx_tiles):
    x = jnp.where(pl.program_id(0) < ctx_tiles, xc_ref[...], xs_ref[...])
    nrm = x * lax.rsqrt(jnp.mean(x * x, axis=-1, keepdims=True) + EPS) * g_ref[...]
    ada = ada_ref[0]
    shift = ada[:, 0:D_MODEL]
    scale = ada[:, D_MODEL:2 * D_MODEL]
    h = (nrm * (1.0 + scale) + shift).astype(BF16)
    om_ref[...] = _dot(h, wm_ref[...]).astype(BF16)
    os_ref[...] = _dot(h, ws_ref[...])


def _ada_row(i, ctx_tiles, tiles_per_latent):
    return jnp.where(i < ctx_tiles, 0, 1 + (i - ctx_tiles) // tiles_per_latent)


def _ctx_tile(i, ctx_tiles):
    return jnp.minimum(i, ctx_tiles - 1)


def _lat_tile(i, ctx_tiles):
    return jnp.maximum(i - ctx_tiles, 0)


def _inproj(x_c, x_s, ada, norm_pre, w_main, w_small, layer, ctx_tiles, tiles_per_latent):
    nt = x_c.shape[0] + x_s.shape[0]
    row = functools.partial(_ada_row, ctx_tiles=ctx_tiles, tiles_per_latent=tiles_per_latent)
    return pl.pallas_call(
        functools.partial(_inproj_kernel, ctx_tiles=ctx_tiles),
        out_shape=(jax.ShapeDtypeStruct((nt, MAIN_W), BF16), jax.ShapeDtypeStruct((nt, SIDE_W), F32)),
        grid=(nt // TOKEN_TILE,),
        in_specs=[pl.BlockSpec((TOKEN_TILE, D_MODEL), lambda i: (_ctx_tile(i, ctx_tiles), 0)),
                  pl.BlockSpec((TOKEN_TILE, D_MODEL), lambda i: (_lat_tile(i, ctx_tiles), 0)),
                  pl.BlockSpec((None, 1, 1, 3 * D_MODEL), lambda i: (layer, row(i), 0, 0)),
                  pl.BlockSpec((None, 1, D_MODEL), lambda i: (layer, 0, 0)),
                  pl.BlockSpec((None, D_MODEL, MAIN_W), lambda i: (layer, 0, 0)),
                  pl.BlockSpec((None, D_MODEL, SIDE_W), lambda i: (layer, 0, 0))],
        out_specs=(pl.BlockSpec((TOKEN_TILE, MAIN_W), lambda i: (i, 0)),
                   pl.BlockSpec((TOKEN_TILE, SIDE_W), lambda i: (i, 0))),
        compiler_params=_params(("arbitrary",)),
        name="inproj",
    )(x_c, x_s, ada, norm_pre, w_main, w_small)


def _tri_mask(n, reverse, strict=False):
    r = lax.broadcasted_iota(jnp.int32, (n, n), 0)
    c = lax.broadcasted_iota(jnp.int32, (n, n), 1)
    if reverse:
        return (r < c) if strict else (r <= c)
    return (r > c) if strict else (r >= c)


def _eye_mask(n):
    return lax.broadcasted_iota(jnp.int32, (n, n), 0) == lax.broadcasted_iota(jnp.int32, (n, n), 1)


GLA_CPI = 4


def _state_refs(rest, has_state, emit_state, aliased):
    rest = list(rest)
    s0_ref = rest.pop(0) if has_state else None
    if aliased:
        rest.pop(0)
    o_ref = rest.pop(0)
    sf_ref = rest.pop(0) if emit_state else None
    return s0_ref, o_ref, sf_ref, rest


def _gla_kernel(q_ref, k_ref, v_ref, sm_ref, gw_ref, gb_ref, *rest, n_chunks, has_state, emit_state, aliased):
    s0_ref, o_ref, sf_ref, (g_s, ob_s, st_ref) = _state_refs(rest, has_state, emit_state, aliased)
    for d in range(2):
        if has_state:
            s0 = jnp.concatenate([s0_ref[0, d, 0], s0_ref[0, d, 1]], axis=0)
            st_ref[d] = s0.T
        else:
            st_ref[d] = jnp.zeros((GLA_DV, LANE), F32)
        z = _dot3(sm_ref[...], gw_ref[d]) + gb_ref[d]
        g_s[d] = -_softplus(-z) * (1.0 / GLA_TAU)

    lane = lax.broadcasted_iota(jnp.int32, (CHUNK, LANE), 1)
    head_mask = [lane < GLA_DK, lane >= GLA_DK]
    scale = GLA_DK ** -0.5
    causal = [_tri_mask(CHUNK, False), _tri_mask(CHUNK, True)]
    causal_bf = [m.astype(BF16) for m in causal]

    def chunk_step(it, carry):
        jd = [(j, d) for j in range(GLA_CPI) for d in range(2)]
        jdh = [(j, d, h) for (j, d) in jd for h in range(2)]
        cc = {(j, d): (it * GLA_CPI + j) if d == 0 else n_chunks - 1 - (it * GLA_CPI + j) for (j, d) in jd}
        rows = {u: pl.ds(pl.multiple_of(cc[u] * CHUNK, CHUNK), CHUNK) for u in jd}
        q = {u: q_ref[rows[u], :].astype(F32) for u in jd}
        k = {u: k_ref[rows[u], :].astype(F32) for u in jd}
        v = {u: v_ref[rows[u], :] for u in jd}
        g = {(j, d): g_s[d, rows[j, d], :] for (j, d) in jd}
        g_hi = {u: g[u].astype(BF16) for u in jd}
        g_lo = {u: (g[u] - g_hi[u].astype(F32)).astype(BF16) for u in jd}
        b = {u: _dot(causal_bf[u[1]], g_hi[u]) + _dot(causal_bf[u[1]], g_lo[u]) for u in jd}
        b_last = {(j, d): b[j, d][CHUNK - 1:CHUNK, :] if d == 0 else b[j, d][0:1, :] for (j, d) in jd}
        qe = {u: q[u] * jnp.exp(b[u]) * scale for u in jd}
        ke = {u: (k[u] * jnp.exp(-b[u])).astype(BF16) for u in jd}
        kd = {u: k[u] * jnp.exp(b_last[u] - b[u]) for u in jd}
        qh = {(j, d, h): jnp.where(head_mask[h], qe[j, d], 0.0).astype(BF16) for (j, d, h) in jdh}
        vh = {(j, d, h): v[j, d][:, h * GLA_DV:(h + 1) * GLA_DV].astype(BF16) for (j, d, h) in jdh}
        sc = {(j, d, h): jnp.where(causal[d], _dot_nt(qh[j, d, h], ke[j, d]), 0.0).astype(BF16)
              for (j, d, h) in jdh}
        upd = {(j, d, h): _dot_tn(vh[j, d, h], jnp.where(head_mask[h], kd[j, d], 0.0).astype(BF16))
               for (j, d, h) in jdh}
        o_intra = {u: _dot(sc[u], vh[u]) for u in jdh}
        for j in range(GLA_CPI):
            st_bf = {d: st_ref[d].astype(BF16) for d in range(2)}
            for d in range(2):
                for h in range(2):
                    oh = o_intra[j, d, h] + _dot_nt(qh[j, d, h], st_bf[d])
                    if d == 0:
                        o_ref[rows[j, d], h * GLA_DV:(h + 1) * GLA_DV] = oh
                    else:
                        ob_s[rows[j, d], h * GLA_DV:(h + 1) * GLA_DV] = oh
                st_ref[d] = st_ref[d] * jnp.exp(b_last[j, d]) + upd[j, d, 0] + upd[j, d, 1]
        return carry

    lax.fori_loop(0, n_chunks // GLA_CPI, chunk_step, 0)
    o_ref[...] += ob_s[...]
    if emit_state:
        for d in range(2):
            s = st_ref[d].T
            for h in range(2):
                sf_ref[0, d, h] = s[h * GLA_DK:(h + 1) * GLA_DK, :]


def _state_plumbing(layer, s0, sf_acc, nseq, heads, hps, dk, dv, n_lead):
    blk = (1, None, 2, hps, dk, dv)
    imap = lambda b, p, *_: (b, layer, 0, p, 0, 0)
    in_specs, operands, out_specs, out_shapes, aliases = [], [], [], [], {}
    if s0 is not None:
        in_specs.append(pl.BlockSpec(blk, imap))
        operands.append(s0)
    emit = sf_acc is not False
    if emit:
        if sf_acc is not None:
            aliases[n_lead + len(operands)] = 1
            in_specs.append(pl.BlockSpec(memory_space=pl.ANY))
            operands.append(sf_acc)
        out_specs.append(pl.BlockSpec(blk, imap))
        out_shapes.append(jax.ShapeDtypeStruct((nseq, DEPTH, 2, heads, dk, dv), F32))
    flags = dict(has_state=s0 is not None, emit_state=emit, aliased=emit and sf_acc is not None)
    return in_specs, operands, out_specs, out_shapes, aliases, flags


def _gla(pm, ps, gw_pad, gb, layer, s0, sf_acc, *, nseq, seq_len, row_block0):
    st_in, st_ops, st_out, st_shapes, aliases, flags = _state_plumbing(
        layer, s0, sf_acc, nseq, GLA_H, 2, GLA_DK, GLA_DV, n_lead=6)
    kern = functools.partial(_gla_kernel, n_chunks=seq_len // CHUNK, **flags)
    return pl.pallas_call(
        kern,
        out_shape=[jax.ShapeDtypeStruct((nseq * seq_len, GLA_W), F32)] + st_shapes,
        grid=(nseq, 2),
        in_specs=[pl.BlockSpec((seq_len, LANE), lambda b, p: (row_block0 + b, COL_GQ + p)),
                  pl.BlockSpec((seq_len, LANE), lambda b, p: (row_block0 + b, COL_GK + p)),
                  pl.BlockSpec((seq_len, 2 * LANE), lambda b, p: (row_block0 + b, COL_GV // 2 + p)),
                  pl.BlockSpec((seq_len, LANE), lambda b, p: (row_block0 + b, COL_SM)),
                  pl.BlockSpec((None, 2, LANE, LANE), lambda b, p: (layer, 0, 0, p)),
                  pl.BlockSpec((None, 2, 1, LANE), lambda b, p: (layer, 0, 0, p))] + st_in,
        out_specs=[pl.BlockSpec((seq_len, 2 * LANE), lambda b, p: (b, p))] + st_out,
        scratch_shapes=[pltpu.VMEM((2, seq_len, LANE), F32), pltpu.VMEM((seq_len, 2 * LANE), F32),
                        pltpu.VMEM((2, GLA_DV, LANE), F32)],
        input_output_aliases=aliases,
        compiler_params=_params(("arbitrary", "arbitrary")),
        name="gla",
    )(pm, pm, pm, ps, gw_pad, gb, *st_ops)


GDN_HPS = 4
GDN_CPI = 4
assert GDN_CPI % 2 == 0 and 2 * CHUNK == LANE


def _block_mask(n, s, reverse):
    r = lax.broadcasted_iota(jnp.int32, (n, n), 0)
    c = lax.broadcasted_iota(jnp.int32, (n, n), 1)
    if reverse:
        r, c = c, r
    sh = s.bit_length() - 1
    same_pair = (r >> (sh + 1)) == (c >> (sh + 1))
    return same_pair & (((r >> sh) & 1) == 1) & (((c >> sh) & 1) == 0)


def _gdn_kernel(ab_ref, q_ref, k_ref, v_ref, sm_ref, cwq_ref, cwk_ref, cwv_ref, *rest,
                layer, n_chunks, period, has_state, emit_state, aliased):
    s0_ref, o_ref, sf_ref, scratch = _state_refs(rest, has_state, emit_state, aliased)
    q_s, k_s, v_s, g_s, b_s, u_s, wq_s, a_s, kd_s, dl_s, ob_s, st_ref = scratch
    hp = pl.program_id(1)
    seq_len = n_chunks * CHUNK
    row = lax.broadcasted_iota(jnp.int32, (seq_len, LANE), 0)
    first = (row % period) == 0
    last = (row % period) == period - 1

    def conv_silu(x, w):
        xp = jnp.where(first, 0.0, pltpu.roll(x, 1, 0))
        xn = jnp.where(last, 0.0, pltpu.roll(x, seq_len - 1, 0))
        return _silu(xp * w[0:1, :] + x * w[1:2, :] + xn * w[2:3, :])

    def l2norm(x):
        return x * lax.rsqrt(jnp.sum(x * x, axis=-1, keepdims=True) + EPS)

    lane1 = lax.broadcasted_iota(jnp.int32, (1, LANE), 1)
    a_log = jnp.zeros((1, LANE), F32)
    dt_bias = jnp.zeros((1, LANE), F32)
    for d in range(2):
        for hh in range(GDN_H):
            a_log = jnp.where(lane1 == SM_DA + d * GDN_H + hh, ab_ref[layer, d, hh], a_log)
            dt_bias = jnp.where(lane1 == SM_DA + d * GDN_H + hh, ab_ref[layer, d, GDN_H + hh], dt_bias)
    sm = sm_ref[...]
    g_all = -jnp.exp(a_log) * _softplus(sm + dt_bias)
    b_all = jax.nn.sigmoid(sm)
    lane = lax.broadcasted_iota(jnp.int32, (seq_len, LANE), 1)
    in_chunk = row % CHUNK

    def chunk_cumsum(x, reverse):
        sh = 1
        while sh < CHUNK:
            if reverse:
                x = x + jnp.where(in_chunk < CHUNK - sh, pltpu.roll(x, seq_len - sh, 0), 0.0)
            else:
                x = x + jnp.where(in_chunk >= sh, pltpu.roll(x, sh, 0), 0.0)
            sh *= 2
        return x

    gc_all = [chunk_cumsum(g_all, False), chunk_cumsum(g_all, True)]

    def lane_bcast(x, j):
        col = jnp.sum(jnp.where(lane == j, x, 0.0), axis=1, keepdims=True)
        return jnp.broadcast_to(col, (seq_len, LANE))

    for h in range(GDN_HPS):
        hd = hp * GDN_HPS + h
        lanes = slice(h * LANE, (h + 1) * LANE)
        q_s[h] = l2norm(conv_silu(q_ref[:, lanes].astype(F32), cwq_ref[:, lanes])) * GDN_DK ** -0.5
        k_s[h] = l2norm(conv_silu(k_ref[:, lanes].astype(F32), cwk_ref[:, lanes]))
        v_s[h] = conv_silu(v_ref[:, lanes].astype(F32), cwv_ref[:, lanes])
        for d in range(2):
            g_s[h, d] = lane_bcast(gc_all[d], SM_DA + d * GDN_H + hd)
            b_s[h, d] = lane_bcast(b_all, SM_DB + d * GDN_H + hd)
            if has_state:
                st_ref[h, d] = s0_ref[0, d, h]
            else:
                st_ref[h, d] = jnp.zeros((GDN_DK, GDN_DV), F32)

    causal = [_tri_mask(CHUNK, False), _tri_mask(CHUNK, True)]
    strict_f = [_tri_mask(CHUNK, d == 1, strict=True).astype(F32) for d in range(2)]
    level_masks = [[_block_mask(CHUNK, 1 << j, d == 1).astype(F32) for j in range(CHUNK.bit_length() - 1)]
                   for d in range(2)]

    eye_f = _eye_mask(CHUNK).astype(F32)

    def phase_a(it, carry):
        cs = [it * GDN_CPI + j for j in range(GDN_CPI)]
        rows = [pl.ds(pl.multiple_of(c * CHUNK, CHUNK), CHUNK) for c in cs]
        rows2 = [pl.ds(pl.multiple_of(c * 2 * CHUNK, 2 * CHUNK), CHUNK) for c in cs]
        rows2b = [pl.ds(pl.multiple_of(c * 2 * CHUNK + CHUNK, CHUNK), CHUNK) for c in cs]
        pairs = [(j, h) for j in range(GDN_CPI) for h in range(GDN_HPS)]
        units = [(j, h, d) for (j, h) in pairs for d in range(2)]
        q = {(j, h): q_s[h, rows[j], :] for (j, h) in pairs}
        k = {(j, h): k_s[h, rows[j], :] for (j, h) in pairs}
        v = {(j, h): v_s[h, rows[j], :] for (j, h) in pairs}
        k_bf = {p: k[p].astype(BF16) for p in pairs}
        beta = {(j, h, d): b_s[h, d, rows[j], :] for (j, h, d) in units}
        gc = {(j, h, d): g_s[h, d, rows[j], :] for (j, h, d) in units}
        gc_t = {}
        for h in range(GDN_HPS):
            for d in range(2):
                for j0 in range(0, GDN_CPI, 2):
                    blk_t = jnp.concatenate([gc[j0, h, d], gc[j0 + 1, h, d]], axis=0).T
                    for j in range(2):
                        gc_t[j0 + j, h, d] = blk_t[0:CHUNK, j * CHUNK:(j + 1) * CHUNK]
        qk = {p: _dot_nt(q[p].astype(BF16), k_bf[p]) for p in pairs}
        kk = {p: _dot_nt(k_bf[p], k_bf[p]) for p in pairs}
        decay, m, t = {}, {}, {}
        for u in units:
            j, h, d = u
            diff = gc[u][:, :CHUNK] - gc_t[u]
            decay[u] = jnp.where(causal[d], jnp.exp(jnp.where(causal[d], diff, 0.0)), 0.0)
            m[u] = kk[j, h] * beta[u][:, :CHUNK] * decay[u] * strict_f[d]
            t[u] = eye_f - m[u] * level_masks[d][0]
        for lvl in range(1, len(level_masks[0])):
            t_bf = {u: t[u].astype(BF16) for u in units}
            p1 = {u: _dot(t_bf[u], (m[u] * level_masks[u[2]][lvl]).astype(BF16)).astype(BF16) for u in units}
            t = {u: t[u] - _dot(p1[u], t_bf[u]) for u in units}
        egc = {u: jnp.exp(gc[u]) for u in units}
        rhs = {(j, h, d): jnp.concatenate([v[j, h] * beta[j, h, d], k[j, h] * (beta[j, h, d] * egc[j, h, d])],
                                          axis=1).astype(BF16) for (j, h, d) in units}
        uw = {u: _dot(t[u].astype(BF16), rhs[u]) for u in units}
        for u in units:
            j, h, d = u
            gc_last = gc[u][CHUNK - 1:CHUNK, :] if d == 0 else gc[u][0:1, :]
            u_s[h, d, rows[j], :] = uw[u][:, 0:GDN_DV]
            wq_s[h, d, rows2[j], :] = uw[u][:, GDN_DV:GDN_DV + GDN_DK].astype(BF16)
            wq_s[h, d, rows2b[j], :] = (q[j, h] * egc[u]).astype(BF16)
            a_s[h, d, rows[j], :] = (qk[j, h] * decay[u]).astype(BF16)
            kd_s[h, d, rows[j], :] = (k[j, h] * jnp.exp(gc_last - gc[u])).astype(BF16)
            dl_s[h, d, cs[j]] = jnp.broadcast_to(jnp.exp(gc_last), (8, LANE))
        return carry

    lax.fori_loop(0, n_chunks // GDN_CPI, phase_a, 0)

    def phase_b(i, carry):
        chains = [(h, d) for h in range(GDN_HPS) for d in range(2)]
        cc = {0: i, 1: n_chunks - 1 - i}
        rows = {d: pl.ds(pl.multiple_of(cc[d] * CHUNK, CHUNK), CHUNK) for d in range(2)}
        rows2 = {d: pl.ds(pl.multiple_of(cc[d] * 2 * CHUNK, 2 * CHUNK), 2 * CHUNK) for d in range(2)}
        s = {hd: st_ref[hd[0], hd[1]] for hd in chains}
        ws = {(h, d): _dot(wq_s[h, d, rows2[d], :], s[h, d].astype(BF16)) for (h, d) in chains}
        v_new = {(h, d): (u_s[h, d, rows[d], :] - ws[h, d][0:CHUNK, :]).astype(BF16) for (h, d) in chains}
        o = {(h, d): ws[h, d][CHUNK:2 * CHUNK, :] + _dot(a_s[h, d, rows[d], :], v_new[h, d]) for (h, d) in chains}
        upd = {(h, d): _dot_tn(kd_s[h, d, rows[d], :], v_new[h, d]) for (h, d) in chains}
        for (h, d) in chains:
            lanes = slice(h * LANE, (h + 1) * LANE)
            if d == 0:
                o_ref[rows[d], lanes] = o[h, d]
            else:
                ob_s[rows[d], lanes] = o[h, d]
            st_ref[h, d] = dl_s[h, d, cc[d]][0:1, :] * s[h, d] + upd[h, d]
        return carry

    lax.fori_loop(0, n_chunks, phase_b, 0)
    o_ref[...] += ob_s[...]
    if emit_state:
        for h in range(GDN_HPS):
            for d in range(2):
                sf_ref[0, d, h] = st_ref[h, d]


def _gdn(pm, ps, ab, conv_w, layer, s0, sf_acc, *, nseq, seq_len, row_block0, period):
    hps = GDN_HPS
    st_in, st_ops, st_out, st_shapes, aliases, flags = _state_plumbing(
        layer, s0, sf_acc, nseq, GDN_H, hps, GDN_DK, GDN_DV, n_lead=8)
    n_chunks = seq_len // CHUNK
    kern = functools.partial(_gdn_kernel, layer=layer, n_chunks=n_chunks, period=period, **flags)
    wide = hps * LANE
    npair = GDN_H // hps
    grid_spec = pltpu.PrefetchScalarGridSpec(
        num_scalar_prefetch=1,
        grid=(nseq, npair),
        in_specs=[pl.BlockSpec((seq_len, wide), lambda b, p, ab: (row_block0 + b, COL_DQ // hps + p)),
                  pl.BlockSpec((seq_len, wide), lambda b, p, ab: (row_block0 + b, COL_DK // hps + p)),
                  pl.BlockSpec((seq_len, wide), lambda b, p, ab: (row_block0 + b, COL_DV // hps + p)),
                  pl.BlockSpec((seq_len, LANE), lambda b, p, ab: (row_block0 + b, COL_SM)),
                  pl.BlockSpec((None, 3, wide), lambda b, p, ab: (layer, 0, p)),
                  pl.BlockSpec((None, 3, wide), lambda b, p, ab: (layer, 0, npair + p)),
                  pl.BlockSpec((None, 3, wide), lambda b, p, ab: (layer, 0, 2 * npair + p))] + st_in,
        out_specs=[pl.BlockSpec((seq_len, wide), lambda b, p, ab: (b, p))] + st_out,
        scratch_shapes=[pltpu.VMEM((hps, seq_len, LANE), F32),
                        pltpu.VMEM((hps, seq_len, LANE), F32),
                        pltpu.VMEM((hps, seq_len, LANE), F32),
                        pltpu.VMEM((hps, 2, seq_len, LANE), F32),
                        pltpu.VMEM((hps, 2, seq_len, LANE), F32),
                        pltpu.VMEM((hps, 2, seq_len, GDN_DV), F32),
                        pltpu.VMEM((hps, 2, 2 * seq_len, GDN_DK), BF16),
                        pltpu.VMEM((hps, 2, seq_len, CHUNK), BF16),
                        pltpu.VMEM((hps, 2, seq_len, GDN_DK), BF16),
                        pltpu.VMEM((hps, 2, n_chunks, 8, LANE), F32),
                        pltpu.VMEM((seq_len, wide), F32),
                        pltpu.VMEM((hps, 2, GDN_DK, GDN_DV), F32)],
    )
    return pl.pallas_call(
        kern,
        out_shape=[jax.ShapeDtypeStruct((nseq * seq_len, GDN_W), F32)] + st_shapes,
        grid_spec=grid_spec,
        input_output_aliases=aliases,
        compiler_params=_params(("arbitrary", "arbitrary")),
        name="gdn",
    )(ab, pm, pm, pm, ps, conv_w, conv_w, conv_w, *st_ops)


S5_GPB = LANE // S5_GH
S5_XROWS = 4096


def _lane_block_transpose(arrs, lane_grp):
    arrs = list(arrs)
    k = S5_GPB // 2
    while k:
        high = (lane_grp & k) != 0
        for i in range(S5_GPB):
            if not i & k:
                lo, hi = arrs[i], arrs[i + k]
                arrs[i] = jnp.where(high, pltpu.roll(hi, k * S5_GH, 1), lo)
                arrs[i + k] = jnp.where(high, hi, pltpu.roll(lo, LANE - k * S5_GH, 1))
        k //= 2
    return arrs


def _s5_kernel(*refs, n_chunks, nseq, latent):
    n_x = len(refs) - 11
    x_refs = refs[:n_x]
    mi_ref, wst_ref, wout_ref, lam_ref, h0_ref, y_ref, fin_ref, u_s, e_s, es_s, y_s = refs[n_x:]
    seq_len = n_chunks * S5_T
    seq_per_x = S5_XROWS // seq_len
    lane_grp = lax.broadcasted_iota(jnp.int32, (n_chunks, LANE), 1) >> 4

    def slab_rows(base, t):
        if latent:
            return pl.ds(pl.multiple_of(base + t * GRID_W, GRID_W), n_chunks)
        return pl.ds(base + t, n_chunks, stride=S5_T)

    def relayout_in(x_ref, b0):
        def body(bl, carry):
            base = bl * seq_len
            b = b0 + bl
            for lt in range(2):
                slabs = [x_ref[slab_rows(base, S5_GPB * lt + tt), :] for tt in range(S5_GPB)]
                for gi, tile in enumerate(_lane_block_transpose(slabs, lane_grp)):
                    u_s[gi, lt, pl.ds(b, n_chunks, stride=nseq), :] = tile
            return carry
        lax.fori_loop(0, seq_per_x, body, 0)

    for i, x_ref in enumerate(x_refs):
        relayout_in(x_ref, i * seq_per_x)

    half = 2 * S5_P

    def swap(x):
        return pltpu.roll(x, S5_P, 1)

    u_bf = [jnp.concatenate([u_s[gi, 0], u_s[gi, 1]], axis=1).astype(BF16) for gi in range(S5_GPB)]
    for gi in range(S5_GPB):
        e = _dot(u_bf[gi], wst_ref[gi])
        for d in range(2):
            e_s[gi, d] = e[:, d * half:(d + 1) * half]
            es_s[gi, d] = swap(e[:, d * half:(d + 1) * half])
    for gi in range(S5_GPB):
        y = _dot(u_bf[gi], mi_ref[gi])
        y_s[gi, 0] = y[:, 0:LANE]
        y_s[gi, 1] = y[:, LANE:2 * LANE]

    lam = [lam_ref[gi] for gi in range(S5_GPB)]

    def step(c, carry):
        rf = pl.ds(pl.multiple_of(c * nseq, nseq), nseq)
        rb = pl.ds(pl.multiple_of((n_chunks - 1 - c) * nseq, nseq), nseq)
        new = []
        for gi in range(S5_GPB):
            l = lam[gi]
            for d, rows in ((0, rf), (1, rb)):
                x, xs = carry[4 * gi + 2 * d], carry[4 * gi + 2 * d + 1]
                u_s[gi, d, rows, :] = x
                l1, l2 = l[2 * d:2 * d + 1, :], l[2 * d + 1:2 * d + 2, :]
                new.append(x * l1 + xs * l2 + e_s[gi, d, rows, :])
                new.append(xs * l1 - x * l2 + es_s[gi, d, rows, :])
        return tuple(new)

    init = []
    for gi in range(S5_GPB):
        h0 = h0_ref[gi]
        for d in range(2):
            x0 = h0[:, d * half:(d + 1) * half]
            init += [x0, swap(x0)]
    fin = lax.fori_loop(0, n_chunks, step, tuple(init))
    for gi in range(S5_GPB):
        fin_ref[gi, :, 0:half] = fin[4 * gi]
        fin_ref[gi, :, half:2 * half] = fin[4 * gi + 2]
        x_in = jnp.concatenate([u_s[gi, 0], u_s[gi, 1]], axis=1).astype(BF16)
        y = _dot(x_in, wout_ref[gi])
        y_s[gi, 0] += y[:, 0:LANE]
        y_s[gi, 1] += y[:, LANE:2 * LANE]

    def relayout_out(b, carry):
        base = b * seq_len
        for lt in range(2):
            tiles = [y_s[gi, lt, pl.ds(b, n_chunks, stride=nseq), :] for gi in range(S5_GPB)]
            for tt, slab in enumerate(_lane_block_transpose(tiles, lane_grp)):
                y_ref[slab_rows(base, S5_GPB * lt + tt), :] = slab
        return carry

    lax.fori_loop(0, nseq, relayout_out, 0)


def _s5(ps, m_intra, w_st, w_out, lam_t, h0, layer, h0_layer, *, nseq, n_chunks, row_block0, latent):
    rows = n_chunks * nseq
    n_tok = rows * S5_T
    n_x = n_tok // S5_XROWS
    kern = functools.partial(_s5_kernel, n_chunks=n_chunks, nseq=nseq, latent=latent)
    wspec = pl.BlockSpec((None, S5_GPB, 2 * LANE, 2 * LANE), lambda j: (layer, j, 0, 0))
    x_specs = [pl.BlockSpec((S5_XROWS, LANE), functools.partial(lambda j, i: (row_block0 + i, COL_SU + j), i=i))
               for i in range(n_x)]
    plane = pltpu.VMEM((S5_GPB, 2, rows, LANE), F32)
    return pl.pallas_call(
        kern,
        out_shape=(jax.ShapeDtypeStruct((n_tok, S5_W), F32),
                   jax.ShapeDtypeStruct((S5_G, nseq, 2 * LANE), F32)),
        grid=(S5_G // S5_GPB,),
        in_specs=x_specs + [wspec, wspec, wspec,
                            pl.BlockSpec((None, S5_GPB, 4, LANE), lambda j: (layer, j, 0, 0)),
                            pl.BlockSpec((None, S5_GPB, nseq, 2 * LANE), lambda j: (h0_layer, j, 0, 0))],
        out_specs=(pl.BlockSpec((n_tok, LANE), lambda j: (0, j)),
                   pl.BlockSpec((S5_GPB, nseq, 2 * LANE), lambda j: (j, 0, 0))),
        scratch_shapes=[plane, plane, plane, plane],
        compiler_params=_params(("arbitrary",)),
        name="s5",
    )(*([ps] * n_x), m_intra, w_st, w_out, lam_t, h0)


def _cmul(ar, ai, br, bi):
    return ar * br - ai * bi, ar * bi + ai * br


def _s5_weights(lam_re, lam_im, log_dt, b_re, b_im, c_re, c_im):
    t = S5_T
    dt = jnp.exp(log_dt)[..., None]
    tau = jnp.arange(t + 1, dtype=F32)[:, None, None, None]
    mag = jnp.exp(tau * (lam_re * dt)[None])
    ang = tau * (lam_im * dt)[None]
    pw_re, pw_im = mag * jnp.cos(ang), mag * jnp.sin(ang)
    nr, ni = pw_re[1] - 1.0, pw_im[1]
    den = lam_re * lam_re + lam_im * lam_im
    fr, fi = (nr * lam_re + ni * lam_im) / den, (ni * lam_re - nr * lam_im) / den
    bb_re, bb_im = _cmul(fr[..., None], fi[..., None], b_re[None], b_im[None])

    bt_re, bt_im = jnp.swapaxes(bb_re, -1, -2), jnp.swapaxes(bb_im, -1, -2)
    ct_re, ct_im = jnp.swapaxes(c_re, -1, -2), jnp.swapaxes(c_im, -1, -2)
    pwt_re, pwt_im = jnp.moveaxis(pw_re, 0, -1), jnp.moveaxis(pw_im, 0, -1)
    cl_re, cl_im = _cmul(pwt_re[..., None], pwt_im[..., None], ct_re[:, :, :, None, :], ct_im[:, :, :, None, :])

    def lanes(x):
        return x.reshape(S5_G, S5_P, t * S5_GH)

    def k_rows(d, taus):
        return (jnp.einsum('gip,gpx->gix', bt_re[d], lanes(cl_re[d][:, :, taus]), precision=HI)
                - jnp.einsum('gip,gpx->gix', bt_im[d], lanes(cl_im[d][:, :, taus]), precision=HI))

    width = t * S5_GH
    pad = (t - 1) * S5_GH
    kf = jnp.pad(k_rows(0, slice(0, t)), ((0, 0), (0, 0), (pad, 0)))
    kb = jnp.pad(k_rows(1, slice(t - 1, None, -1)), ((0, 0), (0, 0), (0, pad)))
    m_intra = jnp.stack([kf[:, :, pad - s * S5_GH:pad - s * S5_GH + width]
                         + kb[:, :, (t - 1 - s) * S5_GH:(t - 1 - s) * S5_GH + width] for s in range(t)], axis=1)
    m_intra = m_intra.reshape(S5_G, width, width)

    def st(d, taus):
        return _cmul(jnp.moveaxis(pw_re[taus, d], 0, 1)[:, :, None, :], jnp.moveaxis(pw_im[taus, d], 0, 1)[:, :, None, :],
                     bt_re[d][:, None], bt_im[d][:, None])
    w_st = jnp.concatenate(st(0, slice(t - 1, None, -1)) + st(1, slice(0, t)), axis=-1)
    w_st = w_st.reshape(S5_G, width, 4 * S5_P)

    w_out = jnp.concatenate([lanes(cl_re[0][:, :, 1:t + 1]), -lanes(cl_im[0][:, :, 1:t + 1]),
                             lanes(cl_re[1][:, :, t:0:-1]), -lanes(cl_im[1][:, :, t:0:-1])], axis=1)


    lt_re, lt_im = pw_re[t], pw_im[t]
    lam_t = jnp.stack([jnp.concatenate([lt_re[0], lt_re[0]], -1), jnp.concatenate([-lt_im[0], lt_im[0]], -1),
                       jnp.concatenate([lt_re[1], lt_re[1]], -1), jnp.concatenate([-lt_im[1], lt_im[1]], -1)],
                      axis=1)
    return m_intra.astype(BF16), w_st.astype(BF16), w_out.astype(BF16), lam_t


def _gelu_tanh(x):
    return 0.5 * x * (1.0 + jnp.tanh(math.sqrt(2.0 / math.pi) * (x + 0.044715 * (x * x * x))))


def _out_kernel(xc_ref, xs_ref, ada_ref, ogc_ref, ogs_ref, gg_ref, ysc_ref, yss_ref, su_ref, sg_ref,
                odc_ref, ods_ref, dg_ref, gn_ref, dn_ref, sd_ref, gw_ref, gb_ref, wo_ref, np_ref,
                oc_ref, os_ref, *, ctx_tiles):
    is_ctx = pl.program_id(0) < ctx_tiles

    def pick(c_ref, s_ref):
        return jnp.where(is_ctx, c_ref[...], s_ref[...])

    def head_norm(o, g):
        parts = []
        for h in range(o.shape[1] // LANE):
            oh = o[:, h * LANE:(h + 1) * LANE]
            parts.append(oh * lax.rsqrt(jnp.mean(oh * oh, axis=-1, keepdims=True) + EPS) * g)
        return jnp.concatenate(parts, axis=1)

    o_gla = head_norm(pick(ogc_ref, ogs_ref), gn_ref[...]) * _silu(gg_ref[...].astype(F32))
    y = _gelu_tanh(pick(ysc_ref, yss_ref) + sd_ref[...] * su_ref[...])
    y = y * jax.nn.sigmoid(_dot(y.astype(BF16), gw_ref[...]) + gb_ref[...])
    o_s5 = y * _silu(sg_ref[...].astype(F32))
    o_gdn = head_norm(pick(odc_ref, ods_ref), dn_ref[...]) * _silu(dg_ref[...].astype(F32))
    out = (_dot(o_gla.astype(BF16), wo_ref[0:GLA_W, :])
           + _dot(o_s5.astype(BF16), wo_ref[GLA_W:GLA_W + S5_W, :])
           + _dot(o_gdn.astype(BF16), wo_ref[GLA_W + S5_W:MIX_W, :]))
    r = out * lax.rsqrt(jnp.mean(out * out, axis=-1, keepdims=True) + EPS) * np_ref[...]
    gate = ada_ref[0][:, 2 * D_MODEL:3 * D_MODEL]
    x_new = pick(xc_ref, xs_ref) + gate * r

    @pl.when(is_ctx)
    def _():
        oc_ref[...] = x_new

    @pl.when(jnp.logical_not(is_ctx))
    def _():
        os_ref[...] = x_new


def _out(x_c, x_s, ada, pm, ps, og_c, og_s, y_c, y_s, od_c, od_s, gla_norm, gdn_norm, s5_d, glu_w, glu_b, w_out,
         norm_post, layer, ctx_tiles, tiles_per_latent):
    nt = x_c.shape[0] + x_s.shape[0]
    row = functools.partial(_ada_row, ctx_tiles=ctx_tiles, tiles_per_latent=tiles_per_latent)
    wide = 4 * LANE

    def tok(col_block):
        return pl.BlockSpec((TOKEN_TILE, wide), lambda i: (i, col_block))

    def ctx(width):
        return pl.BlockSpec((TOKEN_TILE, width), lambda i: (_ctx_tile(i, ctx_tiles), 0))

    def lat(width):
        return pl.BlockSpec((TOKEN_TILE, width), lambda i: (_lat_tile(i, ctx_tiles), 0))

    def full(shape):
        return pl.BlockSpec((None,) + shape, lambda i: (layer,) + (0,) * len(shape))

    return pl.pallas_call(
        functools.partial(_out_kernel, ctx_tiles=ctx_tiles),
        out_shape=(jax.ShapeDtypeStruct(x_c.shape, F32), jax.ShapeDtypeStruct(x_s.shape, F32)),
        grid=(nt // TOKEN_TILE,),
        in_specs=[ctx(D_MODEL), lat(D_MODEL),
                  pl.BlockSpec((None, 1, 1, 3 * D_MODEL), lambda i: (layer, row(i), 0, 0)),
                  ctx(wide), lat(wide), tok(COL_GGATE // 4),
                  ctx(wide), lat(wide), tok(COL_SU // 4), tok(COL_SGATE // 4),
                  ctx(wide), lat(wide), tok(COL_DGATE // 4),
                  full((1, LANE)), full((1, LANE)), full((1, S5_W)), full((S5_W, S5_W)), full((1, S5_W)),
                  full((MIX_W, D_MODEL)), full((1, D_MODEL))],
        out_specs=(ctx(D_MODEL), lat(D_MODEL)),
        compiler_params=_params(("arbitrary",)),
        name="outproj",
    )(x_c, x_s, ada, og_c, og_s, pm, y_c, y_s, ps, pm, od_c, od_s, pm, gla_norm, gdn_norm, s5_d, glu_w, glu_b,
      w_out, norm_post)


def kernel(x_prompt, x_sample, c, state_gla, state_s5_re, state_s5_im, state_gdn, c_ctx, norm_pre, norm_post, w_ada, b_ada, w_in, gla_gate_w, gla_gate_b, gla_norm, s5_lam_re, s5_lam_im, s5_log_dt, s5_b_re, s5_b_im, s5_c_re, s5_c_im, s5_d, s5_glu_w, s5_glu_b, gdn_conv, gdn_a_log, gdn_dt_bias, gdn_norm, w_out):
    bp, lp, _ = x_prompt.shape
    bs, ls, _ = x_sample.shape
    n_ctx = bp * lp
    assert lp % TOKEN_TILE == 0 and ls % TOKEN_TILE == 0 and n_ctx % ls == 0
    assert ls // GRID_W == S5_T and lp % S5_T == 0 and n_ctx % S5_XROWS == 0 and (bs * ls) % S5_XROWS == 0
    ctx_tiles = n_ctx // TOKEN_TILE
    tiles_per_latent = ls // TOKEN_TILE

    cond = jnp.concatenate([c_ctx[None].astype(F32), c.astype(F32)], axis=0)
    rows = -(-cond.shape[0] // 8) * 8
    cond = jnp.pad(cond, ((0, rows - cond.shape[0]), (0, 0)))
    ada = _adaln(cond, w_ada.astype(F32), b_ada.astype(F32)).reshape(DEPTH, rows, 1, 3 * D_MODEL)

    w_main, w_small = _wprep(w_in.astype(F32))
    w_out_bf = w_out.astype(BF16)
    glu_w_bf = s5_glu_w.astype(BF16)
    gw_pad = jnp.zeros((DEPTH, 2, LANE, GLA_QK), F32)
    for d in range(2):
        gw_pad = gw_pad.at[:, d, SM_GLR + d * GLA_LR:SM_GLR + (d + 1) * GLA_LR, :].set(gla_gate_w[:, d].astype(F32))
    gb = gla_gate_b.astype(F32).reshape(DEPTH, 2, 1, GLA_QK)
    gdn_ab = jnp.concatenate([gdn_a_log, gdn_dt_bias], axis=-1).astype(F32)
    conv_w = gdn_conv.astype(F32)
    m_intra, w_st, w_ro, lam_t = jax.vmap(_s5_weights)(*(p.astype(F32) for p in (
        s5_lam_re, s5_lam_im, s5_log_dt, s5_b_re, s5_b_im, s5_c_re, s5_c_im)))
    h0_c = jnp.zeros((1, S5_G, bp, 4 * S5_P), F32)
    sre, sim = state_s5_re.astype(F32), state_s5_im.astype(F32)
    h0_s = jnp.concatenate([sre[:, :, 0], sim[:, :, 0], sre[:, :, 1], sim[:, :, 1]], axis=-1).transpose(1, 2, 0, 3)
    vec = lambda p: p.astype(F32).reshape(DEPTH, 1, -1)
    norm_pre_v, norm_post_v, gla_norm_v, gdn_norm_v = vec(norm_pre), vec(norm_post), vec(gla_norm), vec(gdn_norm)
    s5_d_v, glu_b_v = vec(s5_d), vec(s5_glu_b)
    st_gla, st_gdn = state_gla.astype(F32), state_gdn.astype(F32)

    x_c = x_prompt.reshape(n_ctx, D_MODEL).astype(F32)
    x_s = x_sample.reshape(bs * ls, D_MODEL).astype(F32)
    new_gla = jnp.zeros((bp, DEPTH, 2, GLA_H, GLA_DK, GLA_DV), F32)
    new_gdn = jnp.zeros((bp, DEPTH, 2, GDN_H, GDN_DK, GDN_DV), F32)
    s5_states = []
    lat0 = n_ctx // ls
    for l in range(DEPTH):
        pm, ps = _inproj(x_c, x_s, ada, norm_pre_v, w_main, w_small, l, ctx_tiles, tiles_per_latent)
        og_c, new_gla = _gla(pm, ps, gw_pad, gb, l, None, new_gla, nseq=bp, seq_len=lp, row_block0=0)
        og_s, = _gla(pm, ps, gw_pad, gb, l, st_gla, False, nseq=bs, seq_len=ls, row_block0=lat0)
        od_c, new_gdn = _gdn(pm, ps, gdn_ab, conv_w, l, None, new_gdn, nseq=bp, seq_len=lp, row_block0=0, period=lp)
        od_s, = _gdn(pm, ps, gdn_ab, conv_w, l, st_gdn, False, nseq=bs, seq_len=ls, row_block0=lat0,
                     period=GRID_W)
        y_c, fin = _s5(ps, m_intra, w_st, w_ro, lam_t, h0_c, l, 0, nseq=bp, n_chunks=lp // S5_T, row_block0=0,
                       latent=False)
        y_s, _ = _s5(ps, m_intra, w_st, w_ro, lam_t, h0_s, l, l, nseq=bs, n_chunks=GRID_W,
                     row_block0=n_ctx // S5_XROWS, latent=True)
        x_c, x_s = _out(x_c, x_s, ada, pm, ps, og_c, og_s, y_c, y_s, od_c, od_s, gla_norm_v, gdn_norm_v, s5_d_v,
                        glu_w_bf, glu_b_v, w_out_bf, norm_post_v, l, ctx_tiles, tiles_per_latent)
        fin = fin.transpose(1, 0, 2).reshape(bp, S5_G, 2, 2, S5_P)
        s5_states.append(fin.transpose(0, 2, 3, 1, 4))

    dt = x_prompt.dtype
    s5_all = jnp.stack(s5_states, axis=1)
    y_prompt = x_c.reshape(bp, lp, D_MODEL).astype(dt)
    y_sample = x_s.reshape(bs, ls, D_MODEL).astype(x_sample.dtype)
    return (y_prompt, y_sample, new_gla.astype(dt), s5_all[:, :, :, 0].astype(dt), s5_all[:, :, :, 1].astype(dt),
            new_gdn.astype(dt))
```

```python
import functools
import math

import jax
import jax.numpy as jnp
from jax import lax
from jax.experimental import pallas as pl
from jax.experimental.pallas import tpu as pltpu

F32 = jnp.float32
BF16 = jnp.bfloat16
HI = lax.Precision.HIGHEST

D_MODEL = 1024
DEPTH = 4
GRID_W = 64
CHUNK = 64
EPS = 1e-6
GLA_H, GLA_DK, GLA_DV, GLA_LR, GLA_TAU = 4, 64, 128, 16, 16.0
GLA_QK, GLA_W = GLA_H * GLA_DK, GLA_H * GLA_DV
S5_GH, S5_W, S5_P = 16, 512, 64
S5_G = S5_W // S5_GH
S5_T = 16
GDN_H, GDN_DK, GDN_DV = 4, 128, 128
GDN_W = GDN_H * GDN_DV
MIX_W = GLA_W + S5_W + GDN_W

LANE = 128
TOKEN_TILE = 256
VMEM_LIMIT = 48 * 1024 * 1024

COL_GQ, COL_GK, COL_GV, COL_GGATE, COL_SGATE = 0, 2, 4, 8, 12
COL_DQ, COL_DK, COL_DV, COL_DGATE = 16, 20, 24, 28
MAIN_W = 32 * LANE
COL_SU, COL_SM = 0, 4
SIDE_W = 5 * LANE
SM_GLR, SM_DA, SM_DB = 0, 32, 40


def _dot(a, b, precision=None):
    return lax.dot_general(a, b, (((1,), (0,)), ((), ())), precision=precision,
                           preferred_element_type=F32)


def _dot_nt(a, b, precision=None):
    return lax.dot_general(a, b, (((1,), (1,)), ((), ())), precision=precision,
                           preferred_element_type=F32)


def _dot_tn(a, b, precision=None):
    return lax.dot_general(a, b, (((0,), (0,)), ((), ())), precision=precision,
                           preferred_element_type=F32)


def _split_bf16(x):
    hi = x.astype(BF16)
    return hi, (x - hi.astype(F32)).astype(BF16)


def _dot3(a, b):
    a_hi, a_lo = _split_bf16(a)
    b_hi, b_lo = _split_bf16(b)
    return _dot(a_hi, b_hi) + (_dot(a_hi, b_lo) + _dot(a_lo, b_hi))


def _silu(x):
    return x * jax.nn.sigmoid(x)


def _softplus(x):
    return jnp.maximum(x, 0.0) + jnp.log1p(jnp.exp(-jnp.abs(x)))


def _params(sem):
    return pltpu.CompilerParams(dimension_semantics=sem, vmem_limit_bytes=VMEM_LIMIT)


def _adaln_kernel(cond_ref, w_ref, b_ref, o_ref):
    c = cond_ref[...]
    o_ref[0] = _dot(_silu(c), w_ref[0], precision=HI) + b_ref[0]


def _adaln(cond, w_ada, b_ada):
    rows = cond.shape[0]
    nj = 3 * D_MODEL // 1024
    return pl.pallas_call(
        _adaln_kernel,
        out_shape=jax.ShapeDtypeStruct((DEPTH, rows, 3 * D_MODEL), F32),
        grid=(DEPTH, nj),
        in_specs=[pl.BlockSpec((rows, D_MODEL), lambda l, j: (0, 0)),
                  pl.BlockSpec((1, D_MODEL, 1024), lambda l, j: (l, 0, j)),
                  pl.BlockSpec((1, 1, 1024), lambda l, j: (l, 0, j))],
        out_specs=pl.BlockSpec((1, rows, 1024), lambda l, j: (l, 0, j)),
        compiler_params=_params(("arbitrary", "arbitrary")),
        name="adaln",
    )(cond, w_ada, b_ada.reshape(DEPTH, 1, 3 * D_MODEL))


def _wprep_kernel(w_ref, m_ref, s_ref):
    x = w_ref[0]
    o = _W_IN_OFFS
    main = jnp.concatenate([x[:, o['gq'][0]:o['gv'][1]], x[:, o['ggate'][0]:o['ggate'][1]],
                            x[:, o['sgate'][0]:o['dqkv'][1]], x[:, o['dgate'][0]:o['dgate'][1]]], axis=1)
    side = jnp.concatenate([x[:, o['su'][0]:o['su'][1]], x[:, o['glr'][0]:o['glr'][1]], x[:, o['da'][0]:o['db'][1]],
                            jnp.zeros((x.shape[0], LANE - 2 * GLA_LR - 4 * GDN_H), x.dtype)], axis=1)
    m_ref[0] = main.astype(BF16)
    s_ref[0] = side.astype(BF16)


def _w_in_offsets():
    offs, pos = {}, 0
    for name, width in (('gq', GLA_QK), ('gk', GLA_QK), ('gv', GLA_W), ('glr', 2 * GLA_LR), ('ggate', GLA_W),
                        ('su', S5_W), ('sgate', S5_W), ('dqkv', 3 * GDN_W), ('da', 2 * GDN_H),
                        ('db', 2 * GDN_H), ('dgate', GDN_W)):
        offs[name] = (pos, pos + width)
        pos += width
    return offs


_W_IN_OFFS = _w_in_offsets()


def _wprep(w_in):
    depth, d_model, in_dim = w_in.shape
    rows = 256
    return pl.pallas_call(
        _wprep_kernel,
        out_shape=(jax.ShapeDtypeStruct((depth, d_model, MAIN_W), BF16),
                   jax.ShapeDtypeStruct((depth, d_model, SIDE_W), BF16)),
        grid=(depth, d_model // rows),
        in_specs=[pl.BlockSpec((1, rows, in_dim), lambda l, i: (l, i, 0))],
        out_specs=(pl.BlockSpec((1, rows, MAIN_W), lambda l, i: (l, i, 0)),
                   pl.BlockSpec((1, rows, SIDE_W), lambda l, i: (l, i, 0))),
        compiler_params=_params(("arbitrary", "arbitrary")),
        name="wprep",
    )(w_in)


def _inproj_kernel(xc_ref, xs_ref, ada_ref, g_ref, wm_ref, ws_ref, om_ref, os_ref, *, ctx_tiles):
    x = jnp.where(pl.program_id(0) < ctx_tiles, xc_ref[...], xs_ref[...])
    nrm = x * lax.rsqrt(jnp.mean(x * x, axis=-1, keepdims=True) + EPS) * g_ref[...]
    ada = ada_ref[0]
    shift = ada[:, 0:D_MODEL]
    scale = ada[:, D_MODEL:2 * D_MODEL]
    h = (nrm * (1.0 + scale) + shift).astype(BF16)
    om_ref[...] = _dot(h, wm_ref[...]).astype(BF16)
    os_ref[...] = _dot(h, ws_ref[...])


def _ada_row(i, ctx_tiles, tiles_per_latent):
    return jnp.where(i < ctx_tiles, 0, 1 + (i - ctx_tiles) // tiles_per_latent)


def _ctx_tile(i, ctx_tiles):
    return jnp.minimum(i, ctx_tiles - 1)


def _lat_tile(i, ctx_tiles):
    return jnp.maximum(i - ctx_tiles, 0)


def _inproj(x_c, x_s, ada, norm_pre, w_main, w_small, layer, ctx_tiles, tiles_per_latent):
    nt = x_c.shape[0] + x_s.shape[0]
    row = functools.partial(_ada_row, ctx_tiles=ctx_tiles, tiles_per_latent=tiles_per_latent)
    return pl.pallas_call(
        functools.partial(_inproj_kernel, ctx_tiles=ctx_tiles),
        out_shape=(jax.ShapeDtypeStruct((nt, MAIN_W), BF16), jax.ShapeDtypeStruct((nt, SIDE_W), F32)),
        grid=(nt // TOKEN_TILE,),
        in_specs=[pl.BlockSpec((TOKEN_TILE, D_MODEL), lambda i: (_ctx_tile(i, ctx_tiles), 0)),
                  pl.BlockSpec((TOKEN_TILE, D_MODEL), lambda i: (_lat_tile(i, ctx_tiles), 0)),
                  pl.BlockSpec((None, 1, 1, 3 * D_MODEL), lambda i: (layer, row(i), 0, 0)),
                  pl.BlockSpec((None, 1, D_MODEL), lambda i: (layer, 0, 0)),
                  pl.BlockSpec((None, D_MODEL, MAIN_W), lambda i: (layer, 0, 0)),
                  pl.BlockSpec((None, D_MODEL, SIDE_W), lambda i: (layer, 0, 0))],
        out_specs=(pl.BlockSpec((TOKEN_TILE, MAIN_W), lambda i: (i, 0)),
                   pl.BlockSpec((TOKEN_TILE, SIDE_W), lambda i: (i, 0))),
        compiler_params=_params(("arbitrary",)),
        name="inproj",
    )(x_c, x_s, ada, norm_pre, w_main, w_small)


def _tri_mask(n, reverse, strict=False):
    r = lax.broadcasted_iota(jnp.int32, (n, n), 0)
    c = lax.broadcasted_iota(jnp.int32, (n, n), 1)
    if reverse:
        return (r < c) if strict else (r <= c)
    return (r > c) if strict else (r >= c)


def _eye_mask(n):
    return lax.broadcasted_iota(jnp.int32, (n, n), 0) == lax.broadcasted_iota(jnp.int32, (n, n), 1)


GLA_CPI = 4


def _state_refs(rest, has_state, emit_state, aliased):
    rest = list(rest)
    s0_ref = rest.pop(0) if has_state else None
    if aliased:
        rest.pop(0)
    o_ref = rest.pop(0)
    sf_ref = rest.pop(0) if emit_state else None
    return s0_ref, o_ref, sf_ref, rest


def _gla_kernel(q_ref, k_ref, v_ref, sm_ref, gw_ref, gb_ref, *rest, n_chunks, has_state, emit_state, aliased):
    s0_ref, o_ref, sf_ref, (g_s, ob_s, st_ref) = _state_refs(rest, has_state, emit_state, aliased)
    for d in range(2):
        if has_state:
            s0 = jnp.concatenate([s0_ref[0, d, 0], s0_ref[0, d, 1]], axis=0)
            st_ref[d] = s0.T
        else:
            st_ref[d] = jnp.zeros((GLA_DV, LANE), F32)
        z = _dot3(sm_ref[...], gw_ref[d]) + gb_ref[d]
        g_s[d] = -_softplus(-z) * (1.0 / GLA_TAU)

    lane = lax.broadcasted_iota(jnp.int32, (CHUNK, LANE), 1)
    head_mask = [lane < GLA_DK, lane >= GLA_DK]
    scale = GLA_DK ** -0.5
    causal = [_tri_mask(CHUNK, False), _tri_mask(CHUNK, True)]
    causal_bf = [m.astype(BF16) for m in causal]

    def chunk_step(it, carry):
        jd = [(j, d) for j in range(GLA_CPI) for d in range(2)]
        jdh = [(j, d, h) for (j, d) in jd for h in range(2)]
        cc = {(j, d): (it * GLA_CPI + j) if d == 0 else n_chunks - 1 - (it * GLA_CPI + j) for (j, d) in jd}
        rows = {u: pl.ds(pl.multiple_of(cc[u] * CHUNK, CHUNK), CHUNK) for u in jd}
        q = {u: q_ref[rows[u], :].astype(F32) for u in jd}
        k = {u: k_ref[rows[u], :].astype(F32) for u in jd}
        v = {u: v_ref[rows[u], :] for u in jd}
        g = {(j, d): g_s[d, rows[j, d], :] for (j, d) in jd}
        g_hi = {u: g[u].astype(BF16) for u in jd}
        g_lo = {u: (g[u] - g_hi[u].astype(F32)).astype(BF16) for u in jd}
        b = {u: _dot(causal_bf[u[1]], g_hi[u]) + _dot(causal_bf[u[1]], g_lo[u]) for u in jd}
        b_last = {(j, d): b[j, d][CHUNK - 1:CHUNK, :] if d == 0 else b[j, d][0:1, :] for (j, d) in jd}
        qe = {u: q[u] * jnp.exp(b[u]) * scale for u in jd}
        ke = {u: (k[u] * jnp.exp(-b[u])).astype(BF16) for u in jd}
        kd = {u: k[u] * jnp.exp(b_last[u] - b[u]) for u in jd}
        qh = {(j, d, h): jnp.where(head_mask[h], qe[j, d], 0.0).astype(BF16) for (j, d, h) in jdh}
        vh = {(j, d, h): v[j, d][:, h * GLA_DV:(h + 1) * GLA_DV].astype(BF16) for (j, d, h) in jdh}
        sc = {(j, d, h): jnp.where(causal[d], _dot_nt(qh[j, d, h], ke[j, d]), 0.0).astype(BF16)
              for (j, d, h) in jdh}
        upd = {(j, d, h): _dot_tn(vh[j, d, h], jnp.where(head_mask[h], kd[j, d], 0.0).astype(BF16))
               for (j, d, h) in jdh}
        o_intra = {u: _dot(sc[u], vh[u]) for u in jdh}
        for j in range(GLA_CPI):
            st_bf = {d: st_ref[d].astype(BF16) for d in range(2)}
            for d in range(2):
                for h in range(2):
                    oh = o_intra[j, d, h] + _dot_nt(qh[j, d, h], st_bf[d])
                    ob_s[d, rows[j, d], h * GLA_DV:(h + 1) * GLA_DV] = oh
                st_ref[d] = st_ref[d] * jnp.exp(b_last[j, d]) + upd[j, d, 0] + upd[j, d, 1]
        return carry

    lax.fori_loop(0, n_chunks // GLA_CPI, chunk_step, 0)
    o_ref[...] = (ob_s[0] + ob_s[1]).astype(o_ref.dtype)
    if emit_state:
        for d in range(2):
            s = st_ref[d].T
            for h in range(2):
                sf_ref[0, d, h] = s[h * GLA_DK:(h + 1) * GLA_DK, :]


def _state_plumbing(layer, s0, sf_acc, nseq, heads, hps, dk, dv, n_lead):
    blk = (1, None, 2, hps, dk, dv)
    imap = lambda b, p, *_: (b, layer, 0, p, 0, 0)
    in_specs, operands, out_specs, out_shapes, aliases = [], [], [], [], {}
    if s0 is not None:
        in_specs.append(pl.BlockSpec(blk, imap))
        operands.append(s0)
    emit = sf_acc is not False
    if emit:
        if sf_acc is not None:
            aliases[n_lead + len(operands)] = 1
            in_specs.append(pl.BlockSpec(memory_space=pl.ANY))
            operands.append(sf_acc)
        out_specs.append(pl.BlockSpec(blk, imap))
        out_shapes.append(jax.ShapeDtypeStruct((nseq, DEPTH, 2, heads, dk, dv), F32))
    flags = dict(has_state=s0 is not None, emit_state=emit, aliased=emit and sf_acc is not None)
    return in_specs, operands, out_specs, out_shapes, aliases, flags


def _gla(pm, ps, gw_pad, gb, layer, s0, sf_acc, *, nseq, seq_len, row_block0):
    st_in, st_ops, st_out, st_shapes, aliases, flags = _state_plumbing(
        layer, s0, sf_acc, nseq, GLA_H, 2, GLA_DK, GLA_DV, n_lead=6)
    kern = functools.partial(_gla_kernel, n_chunks=seq_len // CHUNK, **flags)
    return pl.pallas_call(
        kern,
        out_shape=[jax.ShapeDtypeStruct((nseq * seq_len, GLA_W), BF16)] + st_shapes,
        grid=(nseq, 2),
        in_specs=[pl.BlockSpec((seq_len, LANE), lambda b, p: (row_block0 + b, COL_GQ + p)),
                  pl.BlockSpec((seq_len, LANE), lambda b, p: (row_block0 + b, COL_GK + p)),
                  pl.BlockSpec((seq_len, 2 * LANE), lambda b, p: (row_block0 + b, COL_GV // 2 + p)),
                  pl.BlockSpec((seq_len, LANE), lambda b, p: (row_block0 + b, COL_SM)),
                  pl.BlockSpec((None, 2, LANE, LANE), lambda b, p: (layer, 0, 0, p)),
                  pl.BlockSpec((None, 2, 1, LANE), lambda b, p: (layer, 0, 0, p))] + st_in,
        out_specs=[pl.BlockSpec((seq_len, 2 * LANE), lambda b, p: (b, p))] + st_out,
        scratch_shapes=[pltpu.VMEM((2, seq_len, LANE), F32), pltpu.VMEM((2, seq_len, 2 * LANE), F32),
                        pltpu.VMEM((2, GLA_DV, LANE), F32)],
        input_output_aliases=aliases,
        compiler_params=_params(("arbitrary", "arbitrary")),
        name="gla",
    )(pm, pm, pm, ps, gw_pad, gb, *st_ops)


GDN_HPS = 4
GDN_CPI = 4
assert GDN_CPI % 2 == 0 and 2 * CHUNK == LANE


def _block_mask(n, s, reverse):
    r = lax.broadcasted_iota(jnp.int32, (n, n), 0)
    c = lax.broadcasted_iota(jnp.int32, (n, n), 1)
    if reverse:
        r, c = c, r
    sh = s.bit_length() - 1
    same_pair = (r >> (sh + 1)) == (c >> (sh + 1))
    return same_pair & (((r >> sh) & 1) == 1) & (((c >> sh) & 1) == 0)


def _gdn_kernel(ab_ref, q_ref, k_ref, v_ref, sm_ref, cwq_ref, cwk_ref, cwv_ref, *rest,
                layer, n_chunks, period, has_state, emit_state, aliased):
    s0_ref, o_ref, sf_ref, scratch = _state_refs(rest, has_state, emit_state, aliased)
    q_s, k_s, v_s, g_s, b_s, u_s, wq_s, a_s, kd_s, dl_s, ob_s, st_ref = scratch
    hp = pl.program_id(1)
    seq_len = n_chunks * CHUNK
    row = lax.broadcasted_iota(jnp.int32, (seq_len, LANE), 0)
    first = (row % period) == 0
    last = (row % period) == period - 1

    def conv_silu(x, w):
        xp = jnp.where(first, 0.0, pltpu.roll(x, 1, 0))
        xn = jnp.where(last, 0.0, pltpu.roll(x, seq_len - 1, 0))
        return _silu(xp * w[0:1, :] + x * w[1:2, :] + xn * w[2:3, :])

    def l2norm(x):
        return x * lax.rsqrt(jnp.sum(x * x, axis=-1, keepdims=True) + EPS)

    lane1 = lax.broadcasted_iota(jnp.int32, (1, LANE), 1)
    a_log = jnp.zeros((1, LANE), F32)
    dt_bias = jnp.zeros((1, LANE), F32)
    for d in range(2):
        for hh in range(GDN_H):
            a_log = jnp.where(lane1 == SM_DA + d * GDN_H + hh, ab_ref[layer, d, hh], a_log)
            dt_bias = jnp.where(lane1 == SM_DA + d * GDN_H + hh, ab_ref[layer, d, GDN_H + hh], dt_bias)
    sm = sm_ref[...]
    g_all = -jnp.exp(a_log) * _softplus(sm + dt_bias)
    b_all = jax.nn.sigmoid(sm)
    lane = lax.broadcasted_iota(jnp.int32, (seq_len, LANE), 1)
    in_chunk = row % CHUNK

    def chunk_cumsum(x, reverse):
        sh = 1
        while sh < CHUNK:
            if reverse:
                x = x + jnp.where(in_chunk < CHUNK - sh, pltpu.roll(x, seq_len - sh, 0), 0.0)
            else:
                x = x + jnp.where(in_chunk >= sh, pltpu.roll(x, sh, 0), 0.0)
            sh *= 2
        return x

    gc_all = [chunk_cumsum(g_all, False), chunk_cumsum(g_all, True)]

    def lane_bcast(x, j):
        col = jnp.sum(jnp.where(lane == j, x, 0.0), axis=1, keepdims=True)
        return jnp.broadcast_to(col, (seq_len, LANE))

    for h in range(GDN_HPS):
        hd = hp * GDN_HPS + h
        lanes = slice(h * LANE, (h + 1) * LANE)
        q_s[h] = l2norm(conv_silu(q_ref[:, lanes].astype(F32), cwq_ref[:, lanes])) * GDN_DK ** -0.5
        k_s[h] = l2norm(conv_silu(k_ref[:, lanes].astype(F32), cwk_ref[:, lanes]))
        v_s[h] = conv_silu(v_ref[:, lanes].astype(F32), cwv_ref[:, lanes])
        for d in range(2):
            g_s[h, d] = lane_bcast(gc_all[d], SM_DA + d * GDN_H + hd)
            b_s[h, d] = lane_bcast(b_all, SM_DB + d * GDN_H + hd)
            if has_state:
                st_ref[h, d] = s0_ref[0, d, h]
            else:
                st_ref[h, d] = jnp.zeros((GDN_DK, GDN_DV), F32)

    causal = [_tri_mask(CHUNK, False), _tri_mask(CHUNK, True)]
    strict_f = [_tri_mask(CHUNK, d == 1, strict=True).astype(F32) for d in range(2)]
    level_masks = [[_block_mask(CHUNK, 1 << j, d == 1).astype(F32) for j in range(CHUNK.bit_length() - 1)]
                   for d in range(2)]

    eye_f = _eye_mask(CHUNK).astype(F32)

    def phase_a(it, carry):
        cs = [it * GDN_CPI + j for j in range(GDN_CPI)]
        rows = [pl.ds(pl.multiple_of(c * CHUNK, CHUNK), CHUNK) for c in cs]
        rows2 = [pl.ds(pl.multiple_of(c * 2 * CHUNK, 2 * CHUNK), CHUNK) for c in cs]
        rows2b = [pl.ds(pl.multiple_of(c * 2 * CHUNK + CHUNK, CHUNK), CHUNK) for c in cs]
        pairs = [(j, h) for j in range(GDN_CPI) for h in range(GDN_HPS)]
        units = [(j, h, d) for (j, h) in pairs for d in range(2)]
        q = {(j, h): q_s[h, rows[j], :] for (j, h) in pairs}
        k = {(j, h): k_s[h, rows[j], :] for (j, h) in pairs}
        v = {(j, h): v_s[h, rows[j], :] for (j, h) in pairs}
        k_bf = {p: k[p].astype(BF16) for p in pairs}
        beta = {(j, h, d): b_s[h, d, rows[j], :] for (j, h, d) in units}
        gc = {(j, h, d): g_s[h, d, rows[j], :] for (j, h, d) in units}
        gc_t = {}
        for h in range(GDN_HPS):
            for d in range(2):
                for j0 in range(0, GDN_CPI, 2):
                    blk_t = jnp.concatenate([gc[j0, h, d], gc[j0 + 1, h, d]], axis=0).T
                    for j in range(2):
                        gc_t[j0 + j, h, d] = blk_t[0:CHUNK, j * CHUNK:(j + 1) * CHUNK]
        qk = {p: _dot_nt(q[p].astype(BF16), k_bf[p]) for p in pairs}
        kk = {p: _dot_nt(k_bf[p], k_bf[p]) for p in pairs}
        decay, m, t = {}, {}, {}
        for u in units:
            j, h, d = u
            diff = gc[u][:, :CHUNK] - gc_t[u]
            decay[u] = jnp.where(causal[d], jnp.exp(jnp.where(causal[d], diff, 0.0)), 0.0)
            m[u] = kk[j, h] * beta[u][:, :CHUNK] * decay[u] * strict_f[d]
            t[u] = eye_f - m[u] * level_masks[d][0]
        for lvl in range(1, len(level_masks[0])):
            t_bf = {u: t[u].astype(BF16) for u in units}
            p1 = {u: _dot(t_bf[u], (m[u] * level_masks[u[2]][lvl]).astype(BF16)).astype(BF16) for u in units}
            t = {u: t[u] - _dot(p1[u], t_bf[u]) for u in units}
        egc = {u: jnp.exp(gc[u]) for u in units}
        rhs = {(j, h, d): jnp.concatenate([v[j, h] * beta[j, h, d], k[j, h] * (beta[j, h, d] * egc[j, h, d])],
                                          axis=1).astype(BF16) for (j, h, d) in units}
        uw = {u: _dot(t[u].astype(BF16), rhs[u]) for u in units}
        for u in units:
            j, h, d = u
            gc_last = gc[u][CHUNK - 1:CHUNK, :] if d == 0 else gc[u][0:1, :]
            u_s[h, d, rows[j], :] = uw[u][:, 0:GDN_DV]
            wq_s[h, d, rows2[j], :] = uw[u][:, GDN_DV:GDN_DV + GDN_DK].astype(BF16)
            wq_s[h, d, rows2b[j], :] = (q[j, h] * egc[u]).astype(BF16)
            a_s[h, d, rows[j], :] = (qk[j, h] * decay[u]).astype(BF16)
            kd_s[h, d, rows[j], :] = (k[j, h] * jnp.exp(gc_last - gc[u])).astype(BF16)
            dl_s[h, d, cs[j]] = jnp.broadcast_to(jnp.exp(gc_last), (8, LANE))
        return carry

    lax.fori_loop(0, n_chunks // GDN_CPI, phase_a, 0)

    def phase_b(i, carry):
        chains = [(h, d) for h in range(GDN_HPS) for d in range(2)]
        cc = {0: i, 1: n_chunks - 1 - i}
        rows = {d: pl.ds(pl.multiple_of(cc[d] * CHUNK, CHUNK), CHUNK) for d in range(2)}
        rows2 = {d: pl.ds(pl.multiple_of(cc[d] * 2 * CHUNK, 2 * CHUNK), 2 * CHUNK) for d in range(2)}
        s = {hd: st_ref[hd[0], hd[1]] for hd in chains}
        ws = {(h, d): _dot(wq_s[h, d, rows2[d], :], s[h, d].astype(BF16)) for (h, d) in chains}
        v_new = {(h, d): (u_s[h, d, rows[d], :] - ws[h, d][0:CHUNK, :]).astype(BF16) for (h, d) in chains}
        o = {(h, d): ws[h, d][CHUNK:2 * CHUNK, :] + _dot(a_s[h, d, rows[d], :], v_new[h, d]) for (h, d) in chains}
        upd = {(h, d): _dot_tn(kd_s[h, d, rows[d], :], v_new[h, d]) for (h, d) in chains}
        for (h, d) in chains:
            lanes = slice(h * LANE, (h + 1) * LANE)
            ob_s[d, rows[d], lanes] = o[h, d]
            st_ref[h, d] = dl_s[h, d, cc[d]][0:1, :] * s[h, d] + upd[h, d]
        return carry

    lax.fori_loop(0, n_chunks, phase_b, 0)
    o_ref[...] = (ob_s[0] + ob_s[1]).astype(o_ref.dtype)
    if emit_state:
        for h in range(GDN_HPS):
            for d in range(2):
                sf_ref[0, d, h] = st_ref[h, d]


def _gdn(pm, ps, ab, conv_w, layer, s0, sf_acc, *, nseq, seq_len, row_block0, period):
    hps = GDN_HPS
    st_in, st_ops, st_out, st_shapes, aliases, flags = _state_plumbing(
        layer, s0, sf_acc, nseq, GDN_H, hps, GDN_DK, GDN_DV, n_lead=8)
    n_chunks = seq_len // CHUNK
    kern = functools.partial(_gdn_kernel, layer=layer, n_chunks=n_chunks, period=period, **flags)
    wide = hps * LANE
    npair = GDN_H // hps
    grid_spec = pltpu.PrefetchScalarGridSpec(
        num_scalar_prefetch=1,
        grid=(nseq, npair),
        in_specs=[pl.BlockSpec((seq_len, wide), lambda b, p, ab: (row_block0 + b, COL_DQ // hps + p)),
                  pl.BlockSpec((seq_len, wide), lambda b, p, ab: (row_block0 + b, COL_DK // hps + p)),
                  pl.BlockSpec((seq_len, wide), lambda b, p, ab: (row_block0 + b, COL_DV // hps + p)),
                  pl.BlockSpec((seq_len, LANE), lambda b, p, ab: (row_block0 + b, COL_SM)),
                  pl.BlockSpec((None, 3, wide), lambda b, p, ab: (layer, 0, p)),
                  pl.BlockSpec((None, 3, wide), lambda b, p, ab: (layer, 0, npair + p)),
                  pl.BlockSpec((None, 3, wide), lambda b, p, ab: (layer, 0, 2 * npair + p))] + st_in,
        out_specs=[pl.BlockSpec((seq_len, wide), lambda b, p, ab: (b, p))] + st_out,
        scratch_shapes=[pltpu.VMEM((hps, seq_len, LANE), F32),
                        pltpu.VMEM((hps, seq_len, LANE), F32),
                        pltpu.VMEM((hps, seq_len, LANE), F32),
                        pltpu.VMEM((hps, 2, seq_len, LANE), F32),
                        pltpu.VMEM((hps, 2, seq_len, LANE), F32),
                        pltpu.VMEM((hps, 2, seq_len, GDN_DV), F32),
                        pltpu.VMEM((hps, 2, 2 * seq_len, GDN_DK), BF16),
                        pltpu.VMEM((hps, 2, seq_len, CHUNK), BF16),
                        pltpu.VMEM((hps, 2, seq_len, GDN_DK), BF16),
                        pltpu.VMEM((hps, 2, n_chunks, 8, LANE), F32),
                        pltpu.VMEM((2, seq_len, wide), F32),
                        pltpu.VMEM((hps, 2, GDN_DK, GDN_DV), F32)],
    )
    return pl.pallas_call(
        kern,
        out_shape=[jax.ShapeDtypeStruct((nseq * seq_len, GDN_W), BF16)] + st_shapes,
        grid_spec=grid_spec,
        input_output_aliases=aliases,
        compiler_params=_params(("arbitrary", "arbitrary")),
        name="gdn",
    )(ab, pm, pm, pm, ps, conv_w, conv_w, conv_w, *st_ops)


S5_GPB = LANE // S5_GH
S5_XROWS = 4096


def _lane_block_transpose(arrs, lane_grp):
    arrs = list(arrs)
    k = S5_GPB // 2
    while k:
        high = (lane_grp & k) != 0
        for i in range(S5_GPB):
            if not i & k:
                lo, hi = arrs[i], arrs[i + k]
                arrs[i] = jnp.where(high, pltpu.roll(hi, k * S5_GH, 1), lo)
                arrs[i + k] = jnp.where(high, hi, pltpu.roll(lo, LANE - k * S5_GH, 1))
        k //= 2
    return arrs


def _s5_kernel(*refs, n_chunks, nseq, latent):
    n_x = len(refs) - 11
    x_refs = refs[:n_x]
    mi_ref, wst_ref, wout_ref, lam_ref, h0_ref, y_ref, fin_ref, u_s, e_s, es_s, y_s = refs[n_x:]
    seq_len = n_chunks * S5_T
    seq_per_x = S5_XROWS // seq_len
    lane_grp = lax.broadcasted_iota(jnp.int32, (n_chunks, LANE), 1) >> 4

    def slab_rows(base, t):
        if latent:
            return pl.ds(pl.multiple_of(base + t * GRID_W, GRID_W), n_chunks)
        return pl.ds(base + t, n_chunks, stride=S5_T)

    def relayout_in(x_ref, b0):
        def body(bl, carry):
            base = bl * seq_len
            b = b0 + bl
            for lt in range(2):
                slabs = [x_ref[slab_rows(base, S5_GPB * lt + tt), :] for tt in range(S5_GPB)]
                for gi, tile in enumerate(_lane_block_transpose(slabs, lane_grp)):
                    u_s[gi, lt, pl.ds(b, n_chunks, stride=nseq), :] = tile
            return carry
        lax.fori_loop(0, seq_per_x, body, 0)

    for i, x_ref in enumerate(x_refs):
        relayout_in(x_ref, i * seq_per_x)

    half = 2 * S5_P

    def swap(x):
        return pltpu.roll(x, S5_P, 1)

    u_bf = [jnp.concatenate([u_s[gi, 0], u_s[gi, 1]], axis=1).astype(BF16) for gi in range(S5_GPB)]
    for gi in range(S5_GPB):
        e = _dot(u_bf[gi], wst_ref[gi])
        for d in range(2):
            e_s[gi, d] = e[:, d * half:(d + 1) * half]
            es_s[gi, d] = swap(e[:, d * half:(d + 1) * half])
    for gi in range(S5_GPB):
        y = _dot(u_bf[gi], mi_ref[gi])
        y_s[gi, 0] = y[:, 0:LANE]
        y_s[gi, 1] = y[:, LANE:2 * LANE]

    lam = [lam_ref[gi] for gi in range(S5_GPB)]

    def step(c, carry):
        rf = pl.ds(pl.multiple_of(c * nseq, nseq), nseq)
        rb = pl.ds(pl.multiple_of((n_chunks - 1 - c) * nseq, nseq), nseq)
        new = []
        for gi in range(S5_GPB):
            l = lam[gi]
            for d, rows in ((0, rf), (1, rb)):
                x, xs = carry[4 * gi + 2 * d], carry[4 * gi + 2 * d + 1]
                u_s[gi, d, rows, :] = x
                l1, l2 = l[2 * d:2 * d + 1, :], l[2 * d + 1:2 * d + 2, :]
                new.append(x * l1 + xs * l2 + e_s[gi, d, rows, :])
                new.append(xs * l1 - x * l2 + es_s[gi, d, rows, :])
        return tuple(new)

    init = []
    for gi in range(S5_GPB):
        h0 = h0_ref[gi]
        for d in range(2):
            x0 = h0[:, d * half:(d + 1) * half]
            init += [x0, swap(x0)]
    fin = lax.fori_loop(0, n_chunks, step, tuple(init))
    for gi in range(S5_GPB):
        fin_ref[gi, :, 0:half] = fin[4 * gi]
        fin_ref[gi, :, half:2 * half] = fin[4 * gi + 2]
        x_in = jnp.concatenate([u_s[gi, 0], u_s[gi, 1]], axis=1).astype(BF16)
        y = _dot(x_in, wout_ref[gi])
        y_s[gi, 0] += y[:, 0:LANE]
        y_s[gi, 1] += y[:, LANE:2 * LANE]

    def relayout_out(b, carry):
        base = b * seq_len
        for lt in range(2):
            tiles = [y_s[gi, lt, pl.ds(b, n_chunks, stride=nseq), :] for gi in range(S5_GPB)]
            for tt, slab in enumerate(_lane_block_transpose(tiles, lane_grp)):
                y_ref[slab_rows(base, S5_GPB * lt + tt), :] = slab
        return carry

    lax.fori_loop(0, nseq, relayout_out, 0)


def _s5(ps, m_intra, w_st, w_out, lam_t, h0, layer, h0_layer, *, nseq, n_chunks, row_block0, latent):
    rows = n_chunks * nseq
    n_tok = rows * S5_T
    n_x = n_tok // S5_XROWS
    kern = functools.partial(_s5_kernel, n_chunks=n_chunks, nseq=nseq, latent=latent)
    wspec = pl.BlockSpec((None, S5_GPB, 2 * LANE, 2 * LANE), lambda j: (layer, j, 0, 0))
    x_specs = [pl.BlockSpec((S5_XROWS, LANE), functools.partial(lambda j, i: (row_block0 + i, COL_SU + j), i=i))
               for i in range(n_x)]
    plane = pltpu.VMEM((S5_GPB, 2, rows, LANE), F32)
    return pl.pallas_call(
        kern,
        out_shape=(jax.ShapeDtypeStruct((n_tok, S5_W), F32),
                   jax.ShapeDtypeStruct((S5_G, nseq, 2 * LANE), F32)),
        grid=(S5_G // S5_GPB,),
        in_specs=x_specs + [wspec, wspec, wspec,
                            pl.BlockSpec((None, S5_GPB, 4, LANE), lambda j: (layer, j, 0, 0)),
                            pl.BlockSpec((None, S5_GPB, nseq, 2 * LANE), lambda j: (h0_layer, j, 0, 0))],
        out_specs=(pl.BlockSpec((n_tok, LANE), lambda j: (0, j)),
                   pl.BlockSpec((S5_GPB, nseq, 2 * LANE), lambda j: (j, 0, 0))),
        scratch_shapes=[plane, plane, plane, plane],
        compiler_params=_params(("arbitrary",)),
        name="s5",
    )(*([ps] * n_x), m_intra, w_st, w_out, lam_t, h0)


def _cmul(ar, ai, br, bi):
    return ar * br - ai * bi, ar * bi + ai * br


def _s5_weights(lam_re, lam_im, log_dt, b_re, b_im, c_re, c_im):
    t = S5_T
    dt = jnp.exp(log_dt)[..., None]
    tau = jnp.arange(t + 1, dtype=F32)[:, None, None, None]
    mag = jnp.exp(tau * (lam_re * dt)[None])
    ang = tau * (lam_im * dt)[None]
    pw_re, pw_im = mag * jnp.cos(ang), mag * jnp.sin(ang)
    nr, ni = pw_re[1] - 1.0, pw_im[1]
    den = lam_re * lam_re + lam_im * lam_im
    fr, fi = (nr * lam_re + ni * lam_im) / den, (ni * lam_re - nr * lam_im) / den
    bb_re, bb_im = _cmul(fr[..., None], fi[..., None], b_re[None], b_im[None])

    bt_re, bt_im = jnp.swapaxes(bb_re, -1, -2), jnp.swapaxes(bb_im, -1, -2)
    ct_re, ct_im = jnp.swapaxes(c_re, -1, -2), jnp.swapaxes(c_im, -1, -2)
    pwt_re, pwt_im = jnp.moveaxis(pw_re, 0, -1), jnp.moveaxis(pw_im, 0, -1)
    cl_re, cl_im = _cmul(pwt_re[..., None], pwt_im[..., None], ct_re[:, :, :, None, :], ct_im[:, :, :, None, :])

    def lanes(x):
        return x.reshape(S5_G, S5_P, t * S5_GH)

    def k_rows(d, taus):
        return (jnp.einsum('gip,gpx->gix', bt_re[d], lanes(cl_re[d][:, :, taus]), precision=HI)
                - jnp.einsum('gip,gpx->gix', bt_im[d], lanes(cl_im[d][:, :, taus]), precision=HI))

    width = t * S5_GH
    pad = (t - 1) * S5_GH
    kf = jnp.pad(k_rows(0, slice(0, t)), ((0, 0), (0, 0), (pad, 0)))
    kb = jnp.pad(k_rows(1, slice(t - 1, None, -1)), ((0, 0), (0, 0), (0, pad)))
    m_intra = jnp.stack([kf[:, :, pad - s * S5_GH:pad - s * S5_GH + width]
                         + kb[:, :, (t - 1 - s) * S5_GH:(t - 1 - s) * S5_GH + width] for s in range(t)], axis=1)
    m_intra = m_intra.reshape(S5_G, width, width)

    def st(d, taus):
        return _cmul(jnp.moveaxis(pw_re[taus, d], 0, 1)[:, :, None, :], jnp.moveaxis(pw_im[taus, d], 0, 1)[:, :, None, :],
                     bt_re[d][:, None], bt_im[d][:, None])
    w_st = jnp.concatenate(st(0, slice(t - 1, None, -1)) + st(1, slice(0, t)), axis=-1)
    w_st = w_st.reshape(S5_G, width, 4 * S5_P)

    w_out = jnp.concatenate([lanes(cl_re[0][:, :, 1:t + 1]), -lanes(cl_im[0][:, :, 1:t + 1]),
                             lanes(cl_re[1][:, :, t:0:-1]), -lanes(cl_im[1][:, :, t:0:-1])], axis=1)


    lt_re, lt_im = pw_re[t], pw_im[t]
    lam_t = jnp.stack([jnp.concatenate([lt_re[0], lt_re[0]], -1), jnp.concatenate([-lt_im[0], lt_im[0]], -1),
                       jnp.concatenate([lt_re[1], lt_re[1]], -1), jnp.concatenate([-lt_im[1], lt_im[1]], -1)],
                      axis=1)
    return m_intra.astype(BF16), w_st.astype(BF16), w_out.astype(BF16), lam_t


def _gelu_tanh(x):
    return 0.5 * x * (1.0 + jnp.tanh(math.sqrt(2.0 / math.pi) * (x + 0.044715 * (x * x * x))))


def _out_kernel(xc_ref, xs_ref, ada_ref, ogc_ref, ogs_ref, gg_ref, ysc_ref, yss_ref, su_ref, sg_ref,
                odc_ref, ods_ref, dg_ref, gn_ref, dn_ref, sd_ref, gw_ref, gb_ref, wo_ref, np_ref,
                oc_ref, os_ref, *, ctx_tiles):
    is_ctx = pl.program_id(0) < ctx_tiles

    def pick(c_ref, s_ref):
        return jnp.where(is_ctx, c_ref[...], s_ref[...]).astype(F32)

    def head_norm(o, g):
        parts = []
        for h in range(o.shape[1] // LANE):
            oh = o[:, h * LANE:(h + 1) * LANE]
            parts.append(oh * lax.rsqrt(jnp.mean(oh * oh, axis=-1, keepdims=True) + EPS) * g)
        return jnp.concatenate(parts, axis=1)

    o_gla = head_norm(pick(ogc_ref, ogs_ref), gn_ref[...]) * _silu(gg_ref[...].astype(F32))
    y = _gelu_tanh(pick(ysc_ref, yss_ref) + sd_ref[...] * su_ref[...])
    y = y * jax.nn.sigmoid(_dot(y.astype(BF16), gw_ref[...]) + gb_ref[...])
    o_s5 = y * _silu(sg_ref[...].astype(F32))
    o_gdn = head_norm(pick(odc_ref, ods_ref), dn_ref[...]) * _silu(dg_ref[...].astype(F32))
    out = (_dot(o_gla.astype(BF16), wo_ref[0:GLA_W, :])
           + _dot(o_s5.astype(BF16), wo_ref[GLA_W:GLA_W + S5_W, :])
           + _dot(o_gdn.astype(BF16), wo_ref[GLA_W + S5_W:MIX_W, :]))
    r = out * lax.rsqrt(jnp.mean(out * out, axis=-1, keepdims=True) + EPS) * np_ref[...]
    gate = ada_ref[0][:, 2 * D_MODEL:3 * D_MODEL]
    x_new = pick(xc_ref, xs_ref) + gate * r

    @pl.when(is_ctx)
    def _():
        oc_ref[...] = x_new

    @pl.when(jnp.logical_not(is_ctx))
    def _():
        os_ref[...] = x_new


def _out(x_c, x_s, ada, pm, ps, og_c, og_s, y_c, y_s, od_c, od_s, gla_norm, gdn_norm, s5_d, glu_w, glu_b, w_out,
         norm_post, layer, ctx_tiles, tiles_per_latent):
    nt = x_c.shape[0] + x_s.shape[0]
    row = functools.partial(_ada_row, ctx_tiles=ctx_tiles, tiles_per_latent=tiles_per_latent)
    wide = 4 * LANE

    def tok(col_block):
        return pl.BlockSpec((TOKEN_TILE, wide), lambda i: (i, col_block))

    def ctx(width):
        return pl.BlockSpec((TOKEN_TILE, width), lambda i: (_ctx_tile(i, ctx_tiles), 0))

    def lat(width):
        return pl.BlockSpec((TOKEN_TILE, width), lambda i: (_lat_tile(i, ctx_tiles), 0))

    def full(shape):
        return pl.BlockSpec((None,) + shape, lambda i: (layer,) + (0,) * len(shape))

    return pl.pallas_call(
        functools.partial(_out_kernel, ctx_tiles=ctx_tiles),
        out_shape=(jax.ShapeDtypeStruct(x_c.shape, F32), jax.ShapeDtypeStruct(x_s.shape, F32)),
        grid=(nt // TOKEN_TILE,),
        in_specs=[ctx(D_MODEL), lat(D_MODEL),
                  pl.BlockSpec((None, 1, 1, 3 * D_MODEL), lambda i: (layer, row(i), 0, 0)),
                  ctx(wide), lat(wide), tok(COL_GGATE // 4),
                  ctx(wide), lat(wide), tok(COL_SU // 4), tok(COL_SGATE // 4),
                  ctx(wide), lat(wide), tok(COL_DGATE // 4),
                  full((1, LANE)), full((1, LANE)), full((1, S5_W)), full((S5_W, S5_W)), full((1, S5_W)),
                  full((MIX_W, D_MODEL)), full((1, D_MODEL))],
        out_specs=(ctx(D_MODEL), lat(D_MODEL)),
        compiler_params=_params(("arbitrary",)),
        name="outproj",
    )(x_c, x_s, ada, og_c, og_s, pm, y_c, y_s, ps, pm, od_c, od_s, pm, gla_norm, gdn_norm, s5_d, glu_w, glu_b,
      w_out, norm_post)


def kernel(x_prompt, x_sample, c, state_gla, state_s5_re, state_s5_im, state_gdn, c_ctx, norm_pre, norm_post, w_ada, b_ada, w_in, gla_gate_w, gla_gate_b, gla_norm, s5_lam_re, s5_lam_im, s5_log_dt, s5_b_re, s5_b_im, s5_c_re, s5_c_im, s5_d, s5_glu_w, s5_glu_b, gdn_conv, gdn_a_log, gdn_dt_bias, gdn_norm, w_out):
    bp, lp, _ = x_prompt.shape
    bs, ls, _ = x_sample.shape
    n_ctx = bp * lp
    assert lp % TOKEN_TILE == 0 and ls % TOKEN_TILE == 0 and n_ctx % ls == 0
    assert ls // GRID_W == S5_T and lp % S5_T == 0 and n_ctx % S5_XROWS == 0 and (bs * ls) % S5_XROWS == 0
    ctx_tiles = n_ctx // TOKEN_TILE
    tiles_per_latent = ls // TOKEN_TILE

    cond = jnp.concatenate([c_ctx[None].astype(F32), c.astype(F32)], axis=0)
    rows = -(-cond.shape[0] // 8) * 8
    cond = jnp.pad(cond, ((0, rows - cond.shape[0]), (0, 0)))
    ada = _adaln(cond, w_ada.astype(F32), b_ada.astype(F32)).reshape(DEPTH, rows, 1, 3 * D_MODEL)

    w_main, w_small = _wprep(w_in.astype(F32))
    w_out_bf = w_out.astype(BF16)
    glu_w_bf = s5_glu_w.astype(BF16)
    gw_pad = jnp.zeros((DEPTH, 2, LANE, GLA_QK), F32)
    for d in range(2):
        gw_pad = gw_pad.at[:, d, SM_GLR + d * GLA_LR:SM_GLR + (d + 1) * GLA_LR, :].set(gla_gate_w[:, d].astype(F32))
    gb = gla_gate_b.astype(F32).reshape(DEPTH, 2, 1, GLA_QK)
    gdn_ab = jnp.concatenate([gdn_a_log, gdn_dt_bias], axis=-1).astype(F32)
    conv_w = gdn_conv.astype(F32)
    m_intra, w_st, w_ro, lam_t = jax.vmap(_s5_weights)(*(p.astype(F32) for p in (
        s5_lam_re, s5_lam_im, s5_log_dt, s5_b_re, s5_b_im, s5_c_re, s5_c_im)))
    h0_c = jnp.zeros((1, S5_G, bp, 4 * S5_P), F32)
    sre, sim = state_s5_re.astype(F32), state_s5_im.astype(F32)
    h0_s = jnp.concatenate([sre[:, :, 0], sim[:, :, 0], sre[:, :, 1], sim[:, :, 1]], axis=-1).transpose(1, 2, 0, 3)
    vec = lambda p: p.astype(F32).reshape(DEPTH, 1, -1)
    norm_pre_v, norm_post_v, gla_norm_v, gdn_norm_v = vec(norm_pre), vec(norm_post), vec(gla_norm), vec(gdn_norm)
    s5_d_v, glu_b_v = vec(s5_d), vec(s5_glu_b)
    st_gla, st_gdn = state_gla.astype(F32), state_gdn.astype(F32)

    x_c = x_prompt.reshape(n_ctx, D_MODEL).astype(F32)
    x_s = x_sample.reshape(bs * ls, D_MODEL).astype(F32)
    new_gla = jnp.zeros((bp, DEPTH, 2, GLA_H, GLA_DK, GLA_DV), F32)
    new_gdn = jnp.zeros((bp, DEPTH, 2, GDN_H, GDN_DK, GDN_DV), F32)
    s5_states = []
    lat0 = n_ctx // ls
    for l in range(DEPTH):
        pm, ps = _inproj(x_c, x_s, ada, norm_pre_v, w_main, w_small, l, ctx_tiles, tiles_per_latent)
        og_c, new_gla = _gla(pm, ps, gw_pad, gb, l, None, new_gla, nseq=bp, seq_len=lp, row_block0=0)
        og_s, = _gla(pm, ps, gw_pad, gb, l, st_gla, False, nseq=bs, seq_len=ls, row_block0=lat0)
        od_c, new_gdn = _gdn(pm, ps, gdn_ab, conv_w, l, None, new_gdn, nseq=bp, seq_len=lp, row_block0=0, period=lp)
        od_s, = _gdn(pm, ps, gdn_ab, conv_w, l, st_gdn, False, nseq=bs, seq_len=ls, row_block0=lat0,
                     period=GRID_W)
        y_c, fin = _s5(ps, m_intra, w_st, w_ro, lam_t, h0_c, l, 0, nseq=bp, n_chunks=lp // S5_T, row_block0=0,
                       latent=False)
        y_s, _ = _s5(ps, m_intra, w_st, w_ro, lam_t, h0_s, l, l, nseq=bs, n_chunks=GRID_W,
                     row_block0=n_ctx // S5_XROWS, latent=True)
        x_c, x_s = _out(x_c, x_s, ada, pm, ps, og_c, og_s, y_c, y_s, od_c, od_s, gla_norm_v, gdn_norm_v, s5_d_v,
                        glu_w_bf, glu_b_v, w_out_bf, norm_post_v, l, ctx_tiles, tiles_per_latent)
        fin = fin.transpose(1, 0, 2).reshape(bp, S5_G, 2, 2, S5_P)
        s5_states.append(fin.transpose(0, 2, 3, 1, 4))

    dt = x_prompt.dtype
    s5_all = jnp.stack(s5_states, axis=1)
    y_prompt = x_c.reshape(bp, lp, D_MODEL).astype(dt)
    y_sample = x_s.reshape(bs, ls, D_MODEL).astype(x_sample.dtype)
    return (y_prompt, y_sample, new_gla.astype(dt), s5_all[:, :, :, 0].astype(dt), s5_all[:, :, :, 1].astype(dt),
            new_gdn.astype(dt))
```
